```python
import math
import jax, jax.numpy as jnp
from jax import lax
import numpy as np

D_MODEL = 2048
BATCH = 2
SEQ = 4096
DEPTH = 4
DEC_BATCH = 8
DEC_SEQ = 4
PAST_LEN = 16384
PAGE_SIZE = 128

N_MIXERS = 2
N_LRU = (DEPTH + 1) // 2
N_ATTN = DEPTH // 2
LRU_WIDTH = D_MODEL
LRU_HEADS = 8
LRU_HEAD_DIM = LRU_WIDTH // LRU_HEADS
LRU_CONV = 4
LRU_C = 8.0
ATTN_GROUPS = ((128, 1), (512, 4), (2048, 16))
N_GROUPS = len(ATTN_GROUPS)
GROUP_HEADS = 8
HEAD_DIM = 128
ATTN_WIDTH = GROUP_HEADS * HEAD_DIM
QKV_WIDTH = N_GROUPS * 3 * GROUP_HEADS * HEAD_DIM
D_FF = 3 * D_MODEL
FFN_CONV = 3
EPS = 1e-6
NEG = -1e30

kernel_name = 'hybrid_rglru_dilated_swa_convffn_step'


def rms_norm(x, g):
    xf = x.astype(jnp.float32)
    y = xf * lax.rsqrt(jnp.mean(xf * xf, axis=-1, keepdims=True) + EPS) * g.astype(jnp.float32)
    return y.astype(x.dtype)


def causal_dwconv(ext, w, b):
    K = w.shape[0]
    L = ext.shape[1] - K + 1
    out = b
    for k in range(K):
        out = out + ext[:, k:k + L] * w[k]
    return out


def _lin_comb(left, right):
    a1, b1 = left
    a2, b2 = right
    return a1 * a2, a2 * b1 + b2


def rg_lru(u, h0, w_a, b_a, w_i, b_i, lam):
    B, L, W = u.shape
    uf = u.astype(jnp.float32)
    ub = uf.reshape(B, L, LRU_HEADS, LRU_HEAD_DIM)
    r = jax.nn.sigmoid(jnp.einsum('blhi,hij->blhj', ub, w_a.astype(jnp.float32)).reshape(B, L, W) + b_a.astype(jnp.float32))
    ig = jax.nn.sigmoid(jnp.einsum('blhi,hij->blhj', ub, w_i.astype(jnp.float32)).reshape(B, L, W) + b_i.astype(jnp.float32))
    log_a = -LRU_C * r * jax.nn.softplus(-lam.astype(jnp.float32))
    a = jnp.exp(log_a)
    bx = jnp.sqrt(-jnp.expm1(2.0 * log_a)) * (ig * uf)
    bx = bx.at[:, 0].add(a[:, 0] * h0.astype(jnp.float32))
    _, h = lax.associative_scan(_lin_comb, (a, bx), axis=1)
    return h.astype(u.dtype), h[:, -1].astype(u.dtype)


def recurrent_block(x, h0, conv_buf, w_in, b_in, conv_w, conv_b, w_a, b_a, w_i, b_i, lam, w_out, b_out):
    proj = x @ w_in + b_in
    gate, u = jnp.split(proj, 2, axis=-1)
    u_ext = jnp.concatenate([conv_buf.astype(u.dtype), u], axis=1)
    uc = causal_dwconv(u_ext, conv_w, conv_b)
    h, h_last = rg_lru(uc, h0, w_a, b_a, w_i, b_i, lam)
    y = (h * jax.nn.gelu(gate)) @ w_out + b_out
    return y, h_last, u_ext[:, u_ext.shape[1] - (LRU_CONV - 1):]


def conv_ffn(x, conv_buf, w_up, conv_w, conv_b, w_down):
    up = x @ w_up
    up_ext = jnp.concatenate([conv_buf.astype(up.dtype), up], axis=1)
    c = causal_dwconv(up_ext, conv_w, conv_b)
    g, v = jnp.split(c, 2, axis=-1)
    y = (jax.nn.gelu(g) * v) @ w_down
    return y, up_ext[:, up_ext.shape[1] - (FFN_CONV - 1):]


def alibi_slopes():
    n = N_GROUPS * GROUP_HEADS
    s = 2.0 ** (-8.0 * np.arange(1, n + 1) / n)
    return jnp.asarray(s, dtype=jnp.float32).reshape(N_GROUPS, GROUP_HEADS)


def dilated_group_prompt(q, k, v, window, dil, slopes):
    B, S, H, Dh = q.shape
    blk = window // dil
    L = -(-S // dil)
    L = -(-L // blk) * blk
    nb = L // blk
    pad = L * dil - S

    def phase(t):
        t = jnp.pad(t, ((0, 0), (0, pad), (0, 0), (0, 0)))
        t = t.reshape(B, L, dil, H, Dh).transpose(0, 2, 1, 3, 4)
        return t.reshape(B, dil, nb, blk, H, Dh)

    def with_prev(t):
        prev = jnp.pad(t, ((0, 0), (0, 0), (1, 0), (0, 0), (0, 0), (0, 0)))[:, :, :-1]
        return jnp.concatenate([prev, t], axis=3)

    qb = phase(q)
    kk = with_prev(phase(k))
    vv = with_prev(phase(v))
    s = jnp.einsum('bpnihd,bpnjhd->bpnhij', qb, kk).astype(jnp.float32) / math.sqrt(Dh)
    i = jnp.arange(blk)[:, None]
    j = jnp.arange(2 * blk)[None, :]
    steps = i + blk - j
    n = jnp.arange(nb)[:, None, None]
    valid = (steps >= 0) & (steps <= blk) & ((n > 0) | (j >= blk))
    bias = -slopes[:, None, None] * (steps * dil).astype(jnp.float32)
    s = jnp.where(valid[None, None, :, None], s + bias, NEG)
    m = jnp.max(s, axis=-1, keepdims=True)
    e = jnp.exp(s - m)
    den = jnp.sum(e, axis=-1)
    lse = m[..., 0] + jnp.log(den)
    o = jnp.einsum('bpnhij,bpnjhd->bpnihd', e, vv.astype(jnp.float32)) / jnp.swapaxes(den, 3, 4)[..., None]
    o = o.reshape(B, dil, L, H, Dh).transpose(0, 2, 1, 3, 4).reshape(B, L * dil, H, Dh)[:, :S]
    lse = jnp.swapaxes(lse, 3, 4).reshape(B, dil, L, H).transpose(0, 2, 1, 3).reshape(B, L * dil, H)[:, :S]
    return o, lse


def dilated_group_sample(q, k_all, v_all, window, dil, slopes):
    B, T, H, Dh = q.shape
    wb = k_all.shape[1] - T
    steps = jnp.arange(window // dil + 1)
    idx = wb + jnp.arange(T)[:, None] - steps[None, :] * dil
    valid = idx >= 0
    idxc = jnp.maximum(idx, 0)
    kg = k_all[:, idxc]
    vg = v_all[:, idxc]
    dist = (steps * dil).astype(jnp.float32)
    s = jnp.einsum('bthd,btkhd->bthk', q, kg).astype(jnp.float32) / math.sqrt(Dh) - slopes[:, None] * dist[None, :]
    s = jnp.where(valid[:, None, :], s, NEG)
    m = jnp.max(s, axis=-1, keepdims=True)
    e = jnp.exp(s - m)
    den = jnp.sum(e, axis=-1)
    lse = m[..., 0] + jnp.log(den)
    o = jnp.einsum('bthk,btkhd->bthd', e, vg.astype(jnp.float32)) / den[..., None]
    return o, lse


def attention_block(x, kv_bufs, w_qkv, w_o):
    B, L, _ = x.shape
    qkv = (x @ w_qkv).reshape(B, L, N_GROUPS, 3, GROUP_HEADS, HEAD_DIM)
    slopes = alibi_slopes()
    outs, lses, new_kv = [], [], []
    for g, (win, dil) in enumerate(ATTN_GROUPS):
        q, k, v = qkv[:, :, g, 0], qkv[:, :, g, 1], qkv[:, :, g, 2]
        if kv_bufs is None:
            o, lse = dilated_group_prompt(q, k, v, win, dil, slopes[g])
            keep = min(win, L)
            new_kv.append(jnp.stack([k[:, L - keep:], v[:, L - keep:]], axis=2))
        else:
            buf = kv_bufs[g].astype(k.dtype)
            k_all = jnp.concatenate([buf[:, :, 0], k], axis=1)
            v_all = jnp.concatenate([buf[:, :, 1], v], axis=1)
            o, lse = dilated_group_sample(q, k_all, v_all, win, dil, slopes[g])
            new_kv.append(jnp.stack([k, v], axis=2))
        outs.append(o)
        lses.append(lse)
    wgt = jax.nn.softmax(jnp.stack(lses, axis=0), axis=0)
    o = jnp.sum(jnp.stack(outs, axis=0) * wgt[..., None], axis=0)
    y = o.reshape(B, L, ATTN_WIDTH).astype(x.dtype) @ w_o
    return y, new_kv


def run_trunk(x, lru_h, lru_conv, kv_caches, ffn_conv, norm_mix, norm_ffn, norm_final, lru_p, attn_p, ffn_p):
    new_h, new_lconv, new_fconv = [], [], []
    new_kv = [[] for _ in ATTN_GROUPS]
    for layer in range(DEPTH):
        j = layer // N_MIXERS
        xn = rms_norm(x, norm_mix[layer])
        if layer % N_MIXERS == 0:
            y, h_last, c_rows = recurrent_block(xn, lru_h[j], lru_conv[j], *[p[j] for p in lru_p])
            new_h.append(h_last)
            new_lconv.append(c_rows)
        else:
            bufs = None if kv_caches is None else [c[j] for c in kv_caches]
            y, kv_rows = attention_block(xn, bufs, attn_p[0][j], attn_p[1][j])
            for g in range(N_GROUPS):
                new_kv[g].append(kv_rows[g])
        x = x + y
        y, f_rows = conv_ffn(rms_norm(x, norm_ffn[layer]), ffn_conv[layer], *[p[layer] for p in ffn_p])
        new_fconv.append(f_rows)
        x = x + y
    out = rms_norm(x, norm_final)
    kv_out = [jnp.stack(rows, axis=0) for rows in new_kv]
    return out, jnp.stack(new_h, 0), jnp.stack(new_lconv, 0), kv_out, jnp.stack(new_fconv, 0)


def setup_inputs(seed: int = 0) -> dict:
    key = jax.random.key(seed)
    ks = jax.random.split(key, 32)
    f32 = jnp.float32
    nrm = lambda k, shape, scale: jax.random.normal(k, shape, f32) * scale
    a_c = jax.random.uniform(ks[0], (N_LRU, LRU_WIDTH), f32, minval=0.9, maxval=0.999)
    a_base = a_c ** (1.0 / LRU_C)
    lam = jnp.log(a_base) - jnp.log1p(-a_base)
    wbufs = [min(w, PAST_LEN) for (w, _) in ATTN_GROUPS]
    return {
        'x_prompt': nrm(ks[1], (BATCH, SEQ, D_MODEL), 1.0),
        'x_sample': nrm(ks[2], (DEC_BATCH, DEC_SEQ, D_MODEL), 1.0),
        'cache_kv_w128': nrm(ks[3], (N_ATTN, DEC_BATCH, wbufs[0], 2, GROUP_HEADS, HEAD_DIM), 1.0),
        'cache_kv_w512': nrm(ks[4], (N_ATTN, DEC_BATCH, wbufs[1], 2, GROUP_HEADS, HEAD_DIM), 1.0),
        'cache_kv_w2048': nrm(ks[5], (N_ATTN, DEC_BATCH, wbufs[2], 2, GROUP_HEADS, HEAD_DIM), 1.0),
        'state_lru_h': nrm(ks[6], (N_LRU, DEC_BATCH, LRU_WIDTH), 0.5),
        'state_lru_conv': nrm(ks[7], (N_LRU, DEC_BATCH, LRU_CONV - 1, LRU_WIDTH), 1.0),
        'state_ffn_conv': nrm(ks[8], (DEPTH, DEC_BATCH, FFN_CONV - 1, 2 * D_FF), 1.0),
        'norm_mix': 1.0 + nrm(ks[9], (DEPTH, D_MODEL), 0.01),
        'norm_ffn': 1.0 + nrm(ks[10], (DEPTH, D_MODEL), 0.01),
        'norm_final': 1.0 + nrm(ks[11], (D_MODEL,), 0.01),
        'lru_w_in': nrm(ks[12], (N_LRU, D_MODEL, 2 * LRU_WIDTH), D_MODEL ** -0.5),
        'lru_b_in': nrm(ks[13], (N_LRU, 2 * LRU_WIDTH), 0.01),
        'lru_conv_w': nrm(ks[14], (N_LRU, LRU_CONV, LRU_WIDTH), LRU_CONV ** -0.5),
        'lru_conv_b': nrm(ks[15], (N_LRU, LRU_WIDTH), 0.01),
        'lru_w_a': nrm(ks[16], (N_LRU, LRU_HEADS, LRU_HEAD_DIM, LRU_HEAD_DIM), LRU_HEAD_DIM ** -0.5),
        'lru_b_a': nrm(ks[17], (N_LRU, LRU_WIDTH), 0.01),
        'lru_w_i': nrm(ks[18], (N_LRU, LRU_HEADS, LRU_HEAD_DIM, LRU_HEAD_DIM), LRU_HEAD_DIM ** -0.5),
        'lru_b_i': nrm(ks[19], (N_LRU, LRU_WIDTH), 0.01),
        'lru_lambda': lam,
        'lru_w_out': nrm(ks[20], (N_LRU, LRU_WIDTH, D_MODEL), LRU_WIDTH ** -0.5),
        'lru_b_out': nrm(ks[21], (N_LRU, D_MODEL), 0.01),
        'attn_w_qkv': nrm(ks[22], (N_ATTN, D_MODEL, QKV_WIDTH), D_MODEL ** -0.5),
        'attn_w_o': nrm(ks[23], (N_ATTN, ATTN_WIDTH, D_MODEL), ATTN_WIDTH ** -0.5),
        'ffn_w_up': nrm(ks[24], (DEPTH, D_MODEL, 2 * D_FF), D_MODEL ** -0.5),
        'ffn_conv_w': nrm(ks[25], (DEPTH, FFN_CONV, 2 * D_FF), FFN_CONV ** -0.5),
        'ffn_conv_b': nrm(ks[26], (DEPTH, 2 * D_FF), 0.01),
        'ffn_w_down': nrm(ks[27], (DEPTH, D_FF, D_MODEL), D_FF ** -0.5),
    }


def reference(x_prompt, x_sample, cache_kv_w128, cache_kv_w512, cache_kv_w2048, state_lru_h, state_lru_conv, state_ffn_conv,
              norm_mix, norm_ffn, norm_final, lru_w_in, lru_b_in, lru_conv_w, lru_conv_b, lru_w_a, lru_b_a, lru_w_i, lru_b_i,
              lru_lambda, lru_w_out, lru_b_out, attn_w_qkv, attn_w_o, ffn_w_up, ffn_conv_w, ffn_conv_b, ffn_w_down):
    lru_p = (lru_w_in, lru_b_in, lru_conv_w, lru_conv_b, lru_w_a, lru_b_a, lru_w_i, lru_b_i, lru_lambda, lru_w_out, lru_b_out)
    attn_p = (attn_w_qkv, attn_w_o)
    ffn_p = (ffn_w_up, ffn_conv_w, ffn_conv_b, ffn_w_down)
    dt = x_prompt.dtype
    Bp = x_prompt.shape[0]
    y_p, h_p, lc_p, kv_p, fc_p = run_trunk(
        x_prompt,
        jnp.zeros((N_LRU, Bp, LRU_WIDTH), dt),
        jnp.zeros((N_LRU, Bp, LRU_CONV - 1, LRU_WIDTH), dt),
        None,
        jnp.zeros((DEPTH, Bp, FFN_CONV - 1, 2 * D_FF), dt),
        norm_mix, norm_ffn, norm_final, lru_p, attn_p, ffn_p)
    y_s, h_s, lc_s, kv_s, fc_s = run_trunk(
        x_sample, state_lru_h, state_lru_conv, (cache_kv_w128, cache_kv_w512, cache_kv_w2048), state_ffn_conv,
        norm_mix, norm_ffn, norm_final, lru_p, attn_p, ffn_p)
    return (y_p, y_s, kv_p[0], kv_p[1], kv_p[2], h_p, lc_p, fc_p, kv_s[0], kv_s[1], kv_s[2], h_s, lc_s, fc_s)
```

```python
import functools
import math

import jax
import jax.numpy as jnp
from jax import lax
from jax.experimental import pallas as pl
from jax.experimental.pallas import tpu as pltpu

EPS = 1e-6
NEG = -1e30
LRU_C = 8.0
ATTN_GROUPS = ((128, 1), (512, 4), (2048, 16))
N_GROUPS = len(ATTN_GROUPS)
GROUP_HEADS = 8
HEAD_DIM = 128
ATTN_WIDTH = GROUP_HEADS * HEAD_DIM
ATTN_BLK = 128
SUBLANES = 8
LANES = 128
VMEM_LIMIT = 48 * 1024 * 1024

F32 = jnp.float32
BF16 = jnp.bfloat16


def _alibi_slopes():
    n = N_GROUPS * GROUP_HEADS
    return [[2.0 ** (-8.0 * (g * GROUP_HEADS + h + 1) / n) for h in range(GROUP_HEADS)] for g in range(N_GROUPS)]


def _gelu(x):
    c = math.sqrt(2.0 / math.pi)
    return x * (0.5 * (1.0 + jnp.tanh(c * (x + 0.044715 * (x * x * x)))))


def _rms(x, g):
    ms = jnp.mean(x * x, axis=-1, keepdims=True)
    return x * lax.rsqrt(ms + EPS) * g


def _pick(n, cap):
    best = None
    for t in range(LANES, min(n, cap) + 1, LANES):
        if n % t == 0:
            best = t
    assert best is not None, (n, cap)
    return best


def _params(n_axes):
    return pltpu.CompilerParams(dimension_semantics=("arbitrary",) * n_axes, vmem_limit_bytes=VMEM_LIMIT)


def _norm_matmul_kernel(x_ref, g_ref, w_ref, b_ref, o_ref, xn_ref):
    @pl.when(pl.program_id(1) == 0)
    def _():
        xn_ref[...] = _rms(x_ref[...], g_ref[...]).astype(BF16)

    o_ref[...] = jnp.dot(xn_ref[...], w_ref[...], preferred_element_type=F32) + b_ref[...]


def norm_matmul(x, gains, g_layer, w, bias, layer, tm):
    M, D = x.shape
    N = w.shape[-1]
    tn = _pick(N, 512 if tm > 256 else 2048)
    return pl.pallas_call(
        _norm_matmul_kernel,
        grid=(M // tm, N // tn),
        in_specs=[
            pl.BlockSpec((tm, D), lambda i, j: (i, 0)),
            pl.BlockSpec((None, 1, D), lambda i, j: (g_layer, 0, 0)),
            pl.BlockSpec((None, D, tn), lambda i, j: (layer, 0, j)),
            pl.BlockSpec((None, 1, tn), lambda i, j: (layer, 0, j)),
        ],
        out_specs=pl.BlockSpec((tm, tn), lambda i, j: (i, j)),
        out_shape=jax.ShapeDtypeStruct((M, N), F32),
        scratch_shapes=[pltpu.VMEM((tm, D), BF16)],
        compiler_params=_params(2),
        name="norm_matmul",
    )(x, gains, w, bias)


def _matmul_res_kernel(a_ref, w_ref, b_ref, r_ref, o_ref):
    o_ref[...] = r_ref[...] + b_ref[...] + jnp.dot(a_ref[...].astype(BF16), w_ref[...], preferred_element_type=F32)


def matmul_res(a, w, bias, res, layer, tm):
    M, K = a.shape
    N = w.shape[-1]
    tn = _pick(N, 512 if tm > 256 else 2048)
    return pl.pallas_call(
        _matmul_res_kernel,
        grid=(M // tm, N // tn),
        in_specs=[
            pl.BlockSpec((tm, K), lambda i, j: (i, 0)),
            pl.BlockSpec((None, K, tn), lambda i, j: (layer, 0, j)),
            pl.BlockSpec((None, 1, tn), lambda i, j: (layer, 0, j)),
            pl.BlockSpec((tm, tn), lambda i, j: (i, j)),
        ],
        out_specs=pl.BlockSpec((tm, tn), lambda i, j: (i, j)),
        out_shape=jax.ShapeDtypeStruct((M, N), F32),
        compiler_params=_params(2),
        name="matmul_res",
    )(a, w, bias, res)


def _norm_kernel(x_ref, g_ref, o_ref):
    o_ref[...] = _rms(x_ref[...], g_ref[...])


def final_norm(x, gain, tm):
    M, D = x.shape
    return pl.pallas_call(
        _norm_kernel,
        grid=(M // tm,),
        in_specs=[pl.BlockSpec((tm, D), lambda i: (i, 0)), pl.BlockSpec((1, D), lambda i: (0, 0))],
        out_specs=pl.BlockSpec((tm, D), lambda i: (i, 0)),
        out_shape=jax.ShapeDtypeStruct((M, D), F32),
        compiler_params=_params(1),
        name="final_norm",
    )(x, gain)


def _lru_kernel(gate_ref, u_ref, cw_ref, cb_ref, wa_ref, ba_ref, wi_ref, bi_ref, lam_ref, ci_ref, hi_ref,
                o_ref, cs_ref, hs_ref,
                e_ref, a_scr, b_scr, h_scr, cc_ref, hc_ref, *, R, cru, tps):
    i = pl.program_id(0)
    c = pl.program_id(1)
    tm, tc = u_ref.shape

    @pl.when(i % tps == 0)
    def _():
        cc_ref[c] = ci_ref[...]
        hc_ref[c] = hi_ref[...]

    u = u_ref[...]
    e_ref[0:cru, :] = cc_ref[c]
    e_ref[cru:cru + tm, :] = u
    cw = cw_ref[...]
    uc = (cb_ref[...] + cw[3:4] * u + cw[2:3] * e_ref[cru - R:cru - R + tm, :]
          + cw[1:2] * e_ref[cru - 2 * R:cru - 2 * R + tm, :] + cw[0:1] * e_ref[cru - 3 * R:cru - 3 * R + tm, :])
    tail = e_ref[tm:tm + cru, :]
    cc_ref[c] = tail
    cs_ref[c] = tail

    hd = wa_ref.shape[-1]
    ucb = uc.astype(BF16)
    ra, ri = [], []
    for hh in range(tc // hd):
        ub = ucb[:, hh * hd:(hh + 1) * hd]
        ra.append(jnp.dot(ub, wa_ref[hh], preferred_element_type=F32))
        ri.append(jnp.dot(ub, wi_ref[hh], preferred_element_type=F32))
    r = jax.nn.sigmoid(jnp.concatenate(ra, axis=1) + ba_ref[...])
    ig = jax.nn.sigmoid(jnp.concatenate(ri, axis=1) + bi_ref[...])
    nlam = -lam_ref[...]
    softplus = jnp.maximum(nlam, 0.0) + jnp.log1p(jnp.exp(-jnp.abs(nlam)))
    log_a = (-LRU_C) * r * softplus
    a = jnp.exp(log_a)
    th = jnp.tanh(log_a)
    bx = jnp.sqrt((-2.0 * th) / (1.0 - th)) * (ig * uc)

    if R == 1:
        A = a.reshape(tm // SUBLANES, SUBLANES, tc)
        B = bx.reshape(tm // SUBLANES, SUBLANES, tc)
        row = lax.broadcasted_iota(jnp.int32, A.shape, 1)
        s = 1
        while s < SUBLANES:
            m = row >= s
            B = jnp.where(m, A * pltpu.roll(B, s, axis=1) + B, B)
            A = jnp.where(m, A * pltpu.roll(A, s, axis=1), A)
            s *= 2
        a_scr[...] = A.reshape(tm, tc)
        b_scr[...] = B.reshape(tm, tc)
        h0 = hc_ref[c][SUBLANES - 1:SUBLANES, :]
    else:
        assert R == SUBLANES
        a_scr[...] = a
        b_scr[...] = bx
        h0 = hc_ref[c]

    def body(g, h):
        r0 = pl.multiple_of(g * SUBLANES, SUBLANES)
        hg = b_scr[pl.ds(r0, SUBLANES), :] + a_scr[pl.ds(r0, SUBLANES), :] * h
        h_scr[pl.ds(r0, SUBLANES), :] = hg
        return hg[SUBLANES - 1:SUBLANES, :] if R == 1 else hg

    n_groups = tm // SUBLANES
    lax.fori_loop(0, n_groups, body, h0, unroll=min(n_groups, 8))
    h_tail = h_scr[tm - SUBLANES:tm, :]
    hc_ref[c] = h_tail
    hs_ref[c] = h_tail
    o_ref[...] = (h_scr[...] * _gelu(gate_ref[...])).astype(BF16)


def lru_core(proj, conv_w, conv_b, w_a, b_a, w_i, b_i, lam, conv_init, h_init, layer, tm, R):
    M, W2 = proj.shape
    W = W2 // 2
    hd = w_a.shape[-1]
    tc = max(hd, _pick(W, 512))
    assert tc % hd == 0 and W % tc == 0
    nc = W // tc
    nseq, cru, _ = conv_init.shape
    tps = (M // tm) // nseq
    kern = functools.partial(_lru_kernel, R=R, cru=cru, tps=tps)
    row_vec = lambda: pl.BlockSpec((None, 1, tc), lambda i, c: (layer, 0, c))
    gate_w = lambda: pl.BlockSpec((None, tc // hd, hd, hd), lambda i, c: (layer, c, 0, 0))
    unfold = lambda s: jnp.swapaxes(s, 1, 2).reshape(nseq, s.shape[2], W)
    hg, c_rows, h_rows = pl.pallas_call(
        kern,
        grid=(M // tm, nc),
        in_specs=[
            pl.BlockSpec((tm, tc), lambda i, c: (i, c)),
            pl.BlockSpec((tm, tc), lambda i, c: (i, nc + c)),
            pl.BlockSpec((None, conv_w.shape[1], tc), lambda i, c: (layer, 0, c)),
            row_vec(), gate_w(), row_vec(), gate_w(), row_vec(), row_vec(),
            pl.BlockSpec((None, cru, tc), lambda i, c: (i // tps, 0, c)),
            pl.BlockSpec((None, SUBLANES, tc), lambda i, c: (i // tps, 0, c)),
        ],
        out_specs=[
            pl.BlockSpec((tm, tc), lambda i, c: (i, c)),
            pl.BlockSpec((None, nc, cru, tc), lambda i, c: (i // tps, 0, 0, 0)),
            pl.BlockSpec((None, nc, SUBLANES, tc), lambda i, c: (i // tps, 0, 0, 0)),
        ],
        out_shape=[
            jax.ShapeDtypeStruct((M, W), BF16),
            jax.ShapeDtypeStruct((nseq, nc, cru, tc), F32),
            jax.ShapeDtypeStruct((nseq, nc, SUBLANES, tc), F32),
        ],
        scratch_shapes=[
            pltpu.VMEM((cru + tm, tc), F32),
            pltpu.VMEM((tm, tc), F32),
            pltpu.VMEM((tm, tc), F32),
            pltpu.VMEM((tm, tc), F32),
            pltpu.VMEM((nc, cru, tc), F32),
            pltpu.VMEM((nc, SUBLANES, tc), F32),
        ],
        compiler_params=_params(2),
        name="lru_core",
    )(proj, proj, conv_w, conv_b, w_a, b_a, w_i, b_i, lam, conv_init, h_init)
    return hg, unfold(c_rows), unfold(h_rows)


def _ffn_kernel(x_ref, g_ref, wg_ref, wv_ref, cwg_ref, cwv_ref, cbg_ref, cbv_ref, wd_ref, ig_ref, iv_ref,
                o_ref, sg_ref, sv_ref,
                xn_ref, eg_ref, ev_ref, cg_ref, cv_ref, *, R, cr, tps):
    i = pl.program_id(0)
    j = pl.program_id(1)
    tm = x_ref.shape[0]

    @pl.when(j == 0)
    def _():
        x = x_ref[...]
        xn_ref[...] = _rms(x, g_ref[...]).astype(BF16)
        o_ref[...] = x

    @pl.when(i % tps == 0)
    def _():
        cg_ref[j] = ig_ref[...]
        cv_ref[j] = iv_ref[...]

    def side(w_ref, cw_ref, cb_ref, e_ref, c_ref, s_ref):
        up = jnp.dot(xn_ref[...], w_ref[...], preferred_element_type=F32)
        e_ref[0:cr, :] = c_ref[j]
        e_ref[cr:cr + tm, :] = up
        cw = cw_ref[...]
        conv = (cb_ref[...] + cw[2:3] * up + cw[1:2] * e_ref[cr - R:cr - R + tm, :]
                + cw[0:1] * e_ref[cr - 2 * R:cr - 2 * R + tm, :])
        tail = e_ref[tm:tm + cr, :]
        c_ref[j] = tail
        s_ref[j] = tail
        return conv

    cg = side(wg_ref, cwg_ref, cbg_ref, eg_ref, cg_ref, sg_ref)
    cv = side(wv_ref, cwv_ref, cbv_ref, ev_ref, cv_ref, sv_ref)
    act = (_gelu(cg) * cv).astype(BF16)
    o_ref[...] += jnp.dot(act, wd_ref[...], preferred_element_type=F32)


def conv_ffn(x, gains, w_up, conv_w, conv_b, w_down, init, layer, tm, tf, R):
    M, D = x.shape
    F = w_down.shape[1]
    nf = F // tf
    nseq, cr, _ = init.shape
    tps = (M // tm) // nseq
    kern = functools.partial(_ffn_kernel, R=R, cr=cr, tps=tps)
    K = conv_w.shape[1]
    halves = lambda mk: [mk(0), mk(nf)]
    state_spec = lambda off: pl.BlockSpec((None, cr, tf), lambda i, j: (i // tps, 0, off + j))
    out, sg, sv = pl.pallas_call(
        kern,
        grid=(M // tm, nf),
        in_specs=[
            pl.BlockSpec((tm, D), lambda i, j: (i, 0)),
            pl.BlockSpec((None, 1, D), lambda i, j: (layer, 0, 0)),
            *halves(lambda off: pl.BlockSpec((None, D, tf), lambda i, j: (layer, 0, off + j))),
            *halves(lambda off: pl.BlockSpec((None, K, tf), lambda i, j: (layer, 0, off + j))),
            *halves(lambda off: pl.BlockSpec((None, 1, tf), lambda i, j: (layer, 0, off + j))),
            pl.BlockSpec((None, tf, D), lambda i, j: (layer, j, 0)),
            *halves(state_spec),
        ],
        out_specs=[
            pl.BlockSpec((tm, D), lambda i, j: (i, 0)),
            pl.BlockSpec((None, nf, cr, tf), lambda i, j: (i // tps, 0, 0, 0)),
            pl.BlockSpec((None, nf, cr, tf), lambda i, j: (i // tps, 0, 0, 0)),
        ],
        out_shape=[
            jax.ShapeDtypeStruct((M, D), F32),
            jax.ShapeDtypeStruct((nseq, nf, cr, tf), F32),
            jax.ShapeDtypeStruct((nseq, nf, cr, tf), F32),
        ],
        scratch_shapes=[
            pltpu.VMEM((tm, D), BF16),
            pltpu.VMEM((cr + tm, tf), F32),
            pltpu.VMEM((cr + tm, tf), F32),
            pltpu.VMEM((nf, cr, tf), F32),
            pltpu.VMEM((nf, cr, tf), F32),
        ],
        compiler_params=_params(2),
        name="conv_ffn",
    )(x, gains, w_up, w_up, conv_w, conv_w, conv_b, conv_b, w_down, init, init)
    unfold = lambda s: jnp.swapaxes(s, 1, 2).reshape(nseq, cr, F)
    return out, jnp.concatenate([unfold(sg), unfold(sv)], axis=-1)


def _attn_prompt_kernel(q_ref, k_ref, v_ref, o_ref, kp_ref, vp_ref, *, dil, slopes):
    n = pl.program_id(2)
    blk = q_ref.shape[0]

    @pl.when(n == 0)
    def _():
        kp_ref[...] = jnp.zeros_like(kp_ref)
        vp_ref[...] = jnp.zeros_like(vp_ref)

    ii = lax.broadcasted_iota(jnp.int32, (blk, blk), 0)
    jj = lax.broadcasted_iota(jnp.int32, (blk, blk), 1)
    d_cur = (ii - jj).astype(F32) * float(dil)
    d_prev = d_cur + float(blk * dil)
    cur_ok = jj <= ii
    prev_ok = jnp.logical_and(jj >= ii, n > 0)
    lane = lax.broadcasted_iota(jnp.int32, (blk, LANES), 1)
    lse_tile = jnp.zeros((blk, LANES), F32)
    inv_sqrt = 1.0 / math.sqrt(HEAD_DIM)
    nt = (((1,), (1,)), ((), ()))
    for h in range(GROUP_HEADS):
        sl = slice(h * HEAD_DIM, (h + 1) * HEAD_DIM)
        q = q_ref[:, sl].astype(BF16)
        kc = k_ref[:, sl].astype(BF16)
        vc = v_ref[:, sl].astype(BF16)
        s_c = lax.dot_general(q, kc, nt, preferred_element_type=F32) * inv_sqrt
        s_p = lax.dot_general(q, kp_ref[:, sl], nt, preferred_element_type=F32) * inv_sqrt
        s_c = jnp.where(cur_ok, s_c - slopes[h] * d_cur, NEG)
        s_p = jnp.where(prev_ok, s_p - slopes[h] * d_prev, NEG)
        m = jnp.maximum(jnp.max(s_c, axis=-1, keepdims=True), jnp.max(s_p, axis=-1, keepdims=True))
        e_c = jnp.exp(s_c - m)
        e_p = jnp.exp(s_p - m)
        den = jnp.sum(e_c, axis=-1, keepdims=True) + jnp.sum(e_p, axis=-1, keepdims=True)
        o = (jnp.dot(e_c.astype(BF16), vc, preferred_element_type=F32)
             + jnp.dot(e_p.astype(BF16), vp_ref[:, sl], preferred_element_type=F32)) / den
        o_ref[:, sl] = o
        lse_tile = jnp.where(lane == h, m + jnp.log(den), lse_tile)
    o_ref[:, ATTN_WIDTH:ATTN_WIDTH + LANES] = lse_tile
    kp_ref[...] = k_ref[...].astype(BF16)
    vp_ref[...] = v_ref[...].astype(BF16)


def attn_prompt_group(qkv, g, B, S):
    win, dil = ATTN_GROUPS[g]
    blk = win // dil
    assert blk == ATTN_BLK and S % (dil * blk) == 0
    L = S // dil
    nb = L // blk
    Q = qkv.shape[-1]
    nq = Q // ATTN_WIDTH
    ow = ATTN_WIDTH + LANES
    view = qkv.reshape(B, L, dil * Q)
    kern = functools.partial(_attn_prompt_kernel, dil=dil, slopes=_alibi_slopes()[g])
    spec = lambda comp: pl.BlockSpec((None, blk, ATTN_WIDTH), lambda b, r, n: (b, n, r * nq + g * 3 + comp))
    out = pl.pallas_call(
        kern,
        grid=(B, dil, nb),
        in_specs=[spec(0), spec(1), spec(2)],
        out_specs=pl.BlockSpec((None, blk, ow), lambda b, r, n: (b, n, r)),
        out_shape=jax.ShapeDtypeStruct((B, L, dil * ow), F32),
        scratch_shapes=[pltpu.VMEM((blk, ATTN_WIDTH), BF16), pltpu.VMEM((blk, ATTN_WIDTH), BF16)],
        compiler_params=_params(3),
        name=f"attn_prompt_g{g}",
    )(view, view, view)
    return out.reshape(B * S, ow)


def _attn_out_kernel(o0_ref, o1_ref, o2_ref, w_ref, x_ref, out_ref):
    refs = (o0_ref, o1_ref, o2_ref)
    lses = [r[:, ATTN_WIDTH:ATTN_WIDTH + LANES] for r in refs]
    m = jnp.maximum(jnp.maximum(lses[0], lses[1]), lses[2])
    es = [jnp.exp(l - m) for l in lses]
    inv = 1.0 / (es[0] + es[1] + es[2])
    ws = [e * inv for e in es]
    cols = []
    for h in range(GROUP_HEADS):
        sl = slice(h * HEAD_DIM, (h + 1) * HEAD_DIM)
        oh = ws[0][:, h:h + 1] * refs[0][:, sl]
        for g in range(1, N_GROUPS):
            oh = oh + ws[g][:, h:h + 1] * refs[g][:, sl]
        cols.append(oh.astype(BF16))
    o = jnp.concatenate(cols, axis=1)
    out_ref[...] = x_ref[...] + jnp.dot(o, w_ref[...], preferred_element_type=F32)


def attn_out(outs, w_o, x, layer, tm):
    M, D = x.shape
    ow = outs[0].shape[-1]
    o_spec = pl.BlockSpec((tm, ow), lambda i: (i, 0))
    return pl.pallas_call(
        _attn_out_kernel,
        grid=(M // tm,),
        in_specs=[o_spec, o_spec, o_spec,
                  pl.BlockSpec((None, ATTN_WIDTH, D), lambda i: (layer, 0, 0)),
                  pl.BlockSpec((tm, D), lambda i: (i, 0))],
        out_specs=pl.BlockSpec((tm, D), lambda i: (i, 0)),
        out_shape=jax.ShapeDtypeStruct((M, D), F32),
        compiler_params=_params(1),
        name="attn_out",
    )(*outs, w_o, x)


def _attn_sample_kernel(qkv_ref, c0_ref, c1_ref, c2_ref, o_ref, *, slopes):
    T = qkv_ref.shape[0]
    caches = (c0_ref, c1_ref, c2_ref)
    rows = c0_ref.shape[0]
    l_idx = lax.broadcasted_iota(jnp.int32, (rows, 1), 0)
    l_f = l_idx.astype(F32)
    inv_sqrt = 1.0 / math.sqrt(HEAD_DIM)
    kvw = 2 * ATTN_WIDTH
    for t in range(T):
        for h in range(GROUP_HEADS):
            outs, lses = [], []
            for g, (win, dil) in enumerate(ATTN_GROUPS):
                base = g * 3 * ATTN_WIDTH + h * HEAD_DIM
                slope = slopes[g][h]
                q = qkv_ref[t, :, base:base + HEAD_DIM]
                ph = 0 if dil == 1 else t
                col = ph * kvw + h * HEAD_DIM
                kc = caches[g][:, col:col + HEAD_DIM]
                vc = caches[g][:, col + ATTN_WIDTH:col + ATTN_WIDTH + HEAD_DIM]
                sc = jnp.sum(kc * q, axis=-1, keepdims=True) * inv_sqrt
                if dil == 1:
                    sc = jnp.where(l_idx >= t, sc - slope * (float(rows + t) - l_f), NEG)
                    new_ts = list(range(t + 1))
                else:
                    sc = sc - (slope * dil) * (float(rows) - l_f)
                    new_ts = [t]
                m = jnp.max(sc, axis=0, keepdims=True)
                s_new = []
                for t2 in new_ts:
                    k2 = qkv_ref[t2, :, base + ATTN_WIDTH:base + ATTN_WIDTH + HEAD_DIM]
                    s2 = jnp.sum(q * k2, axis=-1, keepdims=True) * inv_sqrt - slope * float((t - t2) * dil)
                    s_new.append(s2)
                    m = jnp.maximum(m, s2)
                ec = jnp.exp(sc - m)
                den = jnp.sum(ec, axis=0, keepdims=True)
                acc = jnp.sum(ec * vc, axis=0, keepdims=True)
                for t2, s2 in zip(new_ts, s_new):
                    v2 = qkv_ref[t2, :, base + 2 * ATTN_WIDTH:base + 2 * ATTN_WIDTH + HEAD_DIM]
                    e2 = jnp.exp(s2 - m)
                    den = den + e2
                    acc = acc + e2 * v2
                outs.append(acc / den)
                lses.append(m + jnp.log(den))
            mm = jnp.maximum(jnp.maximum(lses[0], lses[1]), lses[2])
            es = [jnp.exp(l - mm) for l in lses]
            tot = es[0] + es[1] + es[2]
            o = (es[0] * outs[0] + es[1] * outs[1] + es[2] * outs[2]) / tot
            o_ref[t, :, h * HEAD_DIM:(h + 1) * HEAD_DIM] = o


def attn_sample(qkv, caches, layer, T, Bs):
    Q = qkv.shape[-1]
    kvw = 2 * ATTN_WIDTH
    views, specs = [], []
    for (win, dil), c in zip(ATTN_GROUPS, caches):
        assert c.shape[1] == Bs and c.shape[2] == win and T <= dil * 4 and win // dil == ATTN_BLK
        phases = min(dil, 4)
        assert T <= max(phases, 4) and (dil == 1 or T <= phases)
        views.append(c.reshape(c.shape[0], Bs, win // dil, dil * kvw))
        specs.append(pl.BlockSpec((None, None, win // dil, phases * kvw), lambda b: (layer, b, 0, 0)))
    kern = functools.partial(_attn_sample_kernel, slopes=_alibi_slopes())
    out = pl.pallas_call(
        kern,
        grid=(Bs,),
        in_specs=[pl.BlockSpec((T, None, 1, Q), lambda b: (0, b, 0, 0)), *specs],
        out_specs=pl.BlockSpec((T, None, 1, ATTN_WIDTH), lambda b: (0, b, 0, 0)),
        out_shape=jax.ShapeDtypeStruct((T, Bs, 1, ATTN_WIDTH), F32),
        compiler_params=_params(1),
        name="attn_sample",
    )(qkv.reshape(T, Bs, 1, Q), *views)
    return out.reshape(T * Bs, ATTN_WIDTH)


def _run_trunk(x, nseq, R, tm, tf, lru_h, lru_conv, kv_caches, ffn_conv, P, dims):
    M, D = x.shape
    depth = P["norm_mix"].shape[0]
    new_h, new_lconv, new_fconv, qkvs = [], [], [], []
    for layer in range(depth):
        j = layer // 2
        if layer % 2 == 0:
            proj = norm_matmul(x, P["norm_mix"], layer, P["lru_w_in"], P["lru_b_in"], layer=j, tm=tm)
            hg, c_rows, h_rows = lru_core(proj, P["lru_conv_w"], P["lru_conv_b"], P["lru_w_a"], P["lru_b_a"],
                                          P["lru_w_i"], P["lru_b_i"], P["lru_lambda"], lru_conv[j], lru_h[j],
                                          layer=j, tm=tm, R=R)
            new_h.append(h_rows)
            new_lconv.append(c_rows)
            x = matmul_res(hg, P["lru_w_out"], P["lru_b_out"], x, layer=j, tm=tm)
        else:
            qkv = norm_matmul(x, P["norm_mix"], layer, P["attn_w_qkv"], P["attn_b_zero"], layer=j, tm=tm)
            qkvs.append(qkv)
            if kv_caches is None:
                B, S = dims
                outs = [attn_prompt_group(qkv, g, B, S) for g in range(N_GROUPS)]
                x = attn_out(outs, P["attn_w_o"], x, layer=j, tm=min(tm, 256))
            else:
                T, Bs = dims
                o = attn_sample(qkv, kv_caches, layer=j, T=T, Bs=Bs)
                x = matmul_res(o, P["attn_w_o"], P["attn_bo_zero"], x, layer=j, tm=tm)
        x, f_rows = conv_ffn(x, P["norm_ffn"], P["ffn_w_up"], P["ffn_conv_w"], P["ffn_conv_b"], P["ffn_w_down"],
                             ffn_conv[layer], layer=layer, tm=tm, tf=tf, R=R)
        new_fconv.append(f_rows)
    y = final_norm(x, P["norm_final"], tm=tm)
    return y, new_h, new_lconv, qkvs, new_fconv


def kernel(x_prompt, x_sample, cache_kv_w128, cache_kv_w512, cache_kv_w2048, state_lru_h, state_lru_conv, state_ffn_conv, norm_mix, norm_ffn, norm_final, lru_w_in, lru_b_in, lru_conv_w, lru_conv_b, lru_w_a, lru_b_a, lru_w_i, lru_b_i, lru_lambda, lru_w_out, lru_b_out, attn_w_qkv, attn_w_o, ffn_w_up, ffn_conv_w, ffn_conv_b, ffn_w_down):
    B, S, D = x_prompt.shape
    Bs, T, _ = x_sample.shape
    depth = norm_mix.shape[0]
    n_lru, W = lru_lambda.shape
    n_attn = attn_w_qkv.shape[0]
    F2 = ffn_w_up.shape[-1]
    assert Bs == SUBLANES, "the sample group is laid out time-major with one sublane per sequence"
    row3 = lambda a: a.reshape(a.shape[0], 1, a.shape[-1])
    P = {
        "norm_mix": row3(norm_mix), "norm_ffn": row3(norm_ffn), "norm_final": norm_final.reshape(1, D),
        "lru_w_in": lru_w_in.astype(BF16), "lru_b_in": row3(lru_b_in),
        "lru_conv_w": lru_conv_w, "lru_conv_b": row3(lru_conv_b),
        "lru_w_a": lru_w_a.astype(BF16), "lru_b_a": row3(lru_b_a),
        "lru_w_i": lru_w_i.astype(BF16), "lru_b_i": row3(lru_b_i),
        "lru_lambda": row3(lru_lambda),
        "lru_w_out": lru_w_out.astype(BF16), "lru_b_out": row3(lru_b_out),
        "attn_w_qkv": attn_w_qkv.astype(BF16), "attn_w_o": attn_w_o.astype(BF16),
        "attn_b_zero": jnp.zeros((n_attn, 1, attn_w_qkv.shape[-1]), F32),
        "attn_bo_zero": jnp.zeros((n_attn, 1, D), F32),
        "ffn_w_up": ffn_w_up.astype(BF16), "ffn_conv_w": ffn_conv_w, "ffn_conv_b": row3(ffn_conv_b),
        "ffn_w_down": ffn_w_down.astype(BF16),
    }
    kc = lru_conv_w.shape[1] - 1
    kf = ffn_conv_w.shape[1] - 1

    tm_p = 512 if S % 512 == 0 else S
    zeros = lambda n, w: [jnp.zeros((B, SUBLANES, w), F32)] * n
    y_p, h_p, lc_p, qkv_p, fc_p = _run_trunk(
        x_prompt.reshape(B * S, D), B, 1, tm_p, 512, zeros(n_lru, W), zeros(n_lru, W), None, zeros(depth, F2), P, (B, S))

    tmaj = lambda a: jnp.swapaxes(a, 0, 1).reshape(1, a.shape[1] * a.shape[0], a.shape[-1])
    y_s, h_s, lc_s, qkv_s, fc_s = _run_trunk(
        jnp.swapaxes(x_sample, 0, 1).reshape(T * Bs, D), 1, Bs, T * Bs, 1024,
        [state_lru_h[j].reshape(1, Bs, W) for j in range(n_lru)],
        [tmaj(state_lru_conv[j]) for j in range(n_lru)],
        (cache_kv_w128, cache_kv_w512, cache_kv_w2048),
        [tmaj(state_ffn_conv[l]) for l in range(depth)], P, (T, Bs))

    bmaj = lambda a, k: jnp.swapaxes(a.reshape(-1, Bs, a.shape[-1])[-k:], 0, 1)
    kv_p, kv_s = [], []
    for g, (win, dil) in enumerate(ATTN_GROUPS):
        lo = (g * 3 + 1) * ATTN_WIDTH
        keep = min(win, S)
        kv_p.append(jnp.stack([q.reshape(B, S, -1)[:, S - keep:, lo:lo + 2 * ATTN_WIDTH]
                               .reshape(B, keep, 2, GROUP_HEADS, HEAD_DIM) for q in qkv_p], axis=0))
        kv_s.append(jnp.stack([jnp.swapaxes(q.reshape(T, Bs, -1), 0, 1)[:, :, lo:lo + 2 * ATTN_WIDTH]
                               .reshape(Bs, T, 2, GROUP_HEADS, HEAD_DIM) for q in qkv_s], axis=0))
    return (
        y_p.reshape(B, S, D),
        jnp.swapaxes(y_s.reshape(T, Bs, D), 0, 1),
        kv_p[0], kv_p[1], kv_p[2],
        jnp.stack([h[:, SUBLANES - 1] for h in h_p], axis=0),
        jnp.stack([c[:, SUBLANES - kc:] for c in lc_p], axis=0),
        jnp.stack([f[:, SUBLANES - kf:] for f in fc_p], axis=0),
        kv_s[0], kv_s[1], kv_s[2],
        jnp.stack([h[0] for h in h_s], axis=0),
        jnp.stack([bmaj(c[0], kc) for c in lc_s], axis=0),
        jnp.stack([bmaj(f[0], kf) for f in fc_s], axis=0),
    )
```

```python
import functools
import math

import jax
import jax.numpy as jnp
from jax import lax
from jax.experimental import pallas as pl
from jax.experimental.pallas import tpu as pltpu

EPS = 1e-6
NEG = -1e30
LRU_C = 8.0
ATTN_GROUPS = ((128, 1), (512, 4), (2048, 16))
N_GROUPS = len(ATTN_GROUPS)
GROUP_HEADS = 8
HEAD_DIM = 128
ATTN_WIDTH = GROUP_HEADS * HEAD_DIM
ATTN_BLK = 128
SUBLANES = 8
LANES = 128
VMEM_LIMIT = 48 * 1024 * 1024

F32 = jnp.float32
BF16 = jnp.bfloat16


def _alibi_slopes():
    n = N_GROUPS * GROUP_HEADS
    return [[2.0 ** (-8.0 * (g * GROUP_HEADS + h + 1) / n) for h in range(GROUP_HEADS)] for g in range(N_GROUPS)]


def _gelu(x):
    c = math.sqrt(2.0 / math.pi)
    return x * (0.5 * (1.0 + jnp.tanh(c * (x + 0.044715 * (x * x * x)))))


def _rms(x, g):
    ms = jnp.mean(x * x, axis=-1, keepdims=True)
    return x * lax.rsqrt(ms + EPS) * g


def _pick(n, cap):
    best = None
    for t in range(LANES, min(n, cap) + 1, LANES):
        if n % t == 0:
            best = t
    assert best is not None, (n, cap)
    return best


def _params(n_axes):
    return pltpu.CompilerParams(dimension_semantics=("arbitrary",) * n_axes, vmem_limit_bytes=VMEM_LIMIT)


def _norm_matmul_kernel(x_ref, g_ref, w_ref, b_ref, o_ref, xn_ref):
    @pl.when(pl.program_id(1) == 0)
    def _():
        xn_ref[...] = _rms(x_ref[...], g_ref[...]).astype(BF16)

    acc = jnp.dot(xn_ref[...], w_ref[...], preferred_element_type=F32) + b_ref[...]
    if len(o_ref.shape) == 2:
        o_ref[...] = acc
    else:
        for c in range(o_ref.shape[0]):
            o_ref[c] = acc[:, c * LANES:(c + 1) * LANES]


def norm_matmul(x, gains, g_layer, w, bias, layer, tm, slab_out=False):
    M, D = x.shape
    N = w.shape[-1]
    tn = _pick(N, 512 if tm > 256 else 2048)
    if slab_out:
        out_spec = pl.BlockSpec((tn // LANES, tm, LANES), lambda i, j: (j, i, 0))
        out_shape = jax.ShapeDtypeStruct((N // LANES, M, LANES), F32)
    else:
        out_spec = pl.BlockSpec((tm, tn), lambda i, j: (i, j))
        out_shape = jax.ShapeDtypeStruct((M, N), F32)
    return pl.pallas_call(
        _norm_matmul_kernel,
        grid=(M // tm, N // tn),
        in_specs=[
            pl.BlockSpec((tm, D), lambda i, j: (i, 0)),
            pl.BlockSpec((None, 1, D), lambda i, j: (g_layer, 0, 0)),
            pl.BlockSpec((None, D, tn), lambda i, j: (layer, 0, j)),
            pl.BlockSpec((None, 1, tn), lambda i, j: (layer, 0, j)),
        ],
        out_specs=out_spec,
        out_shape=out_shape,
        scratch_shapes=[pltpu.VMEM((tm, D), BF16)],
        compiler_params=_params(2),
        name="norm_matmul",
    )(x, gains, w, bias)


def _matmul_res_kernel(a_ref, w_ref, b_ref, r_ref, o_ref):
    if len(a_ref.shape) == 2:
        a = a_ref[...]
    else:
        a = jnp.concatenate([a_ref[c] for c in range(a_ref.shape[0])], axis=1)
    o_ref[...] = r_ref[...] + b_ref[...] + jnp.dot(a.astype(BF16), w_ref[...], preferred_element_type=F32)


def matmul_res(a, w, bias, res, layer, tm):
    M, N = res.shape
    K = w.shape[1]
    tn = _pick(N, 512 if tm > 256 else 2048)
    if a.ndim == 2:
        a_spec = pl.BlockSpec((tm, K), lambda i, j: (i, 0))
    else:
        a_spec = pl.BlockSpec((K // LANES, tm, LANES), lambda i, j: (0, i, 0))
    return pl.pallas_call(
        _matmul_res_kernel,
        grid=(M // tm, N // tn),
        in_specs=[
            a_spec,
            pl.BlockSpec((None, K, tn), lambda i, j: (layer, 0, j)),
            pl.BlockSpec((None, 1, tn), lambda i, j: (layer, 0, j)),
            pl.BlockSpec((tm, tn), lambda i, j: (i, j)),
        ],
        out_specs=pl.BlockSpec((tm, tn), lambda i, j: (i, j)),
        out_shape=jax.ShapeDtypeStruct((M, N), F32),
        compiler_params=_params(2),
        name="matmul_res",
    )(a, w, bias, res)


def _norm_kernel(x_ref, g_ref, o_ref):
    o_ref[...] = _rms(x_ref[...], g_ref[...])


def final_norm(x, gain, tm):
    M, D = x.shape
    return pl.pallas_call(
        _norm_kernel,
        grid=(M // tm,),
        in_specs=[pl.BlockSpec((tm, D), lambda i: (i, 0)), pl.BlockSpec((1, D), lambda i: (0, 0))],
        out_specs=pl.BlockSpec((tm, D), lambda i: (i, 0)),
        out_shape=jax.ShapeDtypeStruct((M, D), F32),
        compiler_params=_params(1),
        name="final_norm",
    )(x, gain)


def _lru_kernel(gate_ref, u_ref, cw_ref, cb_ref, wa_ref, ba_ref, wi_ref, bi_ref, lam_ref, ci_ref, hi_ref,
                o_ref, cs_ref, hs_ref,
                e_ref, a_scr, b_scr, h_scr, cc_ref, hc_ref, *, R, cru, tps):
    i = pl.program_id(0)
    c = pl.program_id(1)
    tm, tc = u_ref.shape

    @pl.when(i % tps == 0)
    def _():
        cc_ref[c] = ci_ref[...]
        hc_ref[c] = hi_ref[...]

    u = u_ref[...]
    e_ref[0:cru, :] = cc_ref[c]
    e_ref[cru:cru + tm, :] = u
    cw = cw_ref[...]
    uc = (cb_ref[...] + cw[3:4] * u + cw[2:3] * e_ref[cru - R:cru - R + tm, :]
          + cw[1:2] * e_ref[cru - 2 * R:cru - 2 * R + tm, :] + cw[0:1] * e_ref[cru - 3 * R:cru - 3 * R + tm, :])
    tail = e_ref[tm:tm + cru, :]
    cc_ref[c] = tail
    cs_ref[c] = tail

    hd = wa_ref.shape[-1]
    ucb = uc.astype(BF16)
    ra, ri = [], []
    for hh in range(tc // hd):
        ub = ucb[:, hh * hd:(hh + 1) * hd]
        ra.append(jnp.dot(ub, wa_ref[hh], preferred_element_type=F32))
        ri.append(jnp.dot(ub, wi_ref[hh], preferred_element_type=F32))
    r = jax.nn.sigmoid(jnp.concatenate(ra, axis=1) + ba_ref[...])
    ig = jax.nn.sigmoid(jnp.concatenate(ri, axis=1) + bi_ref[...])
    nlam = -lam_ref[...]
    softplus = jnp.maximum(nlam, 0.0) + jnp.log1p(jnp.exp(-jnp.abs(nlam)))
    log_a = (-LRU_C) * r * softplus
    a = jnp.exp(log_a)
    th = jnp.tanh(log_a)
    bx = jnp.sqrt((-2.0 * th) / (1.0 - th)) * (ig * uc)

    if R == 1:
        A = a.reshape(tm // SUBLANES, SUBLANES, tc)
        B = bx.reshape(tm // SUBLANES, SUBLANES, tc)
        row = lax.broadcasted_iota(jnp.int32, A.shape, 1)
        s = 1
        while s < SUBLANES:
            m = row >= s
            B = jnp.where(m, A * pltpu.roll(B, s, axis=1) + B, B)
            A = jnp.where(m, A * pltpu.roll(A, s, axis=1), A)
            s *= 2
        a_scr[...] = A.reshape(tm, tc)
        b_scr[...] = B.reshape(tm, tc)
        h0 = hc_ref[c][SUBLANES - 1:SUBLANES, :]
    else:
        assert R == SUBLANES
        a_scr[...] = a
        b_scr[...] = bx
        h0 = hc_ref[c]

    def body(g, h):
        r0 = pl.multiple_of(g * SUBLANES, SUBLANES)
        hg = b_scr[pl.ds(r0, SUBLANES), :] + a_scr[pl.ds(r0, SUBLANES), :] * h
        h_scr[pl.ds(r0, SUBLANES), :] = hg
        return hg[SUBLANES - 1:SUBLANES, :] if R == 1 else hg

    n_groups = tm // SUBLANES
    lax.fori_loop(0, n_groups, body, h0, unroll=min(n_groups, 8))
    h_tail = h_scr[tm - SUBLANES:tm, :]
    hc_ref[c] = h_tail
    hs_ref[c] = h_tail
    o_ref[...] = (h_scr[...] * _gelu(gate_ref[...])).astype(BF16)


def lru_core(proj, conv_w, conv_b, w_a, b_a, w_i, b_i, lam, conv_init, h_init, layer, tm, R):
    M, W2 = proj.shape
    W = W2 // 2
    hd = w_a.shape[-1]
    tc = max(hd, _pick(W, 512))
    assert tc % hd == 0 and W % tc == 0
    nc = W // tc
    nseq, cru, _ = conv_init.shape
    tps = (M // tm) // nseq
    kern = functools.partial(_lru_kernel, R=R, cru=cru, tps=tps)
    row_vec = lambda: pl.BlockSpec((None, 1, tc), lambda i, c: (layer, 0, c))
    gate_w = lambda: pl.BlockSpec((None, tc // hd, hd, hd), lambda i, c: (layer, c, 0, 0))
    unfold = lambda s: jnp.swapaxes(s, 1, 2).reshape(nseq, s.shape[2], W)
    hg, c_rows, h_rows = pl.pallas_call(
        kern,
        grid=(M // tm, nc),
        in_specs=[
            pl.BlockSpec((tm, tc), lambda i, c: (i, c)),
            pl.BlockSpec((tm, tc), lambda i, c: (i, nc + c)),
            pl.BlockSpec((None, conv_w.shape[1], tc), lambda i, c: (layer, 0, c)),
            row_vec(), gate_w(), row_vec(), gate_w(), row_vec(), row_vec(),
            pl.BlockSpec((None, cru, tc), lambda i, c: (i // tps, 0, c)),
            pl.BlockSpec((None, SUBLANES, tc), lambda i, c: (i // tps, 0, c)),
        ],
        out_specs=[
            pl.BlockSpec((tm, tc), lambda i, c: (i, c)),
            pl.BlockSpec((None, nc, cru, tc), lambda i, c: (i // tps, 0, 0, 0)),
            pl.BlockSpec((None, nc, SUBLANES, tc), lambda i, c: (i // tps, 0, 0, 0)),
        ],
        out_shape=[
            jax.ShapeDtypeStruct((M, W), BF16),
            jax.ShapeDtypeStruct((nseq, nc, cru, tc), F32),
            jax.ShapeDtypeStruct((nseq, nc, SUBLANES, tc), F32),
        ],
        scratch_shapes=[
            pltpu.VMEM((cru + tm, tc), F32),
            pltpu.VMEM((tm, tc), F32),
            pltpu.VMEM((tm, tc), F32),
            pltpu.VMEM((tm, tc), F32),
            pltpu.VMEM((nc, cru, tc), F32),
            pltpu.VMEM((nc, SUBLANES, tc), F32),
        ],
        compiler_params=_params(2),
        name="lru_core",
    )(proj, proj, conv_w, conv_b, w_a, b_a, w_i, b_i, lam, conv_init, h_init)
    return hg, unfold(c_rows), unfold(h_rows)


def _ffn_kernel(x_ref, g_ref, wg_ref, wv_ref, cwg_ref, cwv_ref, cbg_ref, cbv_ref, wd_ref, ig_ref, iv_ref,
                o_ref, sg_ref, sv_ref,
                xn_ref, eg_ref, ev_ref, cg_ref, cv_ref, *, R, cr, tps):
    i = pl.program_id(0)
    j = pl.program_id(1)
    tm = x_ref.shape[0]

    @pl.when(j == 0)
    def _():
        x = x_ref[...]
        xn_ref[...] = _rms(x, g_ref[...]).astype(BF16)
        o_ref[...] = x

    @pl.when(i % tps == 0)
    def _():
        cg_ref[j] = ig_ref[...]
        cv_ref[j] = iv_ref[...]

    def side(w_ref, cw_ref, cb_ref, e_ref, c_ref, s_ref):
        up = jnp.dot(xn_ref[...], w_ref[...], preferred_element_type=F32)
        e_ref[0:cr, :] = c_ref[j]
        e_ref[cr:cr + tm, :] = up
        cw = cw_ref[...]
        conv = (cb_ref[...] + cw[2:3] * up + cw[1:2] * e_ref[cr - R:cr - R + tm, :]
                + cw[0:1] * e_ref[cr - 2 * R:cr - 2 * R + tm, :])
        tail = e_ref[tm:tm + cr, :]
        c_ref[j] = tail
        s_ref[j] = tail
        return conv

    cg = side(wg_ref, cwg_ref, cbg_ref, eg_ref, cg_ref, sg_ref)
    cv = side(wv_ref, cwv_ref, cbv_ref, ev_ref, cv_ref, sv_ref)
    act = (_gelu(cg) * cv).astype(BF16)
    o_ref[...] += jnp.dot(act, wd_ref[...], preferred_element_type=F32)


def conv_ffn(x, gains, w_up, conv_w, conv_b, w_down, init, layer, tm, tf, R):
    M, D = x.shape
    F = w_down.shape[1]
    nf = F // tf
    nseq, cr, _ = init.shape
    tps = (M // tm) // nseq
    kern = functools.partial(_ffn_kernel, R=R, cr=cr, tps=tps)
    K = conv_w.shape[1]
    halves = lambda mk: [mk(0), mk(nf)]
    state_spec = lambda off: pl.BlockSpec((None, cr, tf), lambda i, j: (i // tps, 0, off + j))
    out, sg, sv = pl.pallas_call(
        kern,
        grid=(M // tm, nf),
        in_specs=[
            pl.BlockSpec((tm, D), lambda i, j: (i, 0)),
            pl.BlockSpec((None, 1, D), lambda i, j: (layer, 0, 0)),
            *halves(lambda off: pl.BlockSpec((None, D, tf), lambda i, j: (layer, 0, off + j))),
            *halves(lambda off: pl.BlockSpec((None, K, tf), lambda i, j: (layer, 0, off + j))),
            *halves(lambda off: pl.BlockSpec((None, 1, tf), lambda i, j: (layer, 0, off + j))),
            pl.BlockSpec((None, tf, D), lambda i, j: (layer, j, 0)),
            *halves(state_spec),
        ],
        out_specs=[
            pl.BlockSpec((tm, D), lambda i, j: (i, 0)),
            pl.BlockSpec((None, nf, cr, tf), lambda i, j: (i // tps, 0, 0, 0)),
            pl.BlockSpec((None, nf, cr, tf), lambda i, j: (i // tps, 0, 0, 0)),
        ],
        out_shape=[
            jax.ShapeDtypeStruct((M, D), F32),
            jax.ShapeDtypeStruct((nseq, nf, cr, tf), F32),
            jax.ShapeDtypeStruct((nseq, nf, cr, tf), F32),
        ],
        scratch_shapes=[
            pltpu.VMEM((tm, D), BF16),
            pltpu.VMEM((cr + tm, tf), F32),
            pltpu.VMEM((cr + tm, tf), F32),
            pltpu.VMEM((nf, cr, tf), F32),
            pltpu.VMEM((nf, cr, tf), F32),
        ],
        compiler_params=_params(2),
        name="conv_ffn",
    )(x, gains, w_up, w_up, conv_w, conv_w, conv_b, conv_b, w_down, init, init)
    unfold = lambda s: jnp.swapaxes(s, 1, 2).reshape(nseq, cr, F)
    return out, jnp.concatenate([unfold(sg), unfold(sv)], axis=-1)


ATTN_SUPER = max(w for w, _ in ATTN_GROUPS)
ATTN_BATCH = 4


def _attn_prompt_kernel(sl_ref, q0, k0, v0, q1, k1, v1, q2, k2, v2, o_ref,
                        ke0, ve0, ke1, ve1, ke2, ve2, o_scr, l_scr):
    n = pl.program_id(2)
    qs, ks, vs = (q0, q1, q2), (k0, k1, k2), (v0, v1, v2)
    kes, ves = (ke0, ke1, ke2), (ve0, ve1, ve2)
    SB = o_ref.shape[0]
    blk = ATTN_BLK
    row = lax.broadcasted_iota(jnp.int32, (blk, 2 * blk), 0)
    col = lax.broadcasted_iota(jnp.int32, (blk, 2 * blk), 1)
    steps = row + blk - col
    band = jnp.logical_and(steps >= 0, steps <= blk)
    band_cur = jnp.logical_and(band, col >= blk)
    steps_f = steps.astype(F32)
    ones_v = jnp.ones((2 * blk, HEAD_DIM), BF16)
    inv_sqrt = 1.0 / math.sqrt(HEAD_DIM)
    nt = (((1,), (1,)), ((), ()))

    def rows_of(start, dil):
        return pl.ds(start, blk, stride=dil) if dil > 1 else pl.ds(start, blk)

    for g, (win, dil) in enumerate(ATTN_GROUPS):
        q_ref, ke, ve = qs[g], kes[g], ves[g]

        @pl.when(n == 0)
        def _():
            ke[0:win, :] = jnp.zeros((win, HEAD_DIM), F32)
            ve[0:win, :] = jnp.zeros((win, HEAD_DIM), F32)

        @pl.when(n > 0)
        def _():
            ke[0:win, :] = ke[SB:SB + win, :]
            ve[0:win, :] = ve[SB:SB + win, :]

        ke[win:win + SB, :] = ks[g][...]
        ve[win:win + SB, :] = vs[g][...]
        alibi = (sl_ref[g:g + 1, :] * float(-dil)) * steps_f
        bias_full = jnp.where(band, alibi, NEG)
        bias_first = jnp.where(band_cur, alibi, NEG)

        def batch(it, carry):
            infos = []
            for u in range(ATTN_BATCH):
                bidx = it * ATTN_BATCH + u
                j = bidx // dil
                r = bidx % dil
                p0 = j * win + r
                if dil == 1:
                    p0 = pl.multiple_of(p0, blk)
                infos.append((p0, jnp.logical_or(n > 0, j > 0)))
            scores = []
            for p0, _ in infos:
                q = q_ref[rows_of(p0, dil), :].astype(BF16)
                kcat = jnp.concatenate([ke[rows_of(p0, dil), :], ke[rows_of(p0 + win, dil), :]], axis=0).astype(BF16)
                scores.append(lax.dot_general(q, kcat, nt, preferred_element_type=F32))
            probs = []
            for (p0, prev_ok), s in zip(infos, scores):
                s = s * inv_sqrt + jnp.where(prev_ok, bias_full, bias_first)
                m = jnp.max(jnp.maximum(s[:, :blk], s[:, blk:]), axis=-1, keepdims=True)
                probs.append((jnp.exp(s - m).astype(BF16), m))
            for (p0, _), (e, m) in zip(infos, probs):
                vcat = jnp.concatenate([ve[rows_of(p0, dil), :], ve[rows_of(p0 + win, dil), :]], axis=0).astype(BF16)
                res = jnp.dot(e, jnp.concatenate([vcat, ones_v], axis=1), preferred_element_type=F32)
                den = res[:, HEAD_DIM:]
                o_scr[g, rows_of(p0, dil), :] = res[:, :HEAD_DIM] / den
                l_scr[g, rows_of(p0, dil), :] = m + jnp.log(den)
            return carry

        lax.fori_loop(0, SB // (blk * ATTN_BATCH), batch, 0)

    chunk = 2 * blk
    for c0 in range(0, SB, chunk):
        ls = [l_scr[g, c0:c0 + chunk, :] for g in range(N_GROUPS)]
        mm = jnp.maximum(jnp.maximum(ls[0], ls[1]), ls[2])
        es = [jnp.exp(l - mm) for l in ls]
        acc = es[0] * o_scr[0, c0:c0 + chunk, :]
        for g in range(1, N_GROUPS):
            acc = acc + es[g] * o_scr[g, c0:c0 + chunk, :]
        o_ref[c0:c0 + chunk, :] = (acc / (es[0] + es[1] + es[2])).astype(BF16)


def attn_prompt(qkv_slabs, B, S):
    n_slabs, M, _ = qkv_slabs.shape
    SB = ATTN_SUPER
    assert n_slabs == 3 * N_GROUPS * GROUP_HEADS and S % SB == 0 and M == B * S
    assert all(w // d == ATTN_BLK for w, d in ATTN_GROUPS) and (SB // ATTN_BLK) % ATTN_BATCH == 0
    nsb = S // SB
    slopes = jnp.asarray(_alibi_slopes(), F32).T
    slopes = jnp.broadcast_to(slopes[:, :, None], (GROUP_HEADS, N_GROUPS, 2 * ATTN_BLK))

    def slab(g, comp):
        base = (g * 3 + comp) * GROUP_HEADS
        return pl.BlockSpec((None, SB, HEAD_DIM), lambda b, h, n: (base + h, b * nsb + n, 0))

    ext = [pltpu.VMEM((w + SB, HEAD_DIM), F32) for w, _ in ATTN_GROUPS for _kv in range(2)]
    return pl.pallas_call(
        _attn_prompt_kernel,
        grid=(B, GROUP_HEADS, nsb),
        in_specs=[pl.BlockSpec((None, N_GROUPS, 2 * ATTN_BLK), lambda b, h, n: (h, 0, 0))]
        + [slab(g, comp) for g in range(N_GROUPS) for comp in range(3)],
        out_specs=pl.BlockSpec((None, SB, HEAD_DIM), lambda b, h, n: (h, b * nsb + n, 0)),
        out_shape=jax.ShapeDtypeStruct((GROUP_HEADS, M, HEAD_DIM), BF16),
        scratch_shapes=ext + [pltpu.VMEM((N_GROUPS, SB, HEAD_DIM), F32), pltpu.VMEM((N_GROUPS, SB, HEAD_DIM), F32)],
        compiler_params=_params(3),
        name="attn_prompt",
    )(slopes, *([qkv_slabs] * (3 * N_GROUPS)))


def _attn_sample_kernel(sl_ref, qkv_ref, c0_ref, c1_ref, c2_ref, o_ref):
    T = qkv_ref.shape[0]
    H = GROUP_HEADS
    caches = (c0_ref, c1_ref, c2_ref)
    rows = c0_ref.shape[0]
    l_idx = lax.broadcasted_iota(jnp.int32, (rows, H, 1), 0)
    l_f = l_idx.astype(F32)
    inv_sqrt = 1.0 / math.sqrt(HEAD_DIM)
    for t in range(T):
        outs, lses = [], []
        for g, (win, dil) in enumerate(ATTN_GROUPS):
            base = g * 3 * H
            slope = sl_ref[g][:, 0:1]
            q = qkv_ref[t, base:base + H, :]
            ph = 0 if dil == 1 else t
            kc = caches[g][:, ph * 2 * H:ph * 2 * H + H, :]
            vc = caches[g][:, ph * 2 * H + H:(ph + 1) * 2 * H, :]
            sc = jnp.sum(kc * q[None], axis=-1, keepdims=True) * inv_sqrt
            if dil == 1:
                sc = jnp.where(l_idx >= t, sc - slope[None] * (float(rows + t) - l_f), NEG)
                new_ts = list(range(t + 1))
            else:
                sc = sc - (slope[None] * float(dil)) * (float(rows) - l_f)
                new_ts = [t]
            m = jnp.max(sc, axis=0)
            s_new = []
            for t2 in new_ts:
                k2 = qkv_ref[t2, base + H:base + 2 * H, :]
                s2 = jnp.sum(q * k2, axis=-1, keepdims=True) * inv_sqrt - slope * float((t - t2) * dil)
                s_new.append(s2)
                m = jnp.maximum(m, s2)
            ec = jnp.exp(sc - m[None])
            den = jnp.sum(ec, axis=0)
            acc = jnp.sum(ec * vc, axis=0)
            for t2, s2 in zip(new_ts, s_new):
                v2 = qkv_ref[t2, base + 2 * H:base + 3 * H, :]
                e2 = jnp.exp(s2 - m)
                den = den + e2
                acc = acc + e2 * v2
            outs.append(acc / den)
            lses.append(m + jnp.log(den))
        mm = jnp.maximum(jnp.maximum(lses[0], lses[1]), lses[2])
        es = [jnp.exp(l - mm) for l in lses]
        o_ref[t] = (es[0] * outs[0] + es[1] * outs[1] + es[2] * outs[2]) / (es[0] + es[1] + es[2])


def attn_sample(qkv, caches, layer, T, Bs):
    H = GROUP_HEADS
    n_rows = qkv.shape[-1] // HEAD_DIM
    views, specs = [], []
    for (win, dil), c in zip(ATTN_GROUPS, caches):
        assert c.shape[1] == Bs and c.shape[2] == win and win // dil == ATTN_BLK and (dil == 1 or T <= dil)
        phases = min(dil, -(-T // 4) * 4)
        views.append(c.reshape(c.shape[0], Bs, win // dil, dil * 2 * H, HEAD_DIM))
        specs.append(pl.BlockSpec((None, None, win // dil, phases * 2 * H, HEAD_DIM), lambda b: (layer, b, 0, 0, 0)))
    slopes = jnp.broadcast_to(jnp.asarray(_alibi_slopes(), F32)[:, :, None], (N_GROUPS, H, HEAD_DIM))
    q_rows = jnp.swapaxes(qkv.reshape(T, Bs, n_rows, HEAD_DIM), 0, 1)
    out = pl.pallas_call(
        _attn_sample_kernel,
        grid=(Bs,),
        in_specs=[pl.BlockSpec((N_GROUPS, H, HEAD_DIM), lambda b: (0, 0, 0)),
                  pl.BlockSpec((None, T, n_rows, HEAD_DIM), lambda b: (b, 0, 0, 0)), *specs],
        out_specs=pl.BlockSpec((None, T, H, HEAD_DIM), lambda b: (b, 0, 0, 0)),
        out_shape=jax.ShapeDtypeStruct((Bs, T, H, HEAD_DIM), F32),
        compiler_params=_params(1),
        name="attn_sample",
    )(slopes, q_rows, *views)
    return jnp.swapaxes(out, 0, 1).reshape(T * Bs, ATTN_WIDTH)


def _run_trunk(x, nseq, R, tm, tf, lru_h, lru_conv, kv_caches, ffn_conv, P, dims):
    M, D = x.shape
    depth = P["norm_mix"].shape[0]
    new_h, new_lconv, new_fconv, qkvs = [], [], [], []
    for layer in range(depth):
        j = layer // 2
        if layer % 2 == 0:
            proj = norm_matmul(x, P["norm_mix"], layer, P["lru_w_in"], P["lru_b_in"], layer=j, tm=tm)
            hg, c_rows, h_rows = lru_core(proj, P["lru_conv_w"], P["lru_conv_b"], P["lru_w_a"], P["lru_b_a"],
                                          P["lru_w_i"], P["lru_b_i"], P["lru_lambda"], lru_conv[j], lru_h[j],
                                          layer=j, tm=tm, R=R)
            new_h.append(h_rows)
            new_lconv.append(c_rows)
            x = matmul_res(hg, P["lru_w_out"], P["lru_b_out"], x, layer=j, tm=tm)
        else:
            qkv = norm_matmul(x, P["norm_mix"], layer, P["attn_w_qkv"], P["attn_b_zero"], layer=j, tm=tm,
                              slab_out=kv_caches is None)
            qkvs.append(qkv)
            if kv_caches is None:
                o = attn_prompt(qkv, *dims)
            else:
                o = attn_sample(qkv, kv_caches, layer=j, T=dims[0], Bs=dims[1])
            x = matmul_res(o, P["attn_w_o"], P["attn_bo_zero"], x, layer=j, tm=tm)
        x, f_rows = conv_ffn(x, P["norm_ffn"], P["ffn_w_up"], P["ffn_conv_w"], P["ffn_conv_b"], P["ffn_w_down"],
                             ffn_conv[layer], layer=layer, tm=tm, tf=tf, R=R)
        new_fconv.append(f_rows)
    y = final_norm(x, P["norm_final"], tm=tm)
    return y, new_h, new_lconv, qkvs, new_fconv


def kernel(x_prompt, x_sample, cache_kv_w128, cache_kv_w512, cache_kv_w2048, state_lru_h, state_lru_conv, state_ffn_conv, norm_mix, norm_ffn, norm_final, lru_w_in, lru_b_in, lru_conv_w, lru_conv_b, lru_w_a, lru_b_a, lru_w_i, lru_b_i, lru_lambda, lru_w_out, lru_b_out, attn_w_qkv, attn_w_o, ffn_w_up, ffn_conv_w, ffn_conv_b, ffn_w_down):
    B, S, D = x_prompt.shape
    Bs, T, _ = x_sample.shape
    depth = norm_mix.shape[0]
    n_lru, W = lru_lambda.shape
    n_attn = attn_w_qkv.shape[0]
    F2 = ffn_w_up.shape[-1]
    assert Bs == SUBLANES, "the sample group is laid out time-major with one sublane per sequence"
    row3 = lambda a: a.reshape(a.shape[0], 1, a.shape[-1])
    P = {
        "norm_mix": row3(norm_mix), "norm_ffn": row3(norm_ffn), "norm_final": norm_final.reshape(1, D),
        "lru_w_in": lru_w_in.astype(BF16), "lru_b_in": row3(lru_b_in),
        "lru_conv_w": lru_conv_w, "lru_conv_b": row3(lru_conv_b),
        "lru_w_a": lru_w_a.astype(BF16), "lru_b_a": row3(lru_b_a),
        "lru_w_i": lru_w_i.astype(BF16), "lru_b_i": row3(lru_b_i),
        "lru_lambda": row3(lru_lambda),
        "lru_w_out": lru_w_out.astype(BF16), "lru_b_out": row3(lru_b_out),
        "attn_w_qkv": attn_w_qkv.astype(BF16), "attn_w_o": attn_w_o.astype(BF16),
        "attn_b_zero": jnp.zeros((n_attn, 1, attn_w_qkv.shape[-1]), F32),
        "attn_bo_zero": jnp.zeros((n_attn, 1, D), F32),
        "ffn_w_up": ffn_w_up.astype(BF16), "ffn_conv_w": ffn_conv_w, "ffn_conv_b": row3(ffn_conv_b),
        "ffn_w_down": ffn_w_down.astype(BF16),
    }
    kc = lru_conv_w.shape[1] - 1
    kf = ffn_conv_w.shape[1] - 1

    tm_p = 512 if S % 512 == 0 else S
    zeros = lambda n, w: [jnp.zeros((B, SUBLANES, w), F32)] * n
    y_p, h_p, lc_p, qkv_p, fc_p = _run_trunk(
        x_prompt.reshape(B * S, D), B, 1, tm_p, _pick(F2 // 2, 512), zeros(n_lru, W), zeros(n_lru, W), None, zeros(depth, F2), P, (B, S))

    tmaj = lambda a: jnp.swapaxes(a, 0, 1).reshape(1, a.shape[1] * a.shape[0], a.shape[-1])
    y_s, h_s, lc_s, qkv_s, fc_s = _run_trunk(
        jnp.swapaxes(x_sample, 0, 1).reshape(T * Bs, D), 1, Bs, T * Bs, _pick(F2 // 2, 1024),
        [state_lru_h[j].reshape(1, Bs, W) for j in range(n_lru)],
        [tmaj(state_lru_conv[j]) for j in range(n_lru)],
        (cache_kv_w128, cache_kv_w512, cache_kv_w2048),
        [tmaj(state_ffn_conv[l]) for l in range(depth)], P, (T, Bs))

    bmaj = lambda a, k: jnp.swapaxes(a.reshape(-1, Bs, a.shape[-1])[-k:], 0, 1)
    kv_p, kv_s = [], []
    for g, (win, dil) in enumerate(ATTN_GROUPS):
        lo = (g * 3 + 1) * ATTN_WIDTH
        keep = min(win, S)
        kv_p.append(jnp.stack([jnp.transpose(
            q.reshape(N_GROUPS, 3, GROUP_HEADS, B, S, HEAD_DIM)[g, 1:3, :, :, S - keep:, :], (2, 3, 0, 1, 4))
            for q in qkv_p], axis=0))
        kv_s.append(jnp.stack([jnp.swapaxes(q.reshape(T, Bs, -1), 0, 1)[:, :, lo:lo + 2 * ATTN_WIDTH]
                               .reshape(Bs, T, 2, GROUP_HEADS, HEAD_DIM) for q in qkv_s], axis=0))
    return (
        y_p.reshape(B, S, D),
        jnp.swapaxes(y_s.reshape(T, Bs, D), 0, 1),
        kv_p[0], kv_p[1], kv_p[2],
        jnp.stack([h[:, SUBLANES - 1] for h in h_p], axis=0),
        jnp.stack([c[:, SUBLANES - kc:] for c in lc_p], axis=0),
        jnp.stack([f[:, SUBLANES - kf:] for f in fc_p], axis=0),
        kv_s[0], kv_s[1], kv_s[2],
        jnp.stack([h[0] for h in h_s], axis=0),
        jnp.stack([bmaj(c[0], kc) for c in lc_s], axis=0),
        jnp.stack([bmaj(f[0], kf) for f in fc_s], axis=0),
    )
```

```python
import functools
import math
from typing import NamedTuple

import jax
import jax.numpy as jnp
from jax import lax
from jax.experimental import pallas as pl
from jax.experimental.pallas import tpu as pltpu

EPS = 1e-6
NEG = -1e30
LRU_C = 8.0
ATTN_GROUPS = ((128, 1), (512, 4), (2048, 16))
N_GROUPS = len(ATTN_GROUPS)
GROUP_HEADS = 8
HEAD_DIM = 128
ATTN_WIDTH = GROUP_HEADS * HEAD_DIM
ATTN_BLK = 128
SUBLANES = 8
LANES = 128
VMEM_LIMIT = 56 * 1024 * 1024
FFN_ROW_CHUNK = 256

F32 = jnp.float32
BF16 = jnp.bfloat16


def _alibi_slopes():
    n = N_GROUPS * GROUP_HEADS
    return [[2.0 ** (-8.0 * (g * GROUP_HEADS + h + 1) / n) for h in range(GROUP_HEADS)] for g in range(N_GROUPS)]


def _gelu(x):
    c = math.sqrt(2.0 / math.pi)
    return x * (0.5 * (1.0 + jnp.tanh(c * (x + 0.044715 * (x * x * x)))))


def _rms(x, g):
    ms = jnp.mean(x * x, axis=-1, keepdims=True)
    return x * lax.rsqrt(ms + EPS) * g


def _pick(n, cap):
    best = None
    for t in range(LANES, min(n, cap) + 1, LANES):
        if n % t == 0:
            best = t
    assert best is not None, (n, cap)
    return best


def _params(n_axes):
    return pltpu.CompilerParams(dimension_semantics=("arbitrary",) * n_axes, vmem_limit_bytes=VMEM_LIMIT)


def _norm_matmul_kernel(x_ref, g_ref, w_ref, b_ref, o_ref, xn_ref):
    @pl.when(pl.program_id(1) == 0)
    def _():
        xn_ref[...] = _rms(x_ref[...], g_ref[...]).astype(BF16)

    acc = jnp.dot(xn_ref[...], w_ref[...], preferred_element_type=F32) + b_ref[...]
    if len(o_ref.shape) == 2:
        o_ref[...] = acc
    else:
        for c in range(o_ref.shape[0]):
            o_ref[c] = acc[:, c * LANES:(c + 1) * LANES]


def norm_matmul(x, gains, g_layer, w, bias, layer, tm, slab_out=False):
    M, D = x.shape
    N = w.shape[-1]
    tn = _pick(N, 1024 if tm > 256 else 2048)
    if slab_out:
        out_spec = pl.BlockSpec((tn // LANES, tm, LANES), lambda i, j: (j, i, 0))
        out_shape = jax.ShapeDtypeStruct((N // LANES, M, LANES), F32)
    else:
        out_spec = pl.BlockSpec((tm, tn), lambda i, j: (i, j))
        out_shape = jax.ShapeDtypeStruct((M, N), F32)
    return pl.pallas_call(
        _norm_matmul_kernel,
        grid=(M // tm, N // tn),
        in_specs=[
            pl.BlockSpec((tm, D), lambda i, j: (i, 0)),
            pl.BlockSpec((None, 1, D), lambda i, j: (g_layer, 0, 0)),
            pl.BlockSpec((None, D, tn), lambda i, j: (layer, 0, j)),
            pl.BlockSpec((None, 1, tn), lambda i, j: (layer, 0, j)),
        ],
        out_specs=out_spec,
        out_shape=out_shape,
        scratch_shapes=[pltpu.VMEM((tm, D), BF16)],
        compiler_params=_params(2),
        name="norm_matmul",
    )(x, gains, w, bias)


def _matmul_res_kernel(a_ref, w_ref, b_ref, r_ref, o_ref):
    if len(a_ref.shape) == 2:
        a = a_ref[...]
    else:
        a = jnp.concatenate([a_ref[c] for c in range(a_ref.shape[0])], axis=1)
    o_ref[...] = r_ref[...] + b_ref[...] + jnp.dot(a.astype(BF16), w_ref[...], preferred_element_type=F32)


def matmul_res(a, w, bias, res, layer, tm):
    M, N = res.shape
    K = w.shape[1]
    tn = _pick(N, 1024 if tm > 256 else 2048)
    if a.ndim == 2:
        a_spec = pl.BlockSpec((tm, K), lambda i, j: (i, 0))
    else:
        a_spec = pl.BlockSpec((K // LANES, tm, LANES), lambda i, j: (0, i, 0))
    return pl.pallas_call(
        _matmul_res_kernel,
        grid=(M // tm, N // tn),
        in_specs=[
            a_spec,
            pl.BlockSpec((None, K, tn), lambda i, j: (layer, 0, j)),
            pl.BlockSpec((None, 1, tn), lambda i, j: (layer, 0, j)),
            pl.BlockSpec((tm, tn), lambda i, j: (i, j)),
        ],
        out_specs=pl.BlockSpec((tm, tn), lambda i, j: (i, j)),
        out_shape=jax.ShapeDtypeStruct((M, N), F32),
        compiler_params=_params(2),
        name="matmul_res",
    )(a, w, bias, res)


def _norm_kernel(x_ref, g_ref, o_ref):
    o_ref[...] = _rms(x_ref[...], g_ref[...])


def final_norm(x, gain, tm):
    M, D = x.shape
    return pl.pallas_call(
        _norm_kernel,
        grid=(M // tm,),
        in_specs=[pl.BlockSpec((tm, D), lambda i: (i, 0)), pl.BlockSpec((1, D), lambda i: (0, 0))],
        out_specs=pl.BlockSpec((tm, D), lambda i: (i, 0)),
        out_shape=jax.ShapeDtypeStruct((M, D), F32),
        compiler_params=_params(1),
        name="final_norm",
    )(x, gain)


def _lru_kernel(gate_ref, u_ref, cw_ref, cb_ref, wa_ref, ba_ref, wi_ref, bi_ref, lam_ref, ci_ref, hi_ref,
                o_ref, cs_ref, hs_ref,
                e_ref, a_scr, b_scr, h_scr, cc_ref, hc_ref, *, R, cru, tps):
    i = pl.program_id(0)
    c = pl.program_id(1)
    tm, tc = u_ref.shape

    @pl.when(i % tps == 0)
    def _():
        cc_ref[c] = ci_ref[...]
        hc_ref[c] = hi_ref[...]

    u = u_ref[...]
    e_ref[0:cru, :] = cc_ref[c]
    e_ref[cru:cru + tm, :] = u
    cw = cw_ref[...]
    uc = (cb_ref[...] + cw[3:4] * u + cw[2:3] * e_ref[cru - R:cru - R + tm, :]
          + cw[1:2] * e_ref[cru - 2 * R:cru - 2 * R + tm, :] + cw[0:1] * e_ref[cru - 3 * R:cru - 3 * R + tm, :])
    tail = e_ref[tm:tm + cru, :]
    cc_ref[c] = tail
    cs_ref[c] = tail

    hd = wa_ref.shape[-1]
    ucb = uc.astype(BF16)
    ra, ri = [], []
    for hh in range(tc // hd):
        ub = ucb[:, hh * hd:(hh + 1) * hd]
        ra.append(jnp.dot(ub, wa_ref[hh], preferred_element_type=F32))
        ri.append(jnp.dot(ub, wi_ref[hh], preferred_element_type=F32))
    r = jax.nn.sigmoid(jnp.concatenate(ra, axis=1) + ba_ref[...])
    ig = jax.nn.sigmoid(jnp.concatenate(ri, axis=1) + bi_ref[...])
    nlam = -lam_ref[...]
    softplus = jnp.maximum(nlam, 0.0) + jnp.log1p(jnp.exp(-jnp.abs(nlam)))
    log_a = (-LRU_C) * r * softplus
    a = jnp.exp(log_a)
    th = jnp.tanh(log_a)
    bx = jnp.sqrt((-2.0 * th) / (1.0 - th)) * (ig * uc)

    if R == 1:
        A = a.reshape(tm // SUBLANES, SUBLANES, tc)
        B = bx.reshape(tm // SUBLANES, SUBLANES, tc)
        row = lax.broadcasted_iota(jnp.int32, A.shape, 1)
        s = 1
        while s < SUBLANES:
            m = row >= s
            B = jnp.where(m, A * pltpu.roll(B, s, axis=1) + B, B)
            A = jnp.where(m, A * pltpu.roll(A, s, axis=1), A)
            s *= 2
        a_scr[...] = A.reshape(tm, tc)
        b_scr[...] = B.reshape(tm, tc)
        h0 = hc_ref[c][SUBLANES - 1:SUBLANES, :]
    else:
        assert R == SUBLANES
        a_scr[...] = a
        b_scr[...] = bx
        h0 = hc_ref[c]

    def body(g, h):
        r0 = pl.multiple_of(g * SUBLANES, SUBLANES)
        hg = b_scr[pl.ds(r0, SUBLANES), :] + a_scr[pl.ds(r0, SUBLANES), :] * h
        h_scr[pl.ds(r0, SUBLANES), :] = hg
        return hg[SUBLANES - 1:SUBLANES, :] if R == 1 else hg

    n_groups = tm // SUBLANES
    lax.fori_loop(0, n_groups, body, h0, unroll=min(n_groups, 8))
    h_tail = h_scr[tm - SUBLANES:tm, :]
    hc_ref[c] = h_tail
    hs_ref[c] = h_tail
    o_ref[...] = (h_scr[...] * _gelu(gate_ref[...])).astype(BF16)


def lru_core(proj, conv_w, conv_b, w_a, b_a, w_i, b_i, lam, conv_init, h_init, layer, tm, R):
    M, W2 = proj.shape
    W = W2 // 2
    hd = w_a.shape[-1]
    tc = max(hd, _pick(W, 512))
    assert tc % hd == 0 and W % tc == 0
    nc = W // tc
    nseq, cru, _ = conv_init.shape
    tps = (M // tm) // nseq
    kern = functools.partial(_lru_kernel, R=R, cru=cru, tps=tps)
    row_vec = lambda: pl.BlockSpec((None, 1, tc), lambda i, c: (layer, 0, c))
    gate_w = lambda: pl.BlockSpec((None, tc // hd, hd, hd), lambda i, c: (layer, c, 0, 0))
    unfold = lambda s: jnp.swapaxes(s, 1, 2).reshape(nseq, s.shape[2], W)
    hg, c_rows, h_rows = pl.pallas_call(
        kern,
        grid=(M // tm, nc),
        in_specs=[
            pl.BlockSpec((tm, tc), lambda i, c: (i, c)),
            pl.BlockSpec((tm, tc), lambda i, c: (i, nc + c)),
            pl.BlockSpec((None, conv_w.shape[1], tc), lambda i, c: (layer, 0, c)),
            row_vec(), gate_w(), row_vec(), gate_w(), row_vec(), row_vec(),
            pl.BlockSpec((None, cru, tc), lambda i, c: (i // tps, 0, c)),
            pl.BlockSpec((None, SUBLANES, tc), lambda i, c: (i // tps, 0, c)),
        ],
        out_specs=[
            pl.BlockSpec((tm, tc), lambda i, c: (i, c)),
            pl.BlockSpec((None, nc, cru, tc), lambda i, c: (i // tps, 0, 0, 0)),
            pl.BlockSpec((None, nc, SUBLANES, tc), lambda i, c: (i // tps, 0, 0, 0)),
        ],
        out_shape=[
            jax.ShapeDtypeStruct((M, W), BF16),
            jax.ShapeDtypeStruct((nseq, nc, cru, tc), F32),
            jax.ShapeDtypeStruct((nseq, nc, SUBLANES, tc), F32),
        ],
        scratch_shapes=[
            pltpu.VMEM((cru + tm, tc), F32),
            pltpu.VMEM((tm, tc), F32),
            pltpu.VMEM((tm, tc), F32),
            pltpu.VMEM((tm, tc), F32),
            pltpu.VMEM((nc, cru, tc), F32),
            pltpu.VMEM((nc, SUBLANES, tc), F32),
        ],
        compiler_params=_params(2),
        name="lru_core",
    )(proj, proj, conv_w, conv_b, w_a, b_a, w_i, b_i, lam, conv_init, h_init)
    return hg, unfold(c_rows), unfold(h_rows)


def _ffn_kernel(x_ref, g_ref, wg_ref, wv_ref, cwg_ref, cwv_ref, cbg_ref, cbv_ref, wd_ref, ig_ref, iv_ref,
                o_ref, sg_ref, sv_ref,
                xn_ref, eg_ref, ev_ref, cg_ref, cv_ref, *, R, cr, tps, rc):
    i = pl.program_id(0)
    j = pl.program_id(1)
    tm = x_ref.shape[0]

    @pl.when(j == 0)
    def _():
        x = x_ref[...]
        xn_ref[...] = _rms(x, g_ref[...]).astype(BF16)
        o_ref[...] = x

    @pl.when(i % tps == 0)
    def _():
        cg_ref[j] = ig_ref[...]
        cv_ref[j] = iv_ref[...]

    eg_ref[0:cr, :] = cg_ref[j]
    ev_ref[0:cr, :] = cv_ref[j]

    def side(xc, r0, w_ref, cw_ref, cb_ref, e_ref):
        up = jnp.dot(xc, w_ref[...], preferred_element_type=F32)
        e_ref[cr + r0:cr + r0 + rc, :] = up
        cw = cw_ref[...]
        return (cb_ref[...] + cw[2:3] * up + cw[1:2] * e_ref[cr + r0 - R:cr + r0 - R + rc, :]
                + cw[0:1] * e_ref[cr + r0 - 2 * R:cr + r0 - 2 * R + rc, :])

    for r0 in range(0, tm, rc):
        xc = xn_ref[r0:r0 + rc, :]
        cg = side(xc, r0, wg_ref, cwg_ref, cbg_ref, eg_ref)
        cv = side(xc, r0, wv_ref, cwv_ref, cbv_ref, ev_ref)
        act = (_gelu(cg) * cv).astype(BF16)
        o_ref[r0:r0 + rc, :] += jnp.dot(act, wd_ref[...], preferred_element_type=F32)

    for e_ref, c_ref, s_ref in ((eg_ref, cg_ref, sg_ref), (ev_ref, cv_ref, sv_ref)):
        tail = e_ref[tm:tm + cr, :]
        c_ref[j] = tail
        s_ref[j] = tail


def conv_ffn(x, gains, w_up, conv_w, conv_b, w_down, init, layer, tm, tf, R):
    M, D = x.shape
    F = w_down.shape[1]
    nf = F // tf
    nseq, cr, _ = init.shape
    tps = (M // tm) // nseq
    kern = functools.partial(_ffn_kernel, R=R, cr=cr, tps=tps, rc=min(tm, FFN_ROW_CHUNK))
    K = conv_w.shape[1]
    halves = lambda mk: [mk(0), mk(nf)]
    state_spec = lambda off: pl.BlockSpec((None, cr, tf), lambda i, j: (i // tps, 0, off + j))
    once = pl.Buffered(1)
    out, sg, sv = pl.pallas_call(
        kern,
        grid=(M // tm, nf),
        in_specs=[
            pl.BlockSpec((tm, D), lambda i, j: (i, 0), pipeline_mode=once),
            pl.BlockSpec((None, 1, D), lambda i, j: (layer, 0, 0)),
            *halves(lambda off: pl.BlockSpec((None, D, tf), lambda i, j: (layer, 0, off + j))),
            *halves(lambda off: pl.BlockSpec((None, K, tf), lambda i, j: (layer, 0, off + j))),
            *halves(lambda off: pl.BlockSpec((None, 1, tf), lambda i, j: (layer, 0, off + j))),
            pl.BlockSpec((None, tf, D), lambda i, j: (layer, j, 0)),
            *halves(state_spec),
        ],
        out_specs=[
            pl.BlockSpec((tm, D), lambda i, j: (i, 0), pipeline_mode=once),
            pl.BlockSpec((None, nf, cr, tf), lambda i, j: (i // tps, 0, 0, 0)),
            pl.BlockSpec((None, nf, cr, tf), lambda i, j: (i // tps, 0, 0, 0)),
        ],
        out_shape=[
            jax.ShapeDtypeStruct((M, D), F32),
            jax.ShapeDtypeStruct((nseq, nf, cr, tf), F32),
            jax.ShapeDtypeStruct((nseq, nf, cr, tf), F32),
        ],
        scratch_shapes=[
            pltpu.VMEM((tm, D), BF16),
            pltpu.VMEM((cr + tm, tf), F32),
            pltpu.VMEM((cr + tm, tf), F32),
            pltpu.VMEM((nf, cr, tf), F32),
            pltpu.VMEM((nf, cr, tf), F32),
        ],
        compiler_params=_params(2),
        name="conv_ffn",
    )(x, gains, w_up, w_up, conv_w, conv_w, conv_b, conv_b, w_down, init, init)
    unfold = lambda s: jnp.swapaxes(s, 1, 2).reshape(nseq, cr, F)
    return out, jnp.concatenate([unfold(sg), unfold(sv)], axis=-1)


ATTN_SUPER = max(w for w, _ in ATTN_GROUPS)
ATTN_BATCH = 4


def _attn_prompt_kernel(sl_ref, q0, k0, v0, q1, k1, v1, q2, k2, v2, o_ref,
                        ke0, ve0, ke1, ve1, ke2, ve2, o_scr, l_scr):
    n = pl.program_id(2)
    qs, ks, vs = (q0, q1, q2), (k0, k1, k2), (v0, v1, v2)
    kes, ves = (ke0, ke1, ke2), (ve0, ve1, ve2)
    SB = o_ref.shape[0]
    blk = ATTN_BLK
    row = lax.broadcasted_iota(jnp.int32, (blk, 2 * blk), 0)
    col = lax.broadcasted_iota(jnp.int32, (blk, 2 * blk), 1)
    steps = row + blk - col
    band = jnp.logical_and(steps >= 0, steps <= blk)
    band_cur = jnp.logical_and(band, col >= blk)
    steps_f = steps.astype(F32)
    ones_v = jnp.ones((2 * blk, HEAD_DIM), BF16)
    inv_sqrt = 1.0 / math.sqrt(HEAD_DIM)
    nt = (((1,), (1,)), ((), ()))

    def rows_of(start, dil):
        return pl.ds(start, blk, stride=dil) if dil > 1 else pl.ds(start, blk)

    for g, (win, dil) in enumerate(ATTN_GROUPS):
        q_ref, ke, ve = qs[g], kes[g], ves[g]

        @pl.when(n == 0)
        def _():
            ke[0:win, :] = jnp.zeros((win, HEAD_DIM), F32)
            ve[0:win, :] = jnp.zeros((win, HEAD_DIM), F32)

        @pl.when(n > 0)
        def _():
            ke[0:win, :] = ke[SB:SB + win, :]
            ve[0:win, :] = ve[SB:SB + win, :]

        ke[win:win + SB, :] = ks[g][...]
        ve[win:win + SB, :] = vs[g][...]
        alibi = (sl_ref[g:g + 1, :] * float(-dil)) * steps_f
        bias_full = jnp.where(band, alibi, NEG)
        bias_first = jnp.where(band_cur, alibi, NEG)

        def batch(it, carry):
            infos = []
            for u in range(ATTN_BATCH):
                bidx = it * ATTN_BATCH + u
                j = bidx // dil
                r = bidx % dil
                p0 = j * win + r
                if dil == 1:
                    p0 = pl.multiple_of(p0, blk)
                infos.append((p0, jnp.logical_or(n > 0, j > 0)))
            scores = []
            for p0, _ in infos:
                q = q_ref[rows_of(p0, dil), :].astype(BF16)
                kcat = jnp.concatenate([ke[rows_of(p0, dil), :], ke[rows_of(p0 + win, dil), :]], axis=0).astype(BF16)
                scores.append(lax.dot_general(q, kcat, nt, preferred_element_type=F32))
            probs = []
            for (p0, prev_ok), s in zip(infos, scores):
                s = s * inv_sqrt + jnp.where(prev_ok, bias_full, bias_first)
                m = jnp.max(jnp.maximum(s[:, :blk], s[:, blk:]), axis=-1, keepdims=True)
                probs.append((jnp.exp(s - m).astype(BF16), m))
            for (p0, _), (e, m) in zip(infos, probs):
                vcat = jnp.concatenate([ve[rows_of(p0, dil), :], ve[rows_of(p0 + win, dil), :]], axis=0).astype(BF16)
                res = jnp.dot(e, jnp.concatenate([vcat, ones_v], axis=1), preferred_element_type=F32)
                den = res[:, HEAD_DIM:]
                o_scr[g, rows_of(p0, dil), :] = res[:, :HEAD_DIM] / den
                l_scr[g, rows_of(p0, dil), :] = m + jnp.log(den)
            return carry

        lax.fori_loop(0, SB // (blk * ATTN_BATCH), batch, 0)

    chunk = 2 * blk
    for c0 in range(0, SB, chunk):
        ls = [l_scr[g, c0:c0 + chunk, :] for g in range(N_GROUPS)]
        mm = jnp.maximum(jnp.maximum(ls[0], ls[1]), ls[2])
        es = [jnp.exp(l - mm) for l in ls]
        acc = es[0] * o_scr[0, c0:c0 + chunk, :]
        for g in range(1, N_GROUPS):
            acc = acc + es[g] * o_scr[g, c0:c0 + chunk, :]
        o_ref[c0:c0 + chunk, :] = (acc / (es[0] + es[1] + es[2])).astype(BF16)


def attn_prompt(qkv_slabs, B, S):
    n_slabs, M, _ = qkv_slabs.shape
    SB = ATTN_SUPER
    assert n_slabs == 3 * N_GROUPS * GROUP_HEADS and S % SB == 0 and M == B * S
    assert all(w // d == ATTN_BLK for w, d in ATTN_GROUPS) and (SB // ATTN_BLK) % ATTN_BATCH == 0
    nsb = S // SB
    slopes = jnp.asarray(_alibi_slopes(), F32).T
    slopes = jnp.broadcast_to(slopes[:, :, None], (GROUP_HEADS, N_GROUPS, 2 * ATTN_BLK))

    def slab(g, comp):
        base = (g * 3 + comp) * GROUP_HEADS
        return pl.BlockSpec((None, SB, HEAD_DIM), lambda b, h, n: (base + h, b * nsb + n, 0))

    ext = [pltpu.VMEM((w + SB, HEAD_DIM), F32) for w, _ in ATTN_GROUPS for _kv in range(2)]
    return pl.pallas_call(
        _attn_prompt_kernel,
        grid=(B, GROUP_HEADS, nsb),
        in_specs=[pl.BlockSpec((None, N_GROUPS, 2 * ATTN_BLK), lambda b, h, n: (h, 0, 0))]
        + [slab(g, comp) for g in range(N_GROUPS) for comp in range(3)],
        out_specs=pl.BlockSpec((None, SB, HEAD_DIM), lambda b, h, n: (h, b * nsb + n, 0)),
        out_shape=jax.ShapeDtypeStruct((GROUP_HEADS, M, HEAD_DIM), BF16),
        scratch_shapes=ext + [pltpu.VMEM((N_GROUPS, SB, HEAD_DIM), F32), pltpu.VMEM((N_GROUPS, SB, HEAD_DIM), F32)],
        compiler_params=_params(3),
        name="attn_prompt",
    )(slopes, *([qkv_slabs] * (3 * N_GROUPS)))


def _attn_sample_kernel(sl_ref, qkv_ref, c0_ref, c1_ref, c2_ref, o_ref):
    T = qkv_ref.shape[0]
    H = GROUP_HEADS
    caches = (c0_ref, c1_ref, c2_ref)
    rows = c0_ref.shape[0]
    l_idx = lax.broadcasted_iota(jnp.int32, (rows, H, 1), 0)
    l_f = l_idx.astype(F32)
    inv_sqrt = 1.0 / math.sqrt(HEAD_DIM)
    for t in range(T):
        outs, lses = [], []
        for g, (win, dil) in enumerate(ATTN_GROUPS):
            base = g * 3 * H
            slope = sl_ref[g][:, 0:1]
            q = qkv_ref[t, base:base + H, :]
            ph = 0 if dil == 1 else t
            kc = caches[g][:, ph * 2 * H:ph * 2 * H + H, :]
            vc = caches[g][:, ph * 2 * H + H:(ph + 1) * 2 * H, :]
            sc = jnp.sum(kc * q[None], axis=-1, keepdims=True) * inv_sqrt
            if dil == 1:
                sc = jnp.where(l_idx >= t, sc - slope[None] * (float(rows + t) - l_f), NEG)
                new_ts = list(range(t + 1))
            else:
                sc = sc - (slope[None] * float(dil)) * (float(rows) - l_f)
                new_ts = [t]
            m = jnp.max(sc, axis=0)
            s_new = []
            for t2 in new_ts:
                k2 = qkv_ref[t2, base + H:base + 2 * H, :]
                s2 = jnp.sum(q * k2, axis=-1, keepdims=True) * inv_sqrt - slope * float((t - t2) * dil)
                s_new.append(s2)
                m = jnp.maximum(m, s2)
            ec = jnp.exp(sc - m[None])
            den = jnp.sum(ec, axis=0)
            acc = jnp.sum(ec * vc, axis=0)
            for t2, s2 in zip(new_ts, s_new):
                v2 = qkv_ref[t2, base + 2 * H:base + 3 * H, :]
                e2 = jnp.exp(s2 - m)
                den = den + e2
                acc = acc + e2 * v2
            outs.append(acc / den)
            lses.append(m + jnp.log(den))
        mm = jnp.maximum(jnp.maximum(lses[0], lses[1]), lses[2])
        es = [jnp.exp(l - mm) for l in lses]
        o_ref[t] = (es[0] * outs[0] + es[1] * outs[1] + es[2] * outs[2]) / (es[0] + es[1] + es[2])


def attn_sample(qkv, caches, layer, T, Bs):
    H = GROUP_HEADS
    n_rows = qkv.shape[-1] // HEAD_DIM
    views, specs = [], []
    for (win, dil), c in zip(ATTN_GROUPS, caches):
        assert c.shape[1] == Bs and c.shape[2] == win and win // dil == ATTN_BLK and (dil == 1 or T <= dil)
        phases = min(dil, -(-T // 4) * 4)
        views.append(c.reshape(c.shape[0], Bs, win // dil, dil * 2 * H, HEAD_DIM))
        specs.append(pl.BlockSpec((None, None, win // dil, phases * 2 * H, HEAD_DIM), lambda b: (layer, b, 0, 0, 0)))
    slopes = jnp.broadcast_to(jnp.asarray(_alibi_slopes(), F32)[:, :, None], (N_GROUPS, H, HEAD_DIM))
    q_rows = jnp.swapaxes(qkv.reshape(T, Bs, n_rows, HEAD_DIM), 0, 1)
    out = pl.pallas_call(
        _attn_sample_kernel,
        grid=(Bs,),
        in_specs=[pl.BlockSpec((N_GROUPS, H, HEAD_DIM), lambda b: (0, 0, 0)),
                  pl.BlockSpec((None, T, n_rows, HEAD_DIM), lambda b: (b, 0, 0, 0)), *specs],
        out_specs=pl.BlockSpec((None, T, H, HEAD_DIM), lambda b: (b, 0, 0, 0)),
        out_shape=jax.ShapeDtypeStruct((Bs, T, H, HEAD_DIM), F32),
        compiler_params=_params(1),
        name="attn_sample",
    )(slopes, q_rows, *views)
    return jnp.swapaxes(out, 0, 1).reshape(T * Bs, ATTN_WIDTH)


class Tiles(NamedTuple):
    seq: int
    mm: int
    ffn: int
    tf: int


def _run_trunk(x, nseq, R, tiles, lru_h, lru_conv, kv_caches, ffn_conv, P, dims):
    M, D = x.shape
    depth = P["norm_mix"].shape[0]
    new_h, new_lconv, new_fconv, qkvs = [], [], [], []
    for layer in range(depth):
        j = layer // 2
        if layer % 2 == 0:
            proj = norm_matmul(x, P["norm_mix"], layer, P["lru_w_in"], P["lru_b_in"], layer=j, tm=tiles.mm)
            hg, c_rows, h_rows = lru_core(proj, P["lru_conv_w"], P["lru_conv_b"], P["lru_w_a"], P["lru_b_a"],
                                          P["lru_w_i"], P["lru_b_i"], P["lru_lambda"], lru_conv[j], lru_h[j],
                                          layer=j, tm=tiles.seq, R=R)
            new_h.append(h_rows)
            new_lconv.append(c_rows)
            x = matmul_res(hg, P["lru_w_out"], P["lru_b_out"], x, layer=j, tm=tiles.mm)
        else:
            qkv = norm_matmul(x, P["norm_mix"], layer, P["attn_w_qkv"], P["attn_b_zero"], layer=j, tm=tiles.mm,
                              slab_out=kv_caches is None)
            qkvs.append(qkv)
            if kv_caches is None:
                o = attn_prompt(qkv, *dims)
            else:
                o = attn_sample(qkv, kv_caches, layer=j, T=dims[0], Bs=dims[1])
            x = matmul_res(o, P["attn_w_o"], P["attn_bo_zero"], x, layer=j, tm=tiles.mm)
        x, f_rows = conv_ffn(x, P["norm_ffn"], P["ffn_w_up"], P["ffn_conv_w"], P["ffn_conv_b"], P["ffn_w_down"],
                             ffn_conv[layer], layer=layer, tm=tiles.ffn, tf=tiles.tf, R=R)
        new_fconv.append(f_rows)
    y = final_norm(x, P["norm_final"], tm=tiles.seq)
    return y, new_h, new_lconv, qkvs, new_fconv


def kernel(x_prompt, x_sample, cache_kv_w128, cache_kv_w512, cache_kv_w2048, state_lru_h, state_lru_conv, state_ffn_conv, norm_mix, norm_ffn, norm_final, lru_w_in, lru_b_in, lru_conv_w, lru_conv_b, lru_w_a, lru_b_a, lru_w_i, lru_b_i, lru_lambda, lru_w_out, lru_b_out, attn_w_qkv, attn_w_o, ffn_w_up, ffn_conv_w, ffn_conv_b, ffn_w_down):
    B, S, D = x_prompt.shape
    Bs, T, _ = x_sample.shape
    depth = norm_mix.shape[0]
    n_lru, W = lru_lambda.shape
    n_attn = attn_w_qkv.shape[0]
    F2 = ffn_w_up.shape[-1]
    assert Bs == SUBLANES, "the sample group is laid out time-major with one sublane per sequence"
    row3 = lambda a: a.reshape(a.shape[0], 1, a.shape[-1])
    P = {
        "norm_mix": row3(norm_mix), "norm_ffn": row3(norm_ffn), "norm_final": norm_final.reshape(1, D),
        "lru_w_in": lru_w_in.astype(BF16), "lru_b_in": row3(lru_b_in),
        "lru_conv_w": lru_conv_w, "lru_conv_b": row3(lru_conv_b),
        "lru_w_a": lru_w_a.astype(BF16), "lru_b_a": row3(lru_b_a),
        "lru_w_i": lru_w_i.astype(BF16), "lru_b_i": row3(lru_b_i),
        "lru_lambda": row3(lru_lambda),
        "lru_w_out": lru_w_out.astype(BF16), "lru_b_out": row3(lru_b_out),
        "attn_w_qkv": attn_w_qkv.astype(BF16), "attn_w_o": attn_w_o.astype(BF16),
        "attn_b_zero": jnp.zeros((n_attn, 1, attn_w_qkv.shape[-1]), F32),
        "attn_bo_zero": jnp.zeros((n_attn, 1, D), F32),
        "ffn_w_up": ffn_w_up.astype(BF16), "ffn_conv_w": ffn_conv_w, "ffn_conv_b": row3(ffn_conv_b),
        "ffn_w_down": ffn_w_down.astype(BF16),
    }
    kc = lru_conv_w.shape[1] - 1
    kf = ffn_conv_w.shape[1] - 1

    row_tile = lambda cap: cap if S % cap == 0 else S
    tiles_p = Tiles(seq=row_tile(512), mm=row_tile(1024), ffn=row_tile(1024), tf=_pick(F2 // 2, 512))
    zeros = lambda n, w: [jnp.zeros((B, SUBLANES, w), F32)] * n
    y_p, h_p, lc_p, qkv_p, fc_p = _run_trunk(
        x_prompt.reshape(B * S, D), B, 1, tiles_p, zeros(n_lru, W), zeros(n_lru, W), None, zeros(depth, F2), P, (B, S))

    tmaj = lambda a: jnp.swapaxes(a, 0, 1).reshape(1, a.shape[1] * a.shape[0], a.shape[-1])
    y_s, h_s, lc_s, qkv_s, fc_s = _run_trunk(
        jnp.swapaxes(x_sample, 0, 1).reshape(T * Bs, D), 1, Bs,
        Tiles(seq=T * Bs, mm=T * Bs, ffn=T * Bs, tf=_pick(F2 // 2, 1024)),
        [state_lru_h[j].reshape(1, Bs, W) for j in range(n_lru)],
        [tmaj(state_lru_conv[j]) for j in range(n_lru)],
        (cache_kv_w128, cache_kv_w512, cache_kv_w2048),
        [tmaj(state_ffn_conv[l]) for l in range(depth)], P, (T, Bs))

    bmaj = lambda a, k: jnp.swapaxes(a.reshape(-1, Bs, a.shape[-1])[-k:], 0, 1)
    kv_p, kv_s = [], []
    for g, (win, dil) in enumerate(ATTN_GROUPS):
        lo = (g * 3 + 1) * ATTN_WIDTH
        keep = min(win, S)
        kv_p.append(jnp.stack([jnp.transpose(
            q.reshape(N_GROUPS, 3, GROUP_HEADS, B, S, HEAD_DIM)[g, 1:3, :, :, S - keep:, :], (2, 3, 0, 1, 4))
            for q in qkv_p], axis=0))
        kv_s.append(jnp.stack([jnp.swapaxes(q.reshape(T, Bs, -1), 0, 1)[:, :, lo:lo + 2 * ATTN_WIDTH]
                               .reshape(Bs, T, 2, GROUP_HEADS, HEAD_DIM) for q in qkv_s], axis=0))
    return (
        y_p.reshape(B, S, D),
        jnp.swapaxes(y_s.reshape(T, Bs, D), 0, 1),
        kv_p[0], kv_p[1], kv_p[2],
        jnp.stack([h[:, SUBLANES - 1] for h in h_p], axis=0),
        jnp.stack([c[:, SUBLANES - kc:] for c in lc_p], axis=0),
        jnp.stack([f[:, SUBLANES - kf:] for f in fc_p], axis=0),
        kv_s[0], kv_s[1], kv_s[2],
        jnp.stack([h[0] for h in h_s], axis=0),
        jnp.stack([bmaj(c[0], kc) for c in lc_s], axis=0),
        jnp.stack([bmaj(f[0], kf) for f in fc_s], axis=0),
    )
```

```python
import functools
import math
from typing import NamedTuple

import jax
import jax.numpy as jnp
from jax import lax
from jax.experimental import pallas as pl
from jax.experimental.pallas import tpu as pltpu

EPS = 1e-6
NEG = -1e30
LRU_C = 8.0
ATTN_GROUPS = ((128, 1), (512, 4), (2048, 16))
N_GROUPS = len(ATTN_GROUPS)
GROUP_HEADS = 8
HEAD_DIM = 128
ATTN_WIDTH = GROUP_HEADS * HEAD_DIM
ATTN_BLK = 128
SUBLANES = 8
LANES = 128
VMEM_LIMIT = 56 * 1024 * 1024

F32 = jnp.float32
BF16 = jnp.bfloat16


def _alibi_slopes():
    n = N_GROUPS * GROUP_HEADS
    return [[2.0 ** (-8.0 * (g * GROUP_HEADS + h + 1) / n) for h in range(GROUP_HEADS)] for g in range(N_GROUPS)]


def _gelu(x):
    c = math.sqrt(2.0 / math.pi)
    return x * (0.5 * (1.0 + jnp.tanh(c * (x + 0.044715 * (x * x * x)))))


def _rms(x, g):
    ms = jnp.mean(x * x, axis=-1, keepdims=True)
    return x * lax.rsqrt(ms + EPS) * g


def _pick(n, cap):
    best = None
    for t in range(LANES, min(n, cap) + 1, LANES):
        if n % t == 0:
            best = t
    assert best is not None, (n, cap)
    return best


def _params(n_axes):
    return pltpu.CompilerParams(dimension_semantics=("arbitrary",) * n_axes, vmem_limit_bytes=VMEM_LIMIT)


def _norm_matmul_kernel(x_ref, g_ref, w_ref, b_ref, o_ref, xn_ref):
    @pl.when(pl.program_id(1) == 0)
    def _():
        xn_ref[...] = _rms(x_ref[...], g_ref[...]).astype(BF16)

    acc = jnp.dot(xn_ref[...], w_ref[...], preferred_element_type=F32) + b_ref[...]
    if len(o_ref.shape) == 2:
        o_ref[...] = acc
    else:
        for c in range(o_ref.shape[0]):
            o_ref[c] = acc[:, c * LANES:(c + 1) * LANES]


def norm_matmul(x, gains, g_layer, w, bias, layer, tm, slab_out=False):
    M, D = x.shape
    N = w.shape[-1]
    tn = _pick(N, 1024 if tm > 256 else 2048)
    if slab_out:
        out_spec = pl.BlockSpec((tn // LANES, tm, LANES), lambda i, j: (j, i, 0))
        out_shape = jax.ShapeDtypeStruct((N // LANES, M, LANES), F32)
    else:
        out_spec = pl.BlockSpec((tm, tn), lambda i, j: (i, j))
        out_shape = jax.ShapeDtypeStruct((M, N), F32)
    return pl.pallas_call(
        _norm_matmul_kernel,
        grid=(M // tm, N // tn),
        in_specs=[
            pl.BlockSpec((tm, D), lambda i, j: (i, 0)),
            pl.BlockSpec((None, 1, D), lambda i, j: (g_layer, 0, 0)),
            pl.BlockSpec((None, D, tn), lambda i, j: (layer, 0, j)),
            pl.BlockSpec((None, 1, tn), lambda i, j: (layer, 0, j)),
        ],
        out_specs=out_spec,
        out_shape=out_shape,
        scratch_shapes=[pltpu.VMEM((tm, D), BF16)],
        compiler_params=_params(2),
        name="norm_matmul",
    )(x, gains, w, bias)


def _matmul_res_kernel(a_ref, w_ref, b_ref, r_ref, o_ref):
    if len(a_ref.shape) == 2:
        a = a_ref[...]
    else:
        a = jnp.concatenate([a_ref[c] for c in range(a_ref.shape[0])], axis=1)
    o_ref[...] = r_ref[...] + b_ref[...] + jnp.dot(a.astype(BF16), w_ref[...], preferred_element_type=F32)


def matmul_res(a, w, bias, res, layer, tm):
    M, N = res.shape
    K = w.shape[1]
    tn = _pick(N, 1024 if tm > 256 else 2048)
    if a.ndim == 2:
        a_spec = pl.BlockSpec((tm, K), lambda i, j: (i, 0))
    else:
        a_spec = pl.BlockSpec((K // LANES, tm, LANES), lambda i, j: (0, i, 0))
    return pl.pallas_call(
        _matmul_res_kernel,
        grid=(M // tm, N // tn),
        in_specs=[
            a_spec,
            pl.BlockSpec((None, K, tn), lambda i, j: (layer, 0, j)),
            pl.BlockSpec((None, 1, tn), lambda i, j: (layer, 0, j)),
            pl.BlockSpec((tm, tn), lambda i, j: (i, j)),
        ],
        out_specs=pl.BlockSpec((tm, tn), lambda i, j: (i, j)),
        out_shape=jax.ShapeDtypeStruct((M, N), F32),
        compiler_params=_params(2),
        name="matmul_res",
    )(a, w, bias, res)


def _norm_kernel(x_ref, g_ref, o_ref):
    o_ref[...] = _rms(x_ref[...], g_ref[...])


def final_norm(x, gain, tm):
    M, D = x.shape
    return pl.pallas_call(
        _norm_kernel,
        grid=(M // tm,),
        in_specs=[pl.BlockSpec((tm, D), lambda i: (i, 0)), pl.BlockSpec((1, D), lambda i: (0, 0))],
        out_specs=pl.BlockSpec((tm, D), lambda i: (i, 0)),
        out_shape=jax.ShapeDtypeStruct((M, D), F32),
        compiler_params=_params(1),
        name="final_norm",
    )(x, gain)


def _lru_kernel(gate_ref, u_ref, cw_ref, cb_ref, wa_ref, ba_ref, wi_ref, bi_ref, lam_ref, ci_ref, hi_ref,
                o_ref, cs_ref, hs_ref,
                e_ref, a_scr, b_scr, h_scr, cc_ref, hc_ref, *, R, cru, tps):
    i = pl.program_id(0)
    c = pl.program_id(1)
    tm, tc = u_ref.shape

    @pl.when(i % tps == 0)
    def _():
        cc_ref[c] = ci_ref[...]
        hc_ref[c] = hi_ref[...]

    u = u_ref[...]
    e_ref[0:cru, :] = cc_ref[c]
    e_ref[cru:cru + tm, :] = u
    cw = cw_ref[...]
    uc = (cb_ref[...] + cw[3:4] * u + cw[2:3] * e_ref[cru - R:cru - R + tm, :]
          + cw[1:2] * e_ref[cru - 2 * R:cru - 2 * R + tm, :] + cw[0:1] * e_ref[cru - 3 * R:cru - 3 * R + tm, :])
    tail = e_ref[tm:tm + cru, :]
    cc_ref[c] = tail
    cs_ref[c] = tail

    hd = wa_ref.shape[-1]
    ucb = uc.astype(BF16)
    ra, ri = [], []
    for hh in range(tc // hd):
        ub = ucb[:, hh * hd:(hh + 1) * hd]
        ra.append(jnp.dot(ub, wa_ref[hh], preferred_element_type=F32))
        ri.append(jnp.dot(ub, wi_ref[hh], preferred_element_type=F32))
    r = jax.nn.sigmoid(jnp.concatenate(ra, axis=1) + ba_ref[...])
    ig = jax.nn.sigmoid(jnp.concatenate(ri, axis=1) + bi_ref[...])
    nlam = -lam_ref[...]
    softplus = jnp.maximum(nlam, 0.0) + jnp.log1p(jnp.exp(-jnp.abs(nlam)))
    log_a = (-LRU_C) * r * softplus
    a = jnp.exp(log_a)
    th = jnp.tanh(log_a)
    bx = jnp.sqrt((-2.0 * th) / (1.0 - th)) * (ig * uc)

    if R == 1:
        A = a.reshape(tm // SUBLANES, SUBLANES, tc)
        B = bx.reshape(tm // SUBLANES, SUBLANES, tc)
        row = lax.broadcasted_iota(jnp.int32, A.shape, 1)
        s = 1
        while s < SUBLANES:
            m = row >= s
            B = jnp.where(m, A * pltpu.roll(B, s, axis=1) + B, B)
            A = jnp.where(m, A * pltpu.roll(A, s, axis=1), A)
            s *= 2
        a_scr[...] = A.reshape(tm, tc)
        b_scr[...] = B.reshape(tm, tc)
        h0 = hc_ref[c][SUBLANES - 1:SUBLANES, :]
    else:
        assert R == SUBLANES
        a_scr[...] = a
        b_scr[...] = bx
        h0 = hc_ref[c]

    def body(g, h):
        r0 = pl.multiple_of(g * SUBLANES, SUBLANES)
        hg = b_scr[pl.ds(r0, SUBLANES), :] + a_scr[pl.ds(r0, SUBLANES), :] * h
        h_scr[pl.ds(r0, SUBLANES), :] = hg
        return hg[SUBLANES - 1:SUBLANES, :] if R == 1 else hg

    n_groups = tm // SUBLANES
    lax.fori_loop(0, n_groups, body, h0, unroll=min(n_groups, 8))
    h_tail = h_scr[tm - SUBLANES:tm, :]
    hc_ref[c] = h_tail
    hs_ref[c] = h_tail
    o_ref[...] = (h_scr[...] * _gelu(gate_ref[...])).astype(BF16)


def lru_core(proj, conv_w, conv_b, w_a, b_a, w_i, b_i, lam, conv_init, h_init, layer, tm, R):
    M, W2 = proj.shape
    W = W2 // 2
    hd = w_a.shape[-1]
    tc = max(hd, _pick(W, 512))
    assert tc % hd == 0 and W % tc == 0
    nc = W // tc
    nseq, cru, _ = conv_init.shape
    tps = (M // tm) // nseq
    kern = functools.partial(_lru_kernel, R=R, cru=cru, tps=tps)
    row_vec = lambda: pl.BlockSpec((None, 1, tc), lambda i, c: (layer, 0, c))
    gate_w = lambda: pl.BlockSpec((None, tc // hd, hd, hd), lambda i, c: (layer, c, 0, 0))
    unfold = lambda s: jnp.swapaxes(s, 1, 2).reshape(nseq, s.shape[2], W)
    hg, c_rows, h_rows = pl.pallas_call(
        kern,
        grid=(M // tm, nc),
        in_specs=[
            pl.BlockSpec((tm, tc), lambda i, c: (i, c)),
            pl.BlockSpec((tm, tc), lambda i, c: (i, nc + c)),
            pl.BlockSpec((None, conv_w.shape[1], tc), lambda i, c: (layer, 0, c)),
            row_vec(), gate_w(), row_vec(), gate_w(), row_vec(), row_vec(),
            pl.BlockSpec((None, cru, tc), lambda i, c: (i // tps, 0, c)),
            pl.BlockSpec((None, SUBLANES, tc), lambda i, c: (i // tps, 0, c)),
        ],
        out_specs=[
            pl.BlockSpec((tm, tc), lambda i, c: (i, c)),
            pl.BlockSpec((None, nc, cru, tc), lambda i, c: (i // tps, 0, 0, 0)),
            pl.BlockSpec((None, nc, SUBLANES, tc), lambda i, c: (i // tps, 0, 0, 0)),
        ],
        out_shape=[
            jax.ShapeDtypeStruct((M, W), BF16),
            jax.ShapeDtypeStruct((nseq, nc, cru, tc), F32),
            jax.ShapeDtypeStruct((nseq, nc, SUBLANES, tc), F32),
        ],
        scratch_shapes=[
            pltpu.VMEM((cru + tm, tc), F32),
            pltpu.VMEM((tm, tc), F32),
            pltpu.VMEM((tm, tc), F32),
            pltpu.VMEM((tm, tc), F32),
            pltpu.VMEM((nc, cru, tc), F32),
            pltpu.VMEM((nc, SUBLANES, tc), F32),
        ],
        compiler_params=_params(2),
        name="lru_core",
    )(proj, proj, conv_w, conv_b, w_a, b_a, w_i, b_i, lam, conv_init, h_init)
    return hg, unfold(c_rows), unfold(h_rows)


def _ffn_kernel(x_ref, g_ref, wg_ref, wv_ref, cwg_ref, cwv_ref, cbg_ref, cbv_ref, wd_ref, ig_ref, iv_ref,
                o_ref, sg_ref, sv_ref,
                xn_ref, eg_ref, ev_ref, cg_ref, cv_ref, *, R, cr, tps):
    i = pl.program_id(0)
    j = pl.program_id(1)
    tm = x_ref.shape[0]

    @pl.when(j == 0)
    def _():
        x = x_ref[...]
        xn_ref[...] = _rms(x, g_ref[...]).astype(BF16)
        o_ref[...] = x

    @pl.when(i % tps == 0)
    def _():
        cg_ref[j] = ig_ref[...]
        cv_ref[j] = iv_ref[...]

    def side(w_ref, cw_ref, cb_ref, e_ref, c_ref, s_ref):
        up = jnp.dot(xn_ref[...], w_ref[...], preferred_element_type=F32)
        e_ref[0:cr, :] = c_ref[j]
        e_ref[cr:cr + tm, :] = up
        cw = cw_ref[...]
        conv = (cb_ref[...] + cw[2:3] * up + cw[1:2] * e_ref[cr - R:cr - R + tm, :]
                + cw[0:1] * e_ref[cr - 2 * R:cr - 2 * R + tm, :])
        tail = e_ref[tm:tm + cr, :]
        c_ref[j] = tail
        s_ref[j] = tail
        return conv

    cg = side(wg_ref, cwg_ref, cbg_ref, eg_ref, cg_ref, sg_ref)
    cv = side(wv_ref, cwv_ref, cbv_ref, ev_ref, cv_ref, sv_ref)
    act = (_gelu(cg) * cv).astype(BF16)
    o_ref[...] += jnp.dot(act, wd_ref[...], preferred_element_type=F32)


def conv_ffn(x, gains, w_up, conv_w, conv_b, w_down, init, layer, tm, tf, R):
    M, D = x.shape
    F = w_down.shape[1]
    nf = F // tf
    nseq, cr, _ = init.shape
    tps = (M // tm) // nseq
    kern = functools.partial(_ffn_kernel, R=R, cr=cr, tps=tps)
    K = conv_w.shape[1]
    halves = lambda mk: [mk(0), mk(nf)]
    state_spec = lambda off: pl.BlockSpec((None, cr, tf), lambda i, j: (i // tps, 0, off + j))
    once = pl.Buffered(1)
    out, sg, sv = pl.pallas_call(
        kern,
        grid=(M // tm, nf),
        in_specs=[
            pl.BlockSpec((tm, D), lambda i, j: (i, 0)),
            pl.BlockSpec((None, 1, D), lambda i, j: (layer, 0, 0)),
            *halves(lambda off: pl.BlockSpec((None, D, tf), lambda i, j: (layer, 0, off + j))),
            *halves(lambda off: pl.BlockSpec((None, K, tf), lambda i, j: (layer, 0, off + j))),
            *halves(lambda off: pl.BlockSpec((None, 1, tf), lambda i, j: (layer, 0, off + j))),
            pl.BlockSpec((None, tf, D), lambda i, j: (layer, j, 0)),
            *halves(state_spec),
        ],
        out_specs=[
            pl.BlockSpec((tm, D), lambda i, j: (i, 0), pipeline_mode=once),
            pl.BlockSpec((None, nf, cr, tf), lambda i, j: (i // tps, 0, 0, 0)),
            pl.BlockSpec((None, nf, cr, tf), lambda i, j: (i // tps, 0, 0, 0)),
        ],
        out_shape=[
            jax.ShapeDtypeStruct((M, D), F32),
            jax.ShapeDtypeStruct((nseq, nf, cr, tf), F32),
            jax.ShapeDtypeStruct((nseq, nf, cr, tf), F32),
        ],
        scratch_shapes=[
            pltpu.VMEM((tm, D), BF16),
            pltpu.VMEM((cr + tm, tf), F32),
            pltpu.VMEM((cr + tm, tf), F32),
            pltpu.VMEM((nf, cr, tf), F32),
            pltpu.VMEM((nf, cr, tf), F32),
        ],
        compiler_params=_params(2),
        name="conv_ffn",
    )(x, gains, w_up, w_up, conv_w, conv_w, conv_b, conv_b, w_down, init, init)
    unfold = lambda s: jnp.swapaxes(s, 1, 2).reshape(nseq, cr, F)
    return out, jnp.concatenate([unfold(sg), unfold(sv)], axis=-1)


ATTN_SUPER = max(w for w, _ in ATTN_GROUPS)
ATTN_BATCH = 8


def _attn_prompt_kernel(sl_ref, q0, k0, v0, q1, k1, v1, q2, k2, v2, o_ref,
                        ke0, ve0, ke1, ve1, ke2, ve2, o_scr, l_scr):
    n = pl.program_id(2)
    qs, ks, vs = (q0, q1, q2), (k0, k1, k2), (v0, v1, v2)
    kes, ves = (ke0, ke1, ke2), (ve0, ve1, ve2)
    SB = o_ref.shape[0]
    blk = ATTN_BLK
    row = lax.broadcasted_iota(jnp.int32, (blk, 2 * blk), 0)
    col = lax.broadcasted_iota(jnp.int32, (blk, 2 * blk), 1)
    steps = row + blk - col
    band = jnp.logical_and(steps >= 0, steps <= blk)
    band_cur = jnp.logical_and(band, col >= blk)
    steps_f = steps.astype(F32)
    ones_v = jnp.ones((2 * blk, HEAD_DIM), BF16)
    inv_sqrt = 1.0 / math.sqrt(HEAD_DIM)
    nt = (((1,), (1,)), ((), ()))

    def rows_of(start, dil):
        return pl.ds(start, blk, stride=dil) if dil > 1 else pl.ds(start, blk)

    for g, (win, dil) in enumerate(ATTN_GROUPS):
        q_ref, ke, ve = qs[g], kes[g], ves[g]

        @pl.when(n == 0)
        def _():
            ke[0:win, :] = jnp.zeros((win, HEAD_DIM), F32)
            ve[0:win, :] = jnp.zeros((win, HEAD_DIM), F32)

        @pl.when(n > 0)
        def _():
            ke[0:win, :] = ke[SB:SB + win, :]
            ve[0:win, :] = ve[SB:SB + win, :]

        ke[win:win + SB, :] = ks[g][...]
        ve[win:win + SB, :] = vs[g][...]
        alibi = (sl_ref[g:g + 1, :] * float(-dil)) * steps_f
        bias_full = jnp.where(band, alibi, NEG)
        bias_first = jnp.where(band_cur, alibi, NEG)

        def batch(it, carry):
            infos = []
            for u in range(ATTN_BATCH):
                bidx = it * ATTN_BATCH + u
                j = bidx // dil
                r = bidx % dil
                p0 = j * win + r
                if dil == 1:
                    p0 = pl.multiple_of(p0, blk)
                infos.append((p0, jnp.logical_or(n > 0, j > 0)))
            scores = []
            for p0, _ in infos:
                q = q_ref[rows_of(p0, dil), :].astype(BF16)
                kcat = jnp.concatenate([ke[rows_of(p0, dil), :], ke[rows_of(p0 + win, dil), :]], axis=0).astype(BF16)
                scores.append(lax.dot_general(q, kcat, nt, preferred_element_type=F32))
            probs = []
            for (p0, prev_ok), s in zip(infos, scores):
                s = s * inv_sqrt + jnp.where(prev_ok, bias_full, bias_first)
                m = jnp.max(jnp.maximum(s[:, :blk], s[:, blk:]), axis=-1, keepdims=True)
                probs.append((jnp.exp(s - m).astype(BF16), m))
            for (p0, _), (e, m) in zip(infos, probs):
                vcat = jnp.concatenate([ve[rows_of(p0, dil), :], ve[rows_of(p0 + win, dil), :]], axis=0).astype(BF16)
                res = jnp.dot(e, jnp.concatenate([vcat, ones_v], axis=1), preferred_element_type=F32)
                den = res[:, HEAD_DIM:]
                o_scr[g, rows_of(p0, dil), :] = res[:, :HEAD_DIM] / den
                l_scr[g, rows_of(p0, dil), :] = m + jnp.log(den)
            return carry

        lax.fori_loop(0, SB // (blk * ATTN_BATCH), batch, 0)

    chunk = 2 * blk
    for c0 in range(0, SB, chunk):
        ls = [l_scr[g, c0:c0 + chunk, :] for g in range(N_GROUPS)]
        mm = jnp.maximum(jnp.maximum(ls[0], ls[1]), ls[2])
        es = [jnp.exp(l - mm) for l in ls]
        acc = es[0] * o_scr[0, c0:c0 + chunk, :]
        for g in range(1, N_GROUPS):
            acc = acc + es[g] * o_scr[g, c0:c0 + chunk, :]
        o_ref[c0:c0 + chunk, :] = (acc / (es[0] + es[1] + es[2])).astype(BF16)


def attn_prompt(qkv_slabs, B, S):
    n_slabs, M, _ = qkv_slabs.shape
    SB = ATTN_SUPER
    assert n_slabs == 3 * N_GROUPS * GROUP_HEADS and S % SB == 0 and M == B * S
    assert all(w // d == ATTN_BLK for w, d in ATTN_GROUPS) and (SB // ATTN_BLK) % ATTN_BATCH == 0
    nsb = S // SB
    slopes = jnp.asarray(_alibi_slopes(), F32).T
    slopes = jnp.broadcast_to(slopes[:, :, None], (GROUP_HEADS, N_GROUPS, 2 * ATTN_BLK))

    def slab(g, comp):
        base = (g * 3 + comp) * GROUP_HEADS
        return pl.BlockSpec((None, SB, HEAD_DIM), lambda b, h, n: (base + h, b * nsb + n, 0))

    ext = [pltpu.VMEM((w + SB, HEAD_DIM), F32) for w, _ in ATTN_GROUPS for _kv in range(2)]
    return pl.pallas_call(
        _attn_prompt_kernel,
        grid=(B, GROUP_HEADS, nsb),
        in_specs=[pl.BlockSpec((None, N_GROUPS, 2 * ATTN_BLK), lambda b, h, n: (h, 0, 0))]
        + [slab(g, comp) for g in range(N_GROUPS) for comp in range(3)],
        out_specs=pl.BlockSpec((None, SB, HEAD_DIM), lambda b, h, n: (h, b * nsb + n, 0)),
        out_shape=jax.ShapeDtypeStruct((GROUP_HEADS, M, HEAD_DIM), BF16),
        scratch_shapes=ext + [pltpu.VMEM((N_GROUPS, SB, HEAD_DIM), F32), pltpu.VMEM((N_GROUPS, SB, HEAD_DIM), F32)],
        compiler_params=_params(3),
        name="attn_prompt",
    )(slopes, *([qkv_slabs] * (3 * N_GROUPS)))


def _attn_sample_kernel(sl_ref, qkv_ref, c0_ref, c1_ref, c2_ref, o_ref):
    T = qkv_ref.shape[0]
    H = GROUP_HEADS
    caches = (c0_ref, c1_ref, c2_ref)
    rows = c0_ref.shape[0]
    l_idx = lax.broadcasted_iota(jnp.int32, (rows, H, 1), 0)
    l_f = l_idx.astype(F32)
    inv_sqrt = 1.0 / math.sqrt(HEAD_DIM)
    for t in range(T):
        outs, lses = [], []
        for g, (win, dil) in enumerate(ATTN_GROUPS):
            base = g * 3 * H
            slope = sl_ref[g][:, 0:1]
            q = qkv_ref[t, base:base + H, :]
            ph = 0 if dil == 1 else t
            kc = caches[g][:, ph * 2 * H:ph * 2 * H + H, :]
            vc = caches[g][:, ph * 2 * H + H:(ph + 1) * 2 * H, :]
            sc = jnp.sum(kc * q[None], axis=-1, keepdims=True) * inv_sqrt
            if dil == 1:
                sc = jnp.where(l_idx >= t, sc - slope[None] * (float(rows + t) - l_f), NEG)
                new_ts = list(range(t + 1))
            else:
                sc = sc - (slope[None] * float(dil)) * (float(rows) - l_f)
                new_ts = [t]
            m = jnp.max(sc, axis=0)
            s_new = []
            for t2 in new_ts:
                k2 = qkv_ref[t2, base + H:base + 2 * H, :]
                s2 = jnp.sum(q * k2, axis=-1, keepdims=True) * inv_sqrt - slope * float((t - t2) * dil)
                s_new.append(s2)
                m = jnp.maximum(m, s2)
            ec = jnp.exp(sc - m[None])
            den = jnp.sum(ec, axis=0)
            acc = jnp.sum(ec * vc, axis=0)
            for t2, s2 in zip(new_ts, s_new):
                v2 = qkv_ref[t2, base + 2 * H:base + 3 * H, :]
                e2 = jnp.exp(s2 - m)
                den = den + e2
                acc = acc + e2 * v2
            outs.append(acc / den)
            lses.append(m + jnp.log(den))
        mm = jnp.maximum(jnp.maximum(lses[0], lses[1]), lses[2])
        es = [jnp.exp(l - mm) for l in lses]
        o_ref[t] = (es[0] * outs[0] + es[1] * outs[1] + es[2] * outs[2]) / (es[0] + es[1] + es[2])


def attn_sample(qkv, caches, layer, T, Bs):
    H = GROUP_HEADS
    n_rows = qkv.shape[-1] // HEAD_DIM
    views, specs = [], []
    for (win, dil), c in zip(ATTN_GROUPS, caches):
        assert c.shape[1] == Bs and c.shape[2] == win and win // dil == ATTN_BLK and (dil == 1 or T <= dil)
        phases = min(dil, -(-T // 4) * 4)
        views.append(c.reshape(c.shape[0], Bs, win // dil, dil * 2 * H, HEAD_DIM))
        specs.append(pl.BlockSpec((None, None, win // dil, phases * 2 * H, HEAD_DIM), lambda b: (layer, b, 0, 0, 0)))
    slopes = jnp.broadcast_to(jnp.asarray(_alibi_slopes(), F32)[:, :, None], (N_GROUPS, H, HEAD_DIM))
    q_rows = jnp.swapaxes(qkv.reshape(T, Bs, n_rows, HEAD_DIM), 0, 1)
    out = pl.pallas_call(
        _attn_sample_kernel,
        grid=(Bs,),
        in_specs=[pl.BlockSpec((N_GROUPS, H, HEAD_DIM), lambda b: (0, 0, 0)),
                  pl.BlockSpec((None, T, n_rows, HEAD_DIM), lambda b: (b, 0, 0, 0)), *specs],
        out_specs=pl.BlockSpec((None, T, H, HEAD_DIM), lambda b: (b, 0, 0, 0)),
        out_shape=jax.ShapeDtypeStruct((Bs, T, H, HEAD_DIM), F32),
        compiler_params=_params(1),
        name="attn_sample",
    )(slopes, q_rows, *views)
    return jnp.swapaxes(out, 0, 1).reshape(T * Bs, ATTN_WIDTH)


class Tiles(NamedTuple):
    seq: int
    mm: int
    ffn: int
    tf: int


def _run_trunk(x, nseq, R, tiles, lru_h, lru_conv, kv_caches, ffn_conv, P, dims):
    M, D = x.shape
    depth = P["norm_mix"].shape[0]
    new_h, new_lconv, new_fconv, qkvs = [], [], [], []
    for layer in range(depth):
        j = layer // 2
        if layer % 2 == 0:
            proj = norm_matmul(x, P["norm_mix"], layer, P["lru_w_in"], P["lru_b_in"], layer=j, tm=tiles.mm)
            hg, c_rows, h_rows = lru_core(proj, P["lru_conv_w"], P["lru_conv_b"], P["lru_w_a"], P["lru_b_a"],
                                          P["lru_w_i"], P["lru_b_i"], P["lru_lambda"], lru_conv[j], lru_h[j],
                                          layer=j, tm=tiles.seq, R=R)
            new_h.append(h_rows)
            new_lconv.append(c_rows)
            x = matmul_res(hg, P["lru_w_out"], P["lru_b_out"], x, layer=j, tm=tiles.mm)
        else:
            qkv = norm_matmul(x, P["norm_mix"], layer, P["attn_w_qkv"], P["attn_b_zero"], layer=j, tm=tiles.mm,
                              slab_out=kv_caches is None)
            qkvs.append(qkv)
            if kv_caches is None:
                o = attn_prompt(qkv, *dims)
            else:
                o = attn_sample(qkv, kv_caches, layer=j, T=dims[0], Bs=dims[1])
            x = matmul_res(o, P["attn_w_o"], P["attn_bo_zero"], x, layer=j, tm=tiles.mm)
        x, f_rows = conv_ffn(x, P["norm_ffn"], P["ffn_w_up"], P["ffn_conv_w"], P["ffn_conv_b"], P["ffn_w_down"],
                             ffn_conv[layer], layer=layer, tm=tiles.ffn, tf=tiles.tf, R=R)
        new_fconv.append(f_rows)
    y = final_norm(x, P["norm_final"], tm=tiles.seq)
    return y, new_h, new_lconv, qkvs, new_fconv


def kernel(x_prompt, x_sample, cache_kv_w128, cache_kv_w512, cache_kv_w2048, state_lru_h, state_lru_conv, state_ffn_conv, norm_mix, norm_ffn, norm_final, lru_w_in, lru_b_in, lru_conv_w, lru_conv_b, lru_w_a, lru_b_a, lru_w_i, lru_b_i, lru_lambda, lru_w_out, lru_b_out, attn_w_qkv, attn_w_o, ffn_w_up, ffn_conv_w, ffn_conv_b, ffn_w_down):
    B, S, D = x_prompt.shape
    Bs, T, _ = x_sample.shape
    depth = norm_mix.shape[0]
    n_lru, W = lru_lambda.shape
    n_attn = attn_w_qkv.shape[0]
    F2 = ffn_w_up.shape[-1]
    assert Bs == SUBLANES, "the sample group is laid out time-major with one sublane per sequence"
    row3 = lambda a: a.reshape(a.shape[0], 1, a.shape[-1])
    P = {
        "norm_mix": row3(norm_mix), "norm_ffn": row3(norm_ffn), "norm_final": norm_final.reshape(1, D),
        "lru_w_in": lru_w_in.astype(BF16), "lru_b_in": row3(lru_b_in),
        "lru_conv_w": lru_conv_w, "lru_conv_b": row3(lru_conv_b),
        "lru_w_a": lru_w_a.astype(BF16), "lru_b_a": row3(lru_b_a),
        "lru_w_i": lru_w_i.astype(BF16), "lru_b_i": row3(lru_b_i),
        "lru_lambda": row3(lru_lambda),
        "lru_w_out": lru_w_out.astype(BF16), "lru_b_out": row3(lru_b_out),
        "attn_w_qkv": attn_w_qkv.astype(BF16), "attn_w_o": attn_w_o.astype(BF16),
        "attn_b_zero": jnp.zeros((n_attn, 1, attn_w_qkv.shape[-1]), F32),
        "attn_bo_zero": jnp.zeros((n_attn, 1, D), F32),
        "ffn_w_up": ffn_w_up.astype(BF16), "ffn_conv_w": ffn_conv_w, "ffn_conv_b": row3(ffn_conv_b),
        "ffn_w_down": ffn_w_down.astype(BF16),
    }
    kc = lru_conv_w.shape[1] - 1
    kf = ffn_conv_w.shape[1] - 1

    row_tile = lambda cap: cap if S % cap == 0 else S
    tiles_p = Tiles(seq=row_tile(512), mm=row_tile(1024), ffn=row_tile(1024), tf=_pick(F2 // 2, 512))
    zeros = lambda n, w: [jnp.zeros((B, SUBLANES, w), F32)] * n
    y_p, h_p, lc_p, qkv_p, fc_p = _run_trunk(
        x_prompt.reshape(B * S, D), B, 1, tiles_p, zeros(n_lru, W), zeros(n_lru, W), None, zeros(depth, F2), P, (B, S))

    tmaj = lambda a: jnp.swapaxes(a, 0, 1).reshape(1, a.shape[1] * a.shape[0], a.shape[-1])
    y_s, h_s, lc_s, qkv_s, fc_s = _run_trunk(
        jnp.swapaxes(x_sample, 0, 1).reshape(T * Bs, D), 1, Bs,
        Tiles(seq=T * Bs, mm=T * Bs, ffn=T * Bs, tf=_pick(F2 // 2, 1024)),
        [state_lru_h[j].reshape(1, Bs, W) for j in range(n_lru)],
        [tmaj(state_lru_conv[j]) for j in range(n_lru)],
        (cache_kv_w128, cache_kv_w512, cache_kv_w2048),
        [tmaj(state_ffn_conv[l]) for l in range(depth)], P, (T, Bs))

    bmaj = lambda a, k: jnp.swapaxes(a.reshape(-1, Bs, a.shape[-1])[-k:], 0, 1)
    kv_p, kv_s = [], []
    for g, (win, dil) in enumerate(ATTN_GROUPS):
        lo = (g * 3 + 1) * ATTN_WIDTH
        keep = min(win, S)
        kv_p.append(jnp.stack([jnp.transpose(
            q.reshape(N_GROUPS, 3, GROUP_HEADS, B, S, HEAD_DIM)[g, 1:3, :, :, S - keep:, :], (2, 3, 0, 1, 4))
            for q in qkv_p], axis=0))
        kv_s.append(jnp.stack([jnp.swapaxes(q.reshape(T, Bs, -1), 0, 1)[:, :, lo:lo + 2 * ATTN_WIDTH]
                               .reshape(Bs, T, 2, GROUP_HEADS, HEAD_DIM) for q in qkv_s], axis=0))
    return (
        y_p.reshape(B, S, D),
        jnp.swapaxes(y_s.reshape(T, Bs, D), 0, 1),
        kv_p[0], kv_p[1], kv_p[2],
        jnp.stack([h[:, SUBLANES - 1] for h in h_p], axis=0),
        jnp.stack([c[:, SUBLANES - kc:] for c in lc_p], axis=0),
        jnp.stack([f[:, SUBLANES - kf:] for f in fc_p], axis=0),
        kv_s[0], kv_s[1], kv_s[2],
        jnp.stack([h[0] for h in h_s], axis=0),
        jnp.stack([bmaj(c[0], kc) for c in lc_s], axis=0),
        jnp.stack([bmaj(f[0], kf) for f in fc_s], axis=0),
    )
```

```python
import functools
import math
from typing import NamedTuple

import jax
import jax.numpy as jnp
from jax import lax
from jax.experimental import pallas as pl
from jax.experimental.pallas import tpu as pltpu

EPS = 1e-6
NEG = -1e30
LRU_C = 8.0
ATTN_GROUPS = ((128, 1), (512, 4), (2048, 16))
N_GROUPS = len(ATTN_GROUPS)
GROUP_HEADS = 8
HEAD_DIM = 128
ATTN_WIDTH = GROUP_HEADS * HEAD_DIM
ATTN_BLK = 128
SUBLANES = 8
LANES = 128
VMEM_LIMIT = 56 * 1024 * 1024

F32 = jnp.float32
F32_TINY = 1.1754944e-38
BF16 = jnp.bfloat16


def _alibi_slopes():
    n = N_GROUPS * GROUP_HEADS
    return [[2.0 ** (-8.0 * (g * GROUP_HEADS + h + 1) / n) for h in range(GROUP_HEADS)] for g in range(N_GROUPS)]


def _gelu(x):
    c = math.sqrt(2.0 / math.pi)
    return x * (0.5 * (1.0 + jnp.tanh(c * (x + 0.044715 * (x * x * x)))))


def _rms(x, g):
    ms = jnp.mean(x * x, axis=-1, keepdims=True)
    return x * lax.rsqrt(ms + EPS) * g


def _pick(n, cap):
    best = None
    for t in range(LANES, min(n, cap) + 1, LANES):
        if n % t == 0:
            best = t
    assert best is not None, (n, cap)
    return best


def _params(n_axes):
    return pltpu.CompilerParams(dimension_semantics=("arbitrary",) * n_axes, vmem_limit_bytes=VMEM_LIMIT)


def _norm_matmul_kernel(x_ref, g_ref, w_ref, b_ref, o_ref, xn_ref):
    @pl.when(pl.program_id(1) == 0)
    def _():
        xn_ref[...] = _rms(x_ref[...], g_ref[...]).astype(BF16)

    acc = jnp.dot(xn_ref[...], w_ref[...], preferred_element_type=F32) + b_ref[...]
    if len(o_ref.shape) == 2:
        o_ref[...] = acc
    else:
        for c in range(o_ref.shape[0]):
            o_ref[c] = acc[:, c * LANES:(c + 1) * LANES]


def norm_matmul(x, gains, g_layer, w, bias, layer, tm, slab_out=False):
    M, D = x.shape
    N = w.shape[-1]
    tn = _pick(N, 1024 if tm > 256 else 2048)
    if slab_out:
        out_spec = pl.BlockSpec((tn // LANES, tm, LANES), lambda i, j: (j, i, 0))
        out_shape = jax.ShapeDtypeStruct((N // LANES, M, LANES), F32)
    else:
        out_spec = pl.BlockSpec((tm, tn), lambda i, j: (i, j))
        out_shape = jax.ShapeDtypeStruct((M, N), F32)
    return pl.pallas_call(
        _norm_matmul_kernel,
        grid=(M // tm, N // tn),
        in_specs=[
            pl.BlockSpec((tm, D), lambda i, j: (i, 0)),
            pl.BlockSpec((None, 1, D), lambda i, j: (g_layer, 0, 0)),
            pl.BlockSpec((None, D, tn), lambda i, j: (layer, 0, j)),
            pl.BlockSpec((None, 1, tn), lambda i, j: (layer, 0, j)),
        ],
        out_specs=out_spec,
        out_shape=out_shape,
        scratch_shapes=[pltpu.VMEM((tm, D), BF16)],
        compiler_params=_params(2),
        name="norm_matmul",
    )(x, gains, w, bias)


def _matmul_res_kernel(a_ref, w_ref, b_ref, r_ref, o_ref):
    if len(a_ref.shape) == 2:
        a = a_ref[...]
    else:
        a = jnp.concatenate([a_ref[c] for c in range(a_ref.shape[0])], axis=1)
    o_ref[...] = r_ref[...] + b_ref[...] + jnp.dot(a.astype(BF16), w_ref[...], preferred_element_type=F32)


def matmul_res(a, w, bias, res, layer, tm):
    M, N = res.shape
    K = w.shape[1]
    tn = _pick(N, 1024 if tm > 256 else 2048)
    if a.ndim == 2:
        a_spec = pl.BlockSpec((tm, K), lambda i, j: (i, 0))
    else:
        a_spec = pl.BlockSpec((K // LANES, tm, LANES), lambda i, j: (0, i, 0))
    return pl.pallas_call(
        _matmul_res_kernel,
        grid=(M // tm, N // tn),
        in_specs=[
            a_spec,
            pl.BlockSpec((None, K, tn), lambda i, j: (layer, 0, j)),
            pl.BlockSpec((None, 1, tn), lambda i, j: (layer, 0, j)),
            pl.BlockSpec((tm, tn), lambda i, j: (i, j)),
        ],
        out_specs=pl.BlockSpec((tm, tn), lambda i, j: (i, j)),
        out_shape=jax.ShapeDtypeStruct((M, N), F32),
        compiler_params=_params(2),
        name="matmul_res",
    )(a, w, bias, res)


def _lru_kernel(x_ref, g_ref, wg_ref, wu_ref, bg_ref, bu_ref, cw_ref, cb_ref, wa_ref, ba_ref, wi_ref, bi_ref, lam_ref,
                wo_ref, bo_ref, ci_ref, hi_ref,
                o_ref, cs_ref, hs_ref,
                xn_ref, e_ref, a_scr, b_scr, h_scr, cc_ref, hc_ref, *, R, cru, tps):
    i = pl.program_id(0)
    c = pl.program_id(1)
    tm = x_ref.shape[0]
    tc = wg_ref.shape[-1]

    @pl.when(c == 0)
    def _():
        x = x_ref[...]
        xn_ref[...] = _rms(x, g_ref[...]).astype(BF16)
        o_ref[...] = x + bo_ref[...]

    @pl.when(i % tps == 0)
    def _():
        cc_ref[c] = ci_ref[...]
        hc_ref[c] = hi_ref[...]

    xn = xn_ref[...]
    gate = jnp.dot(xn, wg_ref[...], preferred_element_type=F32) + bg_ref[...]
    u = jnp.dot(xn, wu_ref[...], preferred_element_type=F32) + bu_ref[...]
    e_ref[0:cru, :] = cc_ref[c]
    e_ref[cru:cru + tm, :] = u
    cw = cw_ref[...]
    uc = (cb_ref[...] + cw[3:4] * u + cw[2:3] * e_ref[cru - R:cru - R + tm, :]
          + cw[1:2] * e_ref[cru - 2 * R:cru - 2 * R + tm, :] + cw[0:1] * e_ref[cru - 3 * R:cru - 3 * R + tm, :])
    tail = e_ref[tm:tm + cru, :]
    cc_ref[c] = tail
    cs_ref[c] = tail

    hd = wa_ref.shape[-1]
    ucb = uc.astype(BF16)
    ra, ri = [], []
    for hh in range(tc // hd):
        ub = ucb[:, hh * hd:(hh + 1) * hd]
        ra.append(jnp.dot(ub, wa_ref[hh], preferred_element_type=F32))
        ri.append(jnp.dot(ub, wi_ref[hh], preferred_element_type=F32))
    r = jax.nn.sigmoid(jnp.concatenate(ra, axis=1) + ba_ref[...])
    ig = jax.nn.sigmoid(jnp.concatenate(ri, axis=1) + bi_ref[...])
    nlam = -lam_ref[...]
    softplus = jnp.maximum(nlam, 0.0) + jnp.log1p(jnp.exp(-jnp.abs(nlam)))
    log_a = (-LRU_C) * r * softplus
    a = jnp.exp(log_a)
    th = jnp.tanh(log_a)
    z = (-2.0 * th) / (1.0 - th)
    bx = (z * lax.rsqrt(jnp.maximum(z, F32_TINY))) * (ig * uc)

    if R == 1:
        A = a.reshape(tm // SUBLANES, SUBLANES, tc)
        B = bx.reshape(tm // SUBLANES, SUBLANES, tc)
        row = lax.broadcasted_iota(jnp.int32, A.shape, 1)
        s = 1
        while s < SUBLANES:
            m = row >= s
            B = jnp.where(m, A * pltpu.roll(B, s, axis=1) + B, B)
            A = jnp.where(m, A * pltpu.roll(A, s, axis=1), A)
            s *= 2
        a_scr[...] = A.reshape(tm, tc)
        b_scr[...] = B.reshape(tm, tc)
        h0 = hc_ref[c][SUBLANES - 1:SUBLANES, :]
    else:
        assert R == SUBLANES
        a_scr[...] = a
        b_scr[...] = bx
        h0 = hc_ref[c]

    def body(g, h):
        r0 = pl.multiple_of(g * SUBLANES, SUBLANES)
        hg = b_scr[pl.ds(r0, SUBLANES), :] + a_scr[pl.ds(r0, SUBLANES), :] * h
        h_scr[pl.ds(r0, SUBLANES), :] = hg
        return hg[SUBLANES - 1:SUBLANES, :] if R == 1 else hg

    n_groups = tm // SUBLANES
    lax.fori_loop(0, n_groups, body, h0, unroll=min(n_groups, 8))
    h_tail = h_scr[tm - SUBLANES:tm, :]
    hc_ref[c] = h_tail
    hs_ref[c] = h_tail
    hg = (h_scr[...] * _gelu(gate)).astype(BF16)
    o_ref[...] += jnp.dot(hg, wo_ref[...], preferred_element_type=F32)


def lru_block(x, gains, g_layer, w_in, b_in, conv_w, conv_b, w_a, b_a, w_i, b_i, lam, w_out, b_out,
              conv_init, h_init, layer, tm, R):
    M, D = x.shape
    W = w_out.shape[1]
    hd = w_a.shape[-1]
    tc = hd
    assert W % tc == 0 and tc % LANES == 0
    nc = W // tc
    nseq, cru, _ = conv_init.shape
    tps = (M // tm) // nseq
    kern = functools.partial(_lru_kernel, R=R, cru=cru, tps=tps)
    row_vec = lambda off=0: pl.BlockSpec((None, 1, tc), lambda i, c: (layer, 0, off + c))
    gate_w = lambda: pl.BlockSpec((None, tc // hd, hd, hd), lambda i, c: (layer, c, 0, 0))
    in_w = lambda off: pl.BlockSpec((None, D, tc), lambda i, c: (layer, 0, off + c))
    unfold = lambda s: jnp.swapaxes(s, 1, 2).reshape(nseq, s.shape[2], W)
    out, c_rows, h_rows = pl.pallas_call(
        kern,
        grid=(M // tm, nc),
        in_specs=[
            pl.BlockSpec((tm, D), lambda i, c: (i, 0)),
            pl.BlockSpec((None, 1, D), lambda i, c: (g_layer, 0, 0)),
            in_w(0), in_w(nc), row_vec(0), row_vec(nc),
            pl.BlockSpec((None, conv_w.shape[1], tc), lambda i, c: (layer, 0, c)),
            row_vec(), gate_w(), row_vec(), gate_w(), row_vec(), row_vec(),
            pl.BlockSpec((None, tc, D), lambda i, c: (layer, c, 0)),
            pl.BlockSpec((None, 1, D), lambda i, c: (layer, 0, 0)),
            pl.BlockSpec((None, cru, tc), lambda i, c: (i // tps, 0, c)),
            pl.BlockSpec((None, SUBLANES, tc), lambda i, c: (i // tps, 0, c)),
        ],
        out_specs=[
            pl.BlockSpec((tm, D), lambda i, c: (i, 0), pipeline_mode=pl.Buffered(1)),
            pl.BlockSpec((None, nc, cru, tc), lambda i, c: (i // tps, 0, 0, 0)),
            pl.BlockSpec((None, nc, SUBLANES, tc), lambda i, c: (i // tps, 0, 0, 0)),
        ],
        out_shape=[
            jax.ShapeDtypeStruct((M, D), F32),
            jax.ShapeDtypeStruct((nseq, nc, cru, tc), F32),
            jax.ShapeDtypeStruct((nseq, nc, SUBLANES, tc), F32),
        ],
        scratch_shapes=[
            pltpu.VMEM((tm, D), BF16),
            pltpu.VMEM((cru + tm, tc), F32),
            pltpu.VMEM((tm, tc), F32),
            pltpu.VMEM((tm, tc), F32),
            pltpu.VMEM((tm, tc), F32),
            pltpu.VMEM((nc, cru, tc), F32),
            pltpu.VMEM((nc, SUBLANES, tc), F32),
        ],
        compiler_params=_params(2),
        name="lru_block",
    )(x, gains, w_in, w_in, b_in, b_in, conv_w, conv_b, w_a, b_a, w_i, b_i, lam, w_out, b_out, conv_init, h_init)
    return out, unfold(c_rows), unfold(h_rows)


def _ffn_kernel(x_ref, g_ref, wg_ref, wv_ref, cwg_ref, cwv_ref, cbg_ref, cbv_ref, wd_ref, ig_ref, iv_ref, *rest,
                R, cr, tps, final):
    fg_ref = rest[0] if final else None
    o_ref, sg_ref, sv_ref, xn_ref, eg_ref, ev_ref, cg_ref, cv_ref = rest[1:] if final else rest
    i = pl.program_id(0)
    j = pl.program_id(1)
    tm = x_ref.shape[0]

    @pl.when(j == 0)
    def _():
        x = x_ref[...]
        xn_ref[...] = _rms(x, g_ref[...]).astype(BF16)
        o_ref[...] = x

    @pl.when(i % tps == 0)
    def _():
        cg_ref[j] = ig_ref[...]
        cv_ref[j] = iv_ref[...]

    def side(w_ref, cw_ref, cb_ref, e_ref, c_ref, s_ref):
        up = jnp.dot(xn_ref[...], w_ref[...], preferred_element_type=F32)
        e_ref[0:cr, :] = c_ref[j]
        e_ref[cr:cr + tm, :] = up
        cw = cw_ref[...]
        conv = (cb_ref[...] + cw[2:3] * up + cw[1:2] * e_ref[cr - R:cr - R + tm, :]
                + cw[0:1] * e_ref[cr - 2 * R:cr - 2 * R + tm, :])
        tail = e_ref[tm:tm + cr, :]
        c_ref[j] = tail
        s_ref[j] = tail
        return conv

    cg = side(wg_ref, cwg_ref, cbg_ref, eg_ref, cg_ref, sg_ref)
    cv = side(wv_ref, cwv_ref, cbv_ref, ev_ref, cv_ref, sv_ref)
    act = (_gelu(cg) * cv).astype(BF16)
    o_ref[...] += jnp.dot(act, wd_ref[...], preferred_element_type=F32)
    if final:
        @pl.when(j == pl.num_programs(1) - 1)
        def _():
            o_ref[...] = _rms(o_ref[...], fg_ref[...])


def conv_ffn(x, gains, w_up, conv_w, conv_b, w_down, init, layer, tm, tf, R, final_gain=None):
    M, D = x.shape
    F = w_down.shape[1]
    nf = F // tf
    nseq, cr, _ = init.shape
    tps = (M // tm) // nseq
    final = final_gain is not None
    kern = functools.partial(_ffn_kernel, R=R, cr=cr, tps=tps, final=final)
    K = conv_w.shape[1]
    halves = lambda mk: [mk(0), mk(nf)]
    state_spec = lambda off: pl.BlockSpec((None, cr, tf), lambda i, j: (i // tps, 0, off + j))
    once = pl.Buffered(1)
    out, sg, sv = pl.pallas_call(
        kern,
        grid=(M // tm, nf),
        in_specs=[
            pl.BlockSpec((tm, D), lambda i, j: (i, 0)),
            pl.BlockSpec((None, 1, D), lambda i, j: (layer, 0, 0)),
            *halves(lambda off: pl.BlockSpec((None, D, tf), lambda i, j: (layer, 0, off + j))),
            *halves(lambda off: pl.BlockSpec((None, K, tf), lambda i, j: (layer, 0, off + j))),
            *halves(lambda off: pl.BlockSpec((None, 1, tf), lambda i, j: (layer, 0, off + j))),
            pl.BlockSpec((None, tf, D), lambda i, j: (layer, j, 0)),
            *halves(state_spec),
            *([pl.BlockSpec((1, D), lambda i, j: (0, 0))] if final else []),
        ],
        out_specs=[
            pl.BlockSpec((tm, D), lambda i, j: (i, 0), pipeline_mode=once),
            pl.BlockSpec((None, nf, cr, tf), lambda i, j: (i // tps, 0, 0, 0)),
            pl.BlockSpec((None, nf, cr, tf), lambda i, j: (i // tps, 0, 0, 0)),
        ],
        out_shape=[
            jax.ShapeDtypeStruct((M, D), F32),
            jax.ShapeDtypeStruct((nseq, nf, cr, tf), F32),
            jax.ShapeDtypeStruct((nseq, nf, cr, tf), F32),
        ],
        scratch_shapes=[
            pltpu.VMEM((tm, D), BF16),
            pltpu.VMEM((cr + tm, tf), F32),
            pltpu.VMEM((cr + tm, tf), F32),
            pltpu.VMEM((nf, cr, tf), F32),
            pltpu.VMEM((nf, cr, tf), F32),
        ],
        compiler_params=_params(2),
        name="conv_ffn",
    )(x, gains, w_up, w_up, conv_w, conv_w, conv_b, conv_b, w_down, init, init, *([final_gain] if final else []))
    unfold = lambda s: jnp.swapaxes(s, 1, 2).reshape(nseq, cr, F)
    return out, jnp.concatenate([unfold(sg), unfold(sv)], axis=-1)


ATTN_SUPER = max(w for w, _ in ATTN_GROUPS)
ATTN_BATCH = 8


def _attn_prompt_kernel(sl_ref, q0, k0, v0, q1, k1, v1, q2, k2, v2, o_ref,
                        ke0, ve0, ke1, ve1, ke2, ve2, o_scr, l_scr):
    n = pl.program_id(2)
    qs, ks, vs = (q0, q1, q2), (k0, k1, k2), (v0, v1, v2)
    kes, ves = (ke0, ke1, ke2), (ve0, ve1, ve2)
    SB = o_ref.shape[0]
    blk = ATTN_BLK
    row = lax.broadcasted_iota(jnp.int32, (blk, 2 * blk), 0)
    col = lax.broadcasted_iota(jnp.int32, (blk, 2 * blk), 1)
    steps = row + blk - col
    band = jnp.logical_and(steps >= 0, steps <= blk)
    band_cur = jnp.logical_and(band, col >= blk)
    steps_f = steps.astype(F32)
    ones_v = jnp.ones((2 * blk, HEAD_DIM), BF16)
    inv_sqrt = 1.0 / math.sqrt(HEAD_DIM)
    nt = (((1,), (1,)), ((), ()))

    def rows_of(start, dil):
        return pl.ds(start, blk, stride=dil) if dil > 1 else pl.ds(start, blk)

    for g, (win, dil) in enumerate(ATTN_GROUPS):
        q_ref, ke, ve = qs[g], kes[g], ves[g]

        @pl.when(n == 0)
        def _():
            ke[0:win, :] = jnp.zeros((win, HEAD_DIM), F32)
            ve[0:win, :] = jnp.zeros((win, HEAD_DIM), F32)

        @pl.when(n > 0)
        def _():
            ke[0:win, :] = ke[SB:SB + win, :]
            ve[0:win, :] = ve[SB:SB + win, :]

        ke[win:win + SB, :] = ks[g][...]
        ve[win:win + SB, :] = vs[g][...]
        alibi = (sl_ref[g:g + 1, :] * float(-dil)) * steps_f
        bias_full = jnp.where(band, alibi, NEG)
        bias_first = jnp.where(band_cur, alibi, NEG)

        def batch(it, carry):
            infos = []
            for u in range(ATTN_BATCH):
                bidx = it * ATTN_BATCH + u
                j = bidx // dil
                p0 = j * win + bidx % dil
                if dil == 1:
                    p0 = pl.multiple_of(p0, blk)
                infos.append((p0, jnp.logical_or(n > 0, j > 0)))
            scores = []
            for p0, _ in infos:
                q = q_ref[rows_of(p0, dil), :].astype(BF16)
                kcat = jnp.concatenate([ke[rows_of(p0, dil), :], ke[rows_of(p0 + win, dil), :]], axis=0).astype(BF16)
                scores.append(lax.dot_general(q, kcat, nt, preferred_element_type=F32))
            probs = []
            for (p0, prev_ok), s in zip(infos, scores):
                s = s * inv_sqrt + jnp.where(prev_ok, bias_full, bias_first)
                m = jnp.max(jnp.maximum(s[:, :blk], s[:, blk:]), axis=-1, keepdims=True)
                probs.append((jnp.exp(s - m).astype(BF16), m))
            for (p0, _), (e, m) in zip(infos, probs):
                vcat = jnp.concatenate([ve[rows_of(p0, dil), :], ve[rows_of(p0 + win, dil), :]], axis=0).astype(BF16)
                res = jnp.dot(e, jnp.concatenate([vcat, ones_v], axis=1), preferred_element_type=F32)
                den = res[:, HEAD_DIM:]
                o_scr[g, rows_of(p0, dil), :] = res[:, :HEAD_DIM] / den
                l_scr[g, rows_of(p0, dil), :] = m + jnp.log(den)
            return carry

        lax.fori_loop(0, SB // (blk * ATTN_BATCH), batch, 0)

    chunk = 2 * blk
    for c0 in range(0, SB, chunk):
        ls = [l_scr[g, c0:c0 + chunk, :] for g in range(N_GROUPS)]
        mm = jnp.maximum(jnp.maximum(ls[0], ls[1]), ls[2])
        es = [jnp.exp(l - mm) for l in ls]
        acc = es[0] * o_scr[0, c0:c0 + chunk, :]
        for g in range(1, N_GROUPS):
            acc = acc + es[g] * o_scr[g, c0:c0 + chunk, :]
        o_ref[c0:c0 + chunk, :] = (acc / (es[0] + es[1] + es[2])).astype(BF16)


def attn_prompt(qkv_slabs, B, S):
    n_slabs, M, _ = qkv_slabs.shape
    SB = ATTN_SUPER
    assert n_slabs == 3 * N_GROUPS * GROUP_HEADS and S % SB == 0 and M == B * S
    assert all(w // d == ATTN_BLK for w, d in ATTN_GROUPS) and (SB // ATTN_BLK) % ATTN_BATCH == 0
    nsb = S // SB
    slopes = jnp.asarray(_alibi_slopes(), F32).T
    slopes = jnp.broadcast_to(slopes[:, :, None], (GROUP_HEADS, N_GROUPS, 2 * ATTN_BLK))

    def slab(g, comp):
        base = (g * 3 + comp) * GROUP_HEADS
        return pl.BlockSpec((None, SB, HEAD_DIM), lambda b, h, n: (base + h, b * nsb + n, 0))

    ext = [pltpu.VMEM((w + SB, HEAD_DIM), F32) for w, _ in ATTN_GROUPS for _kv in range(2)]
    return pl.pallas_call(
        _attn_prompt_kernel,
        grid=(B, GROUP_HEADS, nsb),
        in_specs=[pl.BlockSpec((None, N_GROUPS, 2 * ATTN_BLK), lambda b, h, n: (h, 0, 0))]
        + [slab(g, comp) for g in range(N_GROUPS) for comp in range(3)],
        out_specs=pl.BlockSpec((None, SB, HEAD_DIM), lambda b, h, n: (h, b * nsb + n, 0)),
        out_shape=jax.ShapeDtypeStruct((GROUP_HEADS, M, HEAD_DIM), BF16),
        scratch_shapes=ext + [pltpu.VMEM((N_GROUPS, SB, HEAD_DIM), F32), pltpu.VMEM((N_GROUPS, SB, HEAD_DIM), F32)],
        compiler_params=_params(3),
        name="attn_prompt",
    )(slopes, *([qkv_slabs] * (3 * N_GROUPS)))


def _attn_sample_kernel(sl_ref, qkv_ref, c0_ref, c1_ref, c2_ref, o_ref):
    T = qkv_ref.shape[0]
    H = GROUP_HEADS
    caches = (c0_ref, c1_ref, c2_ref)
    rows = c0_ref.shape[0]
    l_idx = lax.broadcasted_iota(jnp.int32, (rows, H, 1), 0)
    l_f = l_idx.astype(F32)
    inv_sqrt = 1.0 / math.sqrt(HEAD_DIM)
    for t in range(T):
        outs, lses = [], []
        for g, (win, dil) in enumerate(ATTN_GROUPS):
            base = g * 3 * H
            slope = sl_ref[g][:, 0:1]
            q = qkv_ref[t, base:base + H, :]
            ph = 0 if dil == 1 else t
            kc = caches[g][:, ph * 2 * H:ph * 2 * H + H, :]
            vc = caches[g][:, ph * 2 * H + H:(ph + 1) * 2 * H, :]
            sc = jnp.sum(kc * q[None], axis=-1, keepdims=True) * inv_sqrt
            if dil == 1:
                sc = jnp.where(l_idx >= t, sc - slope[None] * (float(rows + t) - l_f), NEG)
                new_ts = list(range(t + 1))
            else:
                sc = sc - (slope[None] * float(dil)) * (float(rows) - l_f)
                new_ts = [t]
            m = jnp.max(sc, axis=0)
            s_new = []
            for t2 in new_ts:
                k2 = qkv_ref[t2, base + H:base + 2 * H, :]
                s2 = jnp.sum(q * k2, axis=-1, keepdims=True) * inv_sqrt - slope * float((t - t2) * dil)
                s_new.append(s2)
                m = jnp.maximum(m, s2)
            ec = jnp.exp(sc - m[None])
            den = jnp.sum(ec, axis=0)
            acc = jnp.sum(ec * vc, axis=0)
            for t2, s2 in zip(new_ts, s_new):
                v2 = qkv_ref[t2, base + 2 * H:base + 3 * H, :]
                e2 = jnp.exp(s2 - m)
                den = den + e2
                acc = acc + e2 * v2
            outs.append(acc / den)
            lses.append(m + jnp.log(den))
        mm = jnp.maximum(jnp.maximum(lses[0], lses[1]), lses[2])
        es = [jnp.exp(l - mm) for l in lses]
        o_ref[t] = (es[0] * outs[0] + es[1] * outs[1] + es[2] * outs[2]) / (es[0] + es[1] + es[2])


def attn_sample(qkv, caches, layer, T, Bs):
    H = GROUP_HEADS
    n_rows = qkv.shape[-1] // HEAD_DIM
    views, specs = [], []
    for (win, dil), c in zip(ATTN_GROUPS, caches):
        assert c.shape[1] == Bs and c.shape[2] == win and win // dil == ATTN_BLK and (dil == 1 or T <= dil)
        phases = min(dil, -(-T // 4) * 4)
        views.append(c.reshape(c.shape[0], Bs, win // dil, dil * 2 * H, HEAD_DIM))
        specs.append(pl.BlockSpec((None, None, win // dil, phases * 2 * H, HEAD_DIM), lambda b: (layer, b, 0, 0, 0)))
    slopes = jnp.broadcast_to(jnp.asarray(_alibi_slopes(), F32)[:, :, None], (N_GROUPS, H, HEAD_DIM))
    q_rows = jnp.swapaxes(qkv.reshape(T, Bs, n_rows, HEAD_DIM), 0, 1)
    out = pl.pallas_call(
        _attn_sample_kernel,
        grid=(Bs,),
        in_specs=[pl.BlockSpec((N_GROUPS, H, HEAD_DIM), lambda b: (0, 0, 0)),
                  pl.BlockSpec((None, T, n_rows, HEAD_DIM), lambda b: (b, 0, 0, 0)), *specs],
        out_specs=pl.BlockSpec((None, T, H, HEAD_DIM), lambda b: (b, 0, 0, 0)),
        out_shape=jax.ShapeDtypeStruct((Bs, T, H, HEAD_DIM), F32),
        compiler_params=_params(1),
        name="attn_sample",
    )(slopes, q_rows, *views)
    return jnp.swapaxes(out, 0, 1).reshape(T * Bs, ATTN_WIDTH)


class Tiles(NamedTuple):
    seq: int
    mm: int
    ffn: int
    tf: int


def _run_trunk(x, nseq, R, tiles, lru_h, lru_conv, kv_caches, ffn_conv, P, dims):
    depth = P["norm_mix"].shape[0]
    new_h, new_lconv, new_fconv, qkvs = [], [], [], []
    for layer in range(depth):
        j = layer // 2
        if layer % 2 == 0:
            x, c_rows, h_rows = lru_block(x, P["norm_mix"], layer, P["lru_w_in"], P["lru_b_in"], P["lru_conv_w"],
                                          P["lru_conv_b"], P["lru_w_a"], P["lru_b_a"], P["lru_w_i"], P["lru_b_i"],
                                          P["lru_lambda"], P["lru_w_out"], P["lru_b_out"], lru_conv[j], lru_h[j],
                                          layer=j, tm=tiles.seq, R=R)
            new_h.append(h_rows)
            new_lconv.append(c_rows)
        else:
            prompt = kv_caches is None
            qkv = norm_matmul(x, P["norm_mix"], layer, P["attn_w_qkv"], P["attn_b_zero"], layer=j, tm=tiles.mm,
                              slab_out=prompt)
            qkvs.append(qkv)
            if prompt:
                o = attn_prompt(qkv, *dims)
            else:
                o = attn_sample(qkv, kv_caches, layer=j, T=dims[0], Bs=dims[1])
            x = matmul_res(o, P["attn_w_o"], P["attn_bo_zero"], x, layer=j, tm=tiles.mm)
        x, f_rows = conv_ffn(x, P["norm_ffn"], P["ffn_w_up"], P["ffn_conv_w"], P["ffn_conv_b"], P["ffn_w_down"],
                             ffn_conv[layer], layer=layer, tm=tiles.ffn, tf=tiles.tf, R=R,
                             final_gain=P["norm_final"] if layer == depth - 1 else None)
        new_fconv.append(f_rows)
    return x, new_h, new_lconv, qkvs, new_fconv


def kernel(x_prompt, x_sample, cache_kv_w128, cache_kv_w512, cache_kv_w2048, state_lru_h, state_lru_conv, state_ffn_conv, norm_mix, norm_ffn, norm_final, lru_w_in, lru_b_in, lru_conv_w, lru_conv_b, lru_w_a, lru_b_a, lru_w_i, lru_b_i, lru_lambda, lru_w_out, lru_b_out, attn_w_qkv, attn_w_o, ffn_w_up, ffn_conv_w, ffn_conv_b, ffn_w_down):
    B, S, D = x_prompt.shape
    Bs, T, _ = x_sample.shape
    depth = norm_mix.shape[0]
    n_lru, W = lru_lambda.shape
    n_attn = attn_w_qkv.shape[0]
    F2 = ffn_w_up.shape[-1]
    assert Bs == SUBLANES, "the sample group is laid out time-major with one sublane per sequence"
    row3 = lambda a: a.reshape(a.shape[0], 1, a.shape[-1])
    P = {
        "norm_mix": row3(norm_mix), "norm_ffn": row3(norm_ffn), "norm_final": norm_final.reshape(1, D),
        "lru_w_in": lru_w_in.astype(BF16), "lru_b_in": row3(lru_b_in),
        "lru_conv_w": lru_conv_w, "lru_conv_b": row3(lru_conv_b),
        "lru_w_a": lru_w_a.astype(BF16), "lru_b_a": row3(lru_b_a),
        "lru_w_i": lru_w_i.astype(BF16), "lru_b_i": row3(lru_b_i),
        "lru_lambda": row3(lru_lambda),
        "lru_w_out": lru_w_out.astype(BF16), "lru_b_out": row3(lru_b_out),
        "attn_w_qkv": attn_w_qkv.astype(BF16), "attn_w_o": attn_w_o.astype(BF16),
        "attn_b_zero": jnp.zeros((n_attn, 1, attn_w_qkv.shape[-1]), F32),
        "attn_bo_zero": jnp.zeros((n_attn, 1, D), F32),
        "ffn_w_up": ffn_w_up.astype(BF16), "ffn_conv_w": ffn_conv_w, "ffn_conv_b": row3(ffn_conv_b),
        "ffn_w_down": ffn_w_down.astype(BF16),
    }
    kc = lru_conv_w.shape[1] - 1
    kf = ffn_conv_w.shape[1] - 1

    row_tile = lambda cap: cap if S % cap == 0 else S
    tiles_p = Tiles(seq=row_tile(1024), mm=row_tile(1024), ffn=row_tile(1024), tf=_pick(F2 // 2, 512))
    zeros = lambda n, w: [jnp.zeros((B, SUBLANES, w), F32)] * n
    y_p, h_p, lc_p, qkv_p, fc_p = _run_trunk(
        x_prompt.reshape(B * S, D), B, 1, tiles_p, zeros(n_lru, W), zeros(n_lru, W), None, zeros(depth, F2), P, (B, S))

    tmaj = lambda a: jnp.swapaxes(a, 0, 1).reshape(1, a.shape[1] * a.shape[0], a.shape[-1])
    y_s, h_s, lc_s, qkv_s, fc_s = _run_trunk(
        jnp.swapaxes(x_sample, 0, 1).reshape(T * Bs, D), 1, Bs,
        Tiles(seq=T * Bs, mm=T * Bs, ffn=T * Bs, tf=_pick(F2 // 2, 1024)),
        [state_lru_h[j].reshape(1, Bs, W) for j in range(n_lru)],
        [tmaj(state_lru_conv[j]) for j in range(n_lru)],
        (cache_kv_w128, cache_kv_w512, cache_kv_w2048),
        [tmaj(state_ffn_conv[l]) for l in range(depth)], P, (T, Bs))

    bmaj = lambda a, k: jnp.swapaxes(a.reshape(-1, Bs, a.shape[-1])[-k:], 0, 1)
    kv_p, kv_s = [], []
    for g, (win, dil) in enumerate(ATTN_GROUPS):
        lo = (g * 3 + 1) * ATTN_WIDTH
        keep = min(win, S)
        kv_p.append(jnp.stack([jnp.transpose(
            q.reshape(N_GROUPS, 3, GROUP_HEADS, B, S, HEAD_DIM)[g, 1:3, :, :, S - keep:, :], (2, 3, 0, 1, 4))
            for q in qkv_p], axis=0))
        kv_s.append(jnp.stack([jnp.swapaxes(q.reshape(T, Bs, -1), 0, 1)[:, :, lo:lo + 2 * ATTN_WIDTH]
                               .reshape(Bs, T, 2, GROUP_HEADS, HEAD_DIM) for q in qkv_s], axis=0))
    return (
        y_p.reshape(B, S, D),
        jnp.swapaxes(y_s.reshape(T, Bs, D), 0, 1),
        kv_p[0], kv_p[1], kv_p[2],
        jnp.stack([h[:, SUBLANES - 1] for h in h_p], axis=0),
        jnp.stack([c[:, SUBLANES - kc:] for c in lc_p], axis=0),
        jnp.stack([f[:, SUBLANES - kf:] for f in fc_p], axis=0),
        kv_s[0], kv_s[1], kv_s[2],
        jnp.stack([h[0] for h in h_s], axis=0),
        jnp.stack([bmaj(c[0], kc) for c in lc_s], axis=0),
        jnp.stack([bmaj(f[0], kf) for f in fc_s], axis=0),
    )
```

```python
import functools
import math
from typing import NamedTuple

import jax
import jax.numpy as jnp
from jax import lax
from jax.experimental import pallas as pl
from jax.experimental.pallas import tpu as pltpu

EPS = 1e-6
NEG = -1e30
LRU_C = 8.0
ATTN_GROUPS = ((128, 1), (512, 4), (2048, 16))
N_GROUPS = len(ATTN_GROUPS)
GROUP_HEADS = 8
HEAD_DIM = 128
ATTN_WIDTH = GROUP_HEADS * HEAD_DIM
ATTN_BLK = 128
SUBLANES = 8
LANES = 128
VMEM_LIMIT = 56 * 1024 * 1024

F32 = jnp.float32
F32_TINY = 1.1754944e-38
BF16 = jnp.bfloat16


def _alibi_slopes():
    n = N_GROUPS * GROUP_HEADS
    return [[2.0 ** (-8.0 * (g * GROUP_HEADS + h + 1) / n) for h in range(GROUP_HEADS)] for g in range(N_GROUPS)]


def _gelu(x):
    c = math.sqrt(2.0 / math.pi)
    return x * (0.5 * (1.0 + jnp.tanh(c * (x + 0.044715 * (x * x * x)))))


def _rms(x, g):
    ms = jnp.mean(x * x, axis=-1, keepdims=True)
    return x * lax.rsqrt(ms + EPS) * g


def _pick(n, cap):
    best = None
    for t in range(LANES, min(n, cap) + 1, LANES):
        if n % t == 0:
            best = t
    assert best is not None, (n, cap)
    return best


def _params(n_axes):
    return pltpu.CompilerParams(dimension_semantics=("arbitrary",) * n_axes, vmem_limit_bytes=VMEM_LIMIT)


def _norm_matmul_kernel(x_ref, g_ref, w_ref, b_ref, o_ref, xn_ref):
    @pl.when(pl.program_id(1) == 0)
    def _():
        xn_ref[...] = _rms(x_ref[...], g_ref[...]).astype(BF16)

    acc = jnp.dot(xn_ref[...], w_ref[...], preferred_element_type=F32) + b_ref[...]
    if len(o_ref.shape) == 2:
        o_ref[...] = acc
    else:
        for c in range(o_ref.shape[0]):
            o_ref[c] = acc[:, c * LANES:(c + 1) * LANES]


def norm_matmul(x, gains, g_layer, w, bias, layer, tm, slab_out=False):
    M, D = x.shape
    N = w.shape[-1]
    tn = _pick(N, 1024 if tm > 256 else 2048)
    if slab_out:
        out_spec = pl.BlockSpec((tn // LANES, tm, LANES), lambda i, j: (j, i, 0))
        out_shape = jax.ShapeDtypeStruct((N // LANES, M, LANES), F32)
    else:
        out_spec = pl.BlockSpec((tm, tn), lambda i, j: (i, j))
        out_shape = jax.ShapeDtypeStruct((M, N), F32)
    return pl.pallas_call(
        _norm_matmul_kernel,
        grid=(M // tm, N // tn),
        in_specs=[
            pl.BlockSpec((tm, D), lambda i, j: (i, 0)),
            pl.BlockSpec((None, 1, D), lambda i, j: (g_layer, 0, 0)),
            pl.BlockSpec((None, D, tn), lambda i, j: (layer, 0, j)),
            pl.BlockSpec((None, 1, tn), lambda i, j: (layer, 0, j)),
        ],
        out_specs=out_spec,
        out_shape=out_shape,
        scratch_shapes=[pltpu.VMEM((tm, D), BF16)],
        compiler_params=_params(2),
        name="norm_matmul",
    )(x, gains, w, bias)


def _matmul_res_kernel(a_ref, w_ref, b_ref, r_ref, o_ref):
    if len(a_ref.shape) == 2:
        a = a_ref[...]
    else:
        a = jnp.concatenate([a_ref[c] for c in range(a_ref.shape[0])], axis=1)
    o_ref[...] = r_ref[...] + b_ref[...] + jnp.dot(a.astype(BF16), w_ref[...], preferred_element_type=F32)


def matmul_res(a, w, bias, res, layer, tm):
    M, N = res.shape
    K = w.shape[1]
    tn = _pick(N, 1024 if tm > 256 else 2048)
    if a.ndim == 2:
        a_spec = pl.BlockSpec((tm, K), lambda i, j: (i, 0))
    else:
        a_spec = pl.BlockSpec((K // LANES, tm, LANES), lambda i, j: (0, i, 0))
    return pl.pallas_call(
        _matmul_res_kernel,
        grid=(M // tm, N // tn),
        in_specs=[
            a_spec,
            pl.BlockSpec((None, K, tn), lambda i, j: (layer, 0, j)),
            pl.BlockSpec((None, 1, tn), lambda i, j: (layer, 0, j)),
            pl.BlockSpec((tm, tn), lambda i, j: (i, j)),
        ],
        out_specs=pl.BlockSpec((tm, tn), lambda i, j: (i, j)),
        out_shape=jax.ShapeDtypeStruct((M, N), F32),
        compiler_params=_params(2),
        name="matmul_res",
    )(a, w, bias, res)


def _lru_kernel(gate_ref, u_ref, cw_ref, cb_ref, wa_ref, ba_ref, wi_ref, bi_ref, lam_ref, ci_ref, hi_ref,
                o_ref, cs_ref, hs_ref,
                e_ref, a_scr, b_scr, h_scr, cc_ref, hc_ref, *, R, cru, tps):
    i = pl.program_id(0)
    c = pl.program_id(1)
    tm, tc = u_ref.shape

    @pl.when(i % tps == 0)
    def _():
        cc_ref[c] = ci_ref[...]
        hc_ref[c] = hi_ref[...]

    u = u_ref[...]
    e_ref[0:cru, :] = cc_ref[c]
    e_ref[cru:cru + tm, :] = u
    cw = cw_ref[...]
    uc = (cb_ref[...] + cw[3:4] * u + cw[2:3] * e_ref[cru - R:cru - R + tm, :]
          + cw[1:2] * e_ref[cru - 2 * R:cru - 2 * R + tm, :] + cw[0:1] * e_ref[cru - 3 * R:cru - 3 * R + tm, :])
    tail = e_ref[tm:tm + cru, :]
    cc_ref[c] = tail
    cs_ref[c] = tail

    hd = wa_ref.shape[-1]
    ucb = uc.astype(BF16)
    ra, ri = [], []
    for hh in range(tc // hd):
        ub = ucb[:, hh * hd:(hh + 1) * hd]
        ra.append(jnp.dot(ub, wa_ref[hh], preferred_element_type=F32))
        ri.append(jnp.dot(ub, wi_ref[hh], preferred_element_type=F32))
    r = jax.nn.sigmoid(jnp.concatenate(ra, axis=1) + ba_ref[...])
    ig = jax.nn.sigmoid(jnp.concatenate(ri, axis=1) + bi_ref[...])
    nlam = -lam_ref[...]
    softplus = jnp.maximum(nlam, 0.0) + jnp.log1p(jnp.exp(-jnp.abs(nlam)))
    log_a = (-LRU_C) * r * softplus
    a = jnp.exp(log_a)
    th = jnp.tanh(log_a)
    z = (-2.0 * th) / (1.0 - th)
    bx = (z * lax.rsqrt(jnp.maximum(z, F32_TINY))) * (ig * uc)

    if R == 1:
        A = a.reshape(tm // SUBLANES, SUBLANES, tc)
        B = bx.reshape(tm // SUBLANES, SUBLANES, tc)
        row = lax.broadcasted_iota(jnp.int32, A.shape, 1)
        s = 1
        while s < SUBLANES:
            m = row >= s
            B = jnp.where(m, A * pltpu.roll(B, s, axis=1) + B, B)
            A = jnp.where(m, A * pltpu.roll(A, s, axis=1), A)
            s *= 2
        a_scr[...] = A.reshape(tm, tc)
        b_scr[...] = B.reshape(tm, tc)
        h0 = hc_ref[c][SUBLANES - 1:SUBLANES, :]
    else:
        assert R == SUBLANES
        a_scr[...] = a
        b_scr[...] = bx
        h0 = hc_ref[c]

    def body(g, h):
        r0 = pl.multiple_of(g * SUBLANES, SUBLANES)
        hg = b_scr[pl.ds(r0, SUBLANES), :] + a_scr[pl.ds(r0, SUBLANES), :] * h
        h_scr[pl.ds(r0, SUBLANES), :] = hg
        return hg[SUBLANES - 1:SUBLANES, :] if R == 1 else hg

    n_groups = tm // SUBLANES
    lax.fori_loop(0, n_groups, body, h0, unroll=min(n_groups, 8))
    h_tail = h_scr[tm - SUBLANES:tm, :]
    hc_ref[c] = h_tail
    hs_ref[c] = h_tail
    o_ref[...] = (h_scr[...] * _gelu(gate_ref[...])).astype(BF16)


def lru_core(proj, conv_w, conv_b, w_a, b_a, w_i, b_i, lam, conv_init, h_init, layer, tm, R):
    M, W2 = proj.shape
    W = W2 // 2
    hd = w_a.shape[-1]
    tc = max(hd, _pick(W, 1024))
    assert tc % hd == 0 and W % tc == 0
    nc = W // tc
    nseq, cru, _ = conv_init.shape
    tps = (M // tm) // nseq
    kern = functools.partial(_lru_kernel, R=R, cru=cru, tps=tps)
    row_vec = lambda: pl.BlockSpec((None, 1, tc), lambda i, c: (layer, 0, c))
    gate_w = lambda: pl.BlockSpec((None, tc // hd, hd, hd), lambda i, c: (layer, c, 0, 0))
    unfold = lambda s: jnp.swapaxes(s, 1, 2).reshape(nseq, s.shape[2], W)
    hg, c_rows, h_rows = pl.pallas_call(
        kern,
        grid=(M // tm, nc),
        in_specs=[
            pl.BlockSpec((tm, tc), lambda i, c: (i, c)),
            pl.BlockSpec((tm, tc), lambda i, c: (i, nc + c)),
            pl.BlockSpec((None, conv_w.shape[1], tc), lambda i, c: (layer, 0, c)),
            row_vec(), gate_w(), row_vec(), gate_w(), row_vec(), row_vec(),
            pl.BlockSpec((None, cru, tc), lambda i, c: (i // tps, 0, c)),
            pl.BlockSpec((None, SUBLANES, tc), lambda i, c: (i // tps, 0, c)),
        ],
        out_specs=[
            pl.BlockSpec((tm, tc), lambda i, c: (i, c)),
            pl.BlockSpec((None, nc, cru, tc), lambda i, c: (i // tps, 0, 0, 0)),
            pl.BlockSpec((None, nc, SUBLANES, tc), lambda i, c: (i // tps, 0, 0, 0)),
        ],
        out_shape=[
            jax.ShapeDtypeStruct((M, W), BF16),
            jax.ShapeDtypeStruct((nseq, nc, cru, tc), F32),
            jax.ShapeDtypeStruct((nseq, nc, SUBLANES, tc), F32),
        ],
        scratch_shapes=[
            pltpu.VMEM((cru + tm, tc), F32),
            pltpu.VMEM((tm, tc), F32),
            pltpu.VMEM((tm, tc), F32),
            pltpu.VMEM((tm, tc), F32),
            pltpu.VMEM((nc, cru, tc), F32),
            pltpu.VMEM((nc, SUBLANES, tc), F32),
        ],
        compiler_params=_params(2),
        name="lru_core",
    )(proj, proj, conv_w, conv_b, w_a, b_a, w_i, b_i, lam, conv_init, h_init)
    return hg, unfold(c_rows), unfold(h_rows)


def _ffn_kernel(x_ref, g_ref, wg_ref, wv_ref, cwg_ref, cwv_ref, cbg_ref, cbv_ref, wd_ref, ig_ref, iv_ref, *rest,
                R, cr, tps, final):
    fg_ref = rest[0] if final else None
    o_ref, sg_ref, sv_ref, xn_ref, eg_ref, ev_ref, cg_ref, cv_ref = rest[1:] if final else rest
    i = pl.program_id(0)
    j = pl.program_id(1)
    tm = x_ref.shape[0]

    @pl.when(j == 0)
    def _():
        x = x_ref[...]
        xn_ref[...] = _rms(x, g_ref[...]).astype(BF16)
        o_ref[...] = x

    @pl.when(i % tps == 0)
    def _():
        cg_ref[j] = ig_ref[...]
        cv_ref[j] = iv_ref[...]

    def side(w_ref, cw_ref, cb_ref, e_ref, c_ref, s_ref):
        up = jnp.dot(xn_ref[...], w_ref[...], preferred_element_type=F32)
        e_ref[0:cr, :] = c_ref[j]
        e_ref[cr:cr + tm, :] = up
        cw = cw_ref[...]
        conv = (cb_ref[...] + cw[2:3] * up + cw[1:2] * e_ref[cr - R:cr - R + tm, :]
                + cw[0:1] * e_ref[cr - 2 * R:cr - 2 * R + tm, :])
        tail = e_ref[tm:tm + cr, :]
        c_ref[j] = tail
        s_ref[j] = tail
        return conv

    cg = side(wg_ref, cwg_ref, cbg_ref, eg_ref, cg_ref, sg_ref)
    cv = side(wv_ref, cwv_ref, cbv_ref, ev_ref, cv_ref, sv_ref)
    act = (_gelu(cg) * cv).astype(BF16)
    o_ref[...] += jnp.dot(act, wd_ref[...], preferred_element_type=F32)
    if final:
        @pl.when(j == pl.num_programs(1) - 1)
        def _():
            o_ref[...] = _rms(o_ref[...], fg_ref[...])


def conv_ffn(x, gains, w_up, conv_w, conv_b, w_down, init, layer, tm, tf, R, final_gain=None):
    M, D = x.shape
    F = w_down.shape[1]
    nf = F // tf
    nseq, cr, _ = init.shape
    tps = (M // tm) // nseq
    final = final_gain is not None
    kern = functools.partial(_ffn_kernel, R=R, cr=cr, tps=tps, final=final)
    K = conv_w.shape[1]
    halves = lambda mk: [mk(0), mk(nf)]
    state_spec = lambda off: pl.BlockSpec((None, cr, tf), lambda i, j: (i // tps, 0, off + j))
    once = pl.Buffered(1)
    out, sg, sv = pl.pallas_call(
        kern,
        grid=(M // tm, nf),
        in_specs=[
            pl.BlockSpec((tm, D), lambda i, j: (i, 0)),
            pl.BlockSpec((None, 1, D), lambda i, j: (layer, 0, 0)),
            *halves(lambda off: pl.BlockSpec((None, D, tf), lambda i, j: (layer, 0, off + j))),
            *halves(lambda off: pl.BlockSpec((None, K, tf), lambda i, j: (layer, 0, off + j))),
            *halves(lambda off: pl.BlockSpec((None, 1, tf), lambda i, j: (layer, 0, off + j))),
            pl.BlockSpec((None, tf, D), lambda i, j: (layer, j, 0)),
            *halves(state_spec),
            *([pl.BlockSpec((1, D), lambda i, j: (0, 0))] if final else []),
        ],
        out_specs=[
            pl.BlockSpec((tm, D), lambda i, j: (i, 0), pipeline_mode=once),
            pl.BlockSpec((None, nf, cr, tf), lambda i, j: (i // tps, 0, 0, 0)),
            pl.BlockSpec((None, nf, cr, tf), lambda i, j: (i // tps, 0, 0, 0)),
        ],
        out_shape=[
            jax.ShapeDtypeStruct((M, D), F32),
            jax.ShapeDtypeStruct((nseq, nf, cr, tf), F32),
            jax.ShapeDtypeStruct((nseq, nf, cr, tf), F32),
        ],
        scratch_shapes=[
            pltpu.VMEM((tm, D), BF16),
            pltpu.VMEM((cr + tm, tf), F32),
            pltpu.VMEM((cr + tm, tf), F32),
            pltpu.VMEM((nf, cr, tf), F32),
            pltpu.VMEM((nf, cr, tf), F32),
        ],
        compiler_params=_params(2),
        name="conv_ffn",
    )(x, gains, w_up, w_up, conv_w, conv_w, conv_b, conv_b, w_down, init, init, *([final_gain] if final else []))
    unfold = lambda s: jnp.swapaxes(s, 1, 2).reshape(nseq, cr, F)
    return out, jnp.concatenate([unfold(sg), unfold(sv)], axis=-1)


ATTN_SUPER = max(w for w, _ in ATTN_GROUPS)
ATTN_BATCH = 8


def _attn_prompt_kernel(sl_ref, q0, k0, v0, q1, k1, v1, q2, k2, v2, o_ref,
                        ke0, ve0, ke1, ve1, ke2, ve2, o_scr, l_scr):
    n = pl.program_id(2)
    qs, ks, vs = (q0, q1, q2), (k0, k1, k2), (v0, v1, v2)
    kes, ves = (ke0, ke1, ke2), (ve0, ve1, ve2)
    SB = o_ref.shape[0]
    blk = ATTN_BLK
    row = lax.broadcasted_iota(jnp.int32, (blk, 2 * blk), 0)
    col = lax.broadcasted_iota(jnp.int32, (blk, 2 * blk), 1)
    steps = row + blk - col
    band = jnp.logical_and(steps >= 0, steps <= blk)
    band_cur = jnp.logical_and(band, col >= blk)
    steps_f = steps.astype(F32)
    ones_v = jnp.ones((2 * blk, HEAD_DIM), BF16)
    inv_sqrt = 1.0 / math.sqrt(HEAD_DIM)
    nt = (((1,), (1,)), ((), ()))

    def rows_of(start, dil):
        return pl.ds(start, blk, stride=dil) if dil > 1 else pl.ds(start, blk)

    for g, (win, dil) in enumerate(ATTN_GROUPS):
        q_ref, ke, ve = qs[g], kes[g], ves[g]

        @pl.when(n == 0)
        def _():
            ke[0:win, :] = jnp.zeros((win, HEAD_DIM), F32)
            ve[0:win, :] = jnp.zeros((win, HEAD_DIM), F32)

        @pl.when(n > 0)
        def _():
            ke[0:win, :] = ke[SB:SB + win, :]
            ve[0:win, :] = ve[SB:SB + win, :]

        ke[win:win + SB, :] = ks[g][...]
        ve[win:win + SB, :] = vs[g][...]
        alibi = (sl_ref[g:g + 1, :] * float(-dil)) * steps_f
        bias_full = jnp.where(band, alibi, NEG)
        bias_first = jnp.where(band_cur, alibi, NEG)

        def batch(it, carry):
            infos = []
            for u in range(ATTN_BATCH):
                bidx = it * ATTN_BATCH + u
                j = bidx // dil
                p0 = j * win + bidx % dil
                if dil == 1:
                    p0 = pl.multiple_of(p0, blk)
                infos.append((p0, jnp.logical_or(n > 0, j > 0)))
            scores = []
            for p0, _ in infos:
                q = q_ref[rows_of(p0, dil), :].astype(BF16)
                kcat = jnp.concatenate([ke[rows_of(p0, dil), :], ke[rows_of(p0 + win, dil), :]], axis=0).astype(BF16)
                scores.append(lax.dot_general(q, kcat, nt, preferred_element_type=F32))
            probs = []
            for (p0, prev_ok), s in zip(infos, scores):
                s = s * inv_sqrt + jnp.where(prev_ok, bias_full, bias_first)
                m = jnp.max(jnp.maximum(s[:, :blk], s[:, blk:]), axis=-1, keepdims=True)
                probs.append((jnp.exp(s - m).astype(BF16), m))
            for (p0, _), (e, m) in zip(infos, probs):
                vcat = jnp.concatenate([ve[rows_of(p0, dil), :], ve[rows_of(p0 + win, dil), :]], axis=0).astype(BF16)
                res = jnp.dot(e, jnp.concatenate([vcat, ones_v], axis=1), preferred_element_type=F32)
                den = res[:, HEAD_DIM:]
                o_scr[g, rows_of(p0, dil), :] = res[:, :HEAD_DIM] / den
                l_scr[g, rows_of(p0, dil), :] = m + jnp.log(den)
            return carry

        lax.fori_loop(0, SB // (blk * ATTN_BATCH), batch, 0)

    chunk = 2 * blk
    for c0 in range(0, SB, chunk):
        ls = [l_scr[g, c0:c0 + chunk, :] for g in range(N_GROUPS)]
        mm = jnp.maximum(jnp.maximum(ls[0], ls[1]), ls[2])
        es = [jnp.exp(l - mm) for l in ls]
        acc = es[0] * o_scr[0, c0:c0 + chunk, :]
        for g in range(1, N_GROUPS):
            acc = acc + es[g] * o_scr[g, c0:c0 + chunk, :]
        o_ref[c0:c0 + chunk, :] = (acc / (es[0] + es[1] + es[2])).astype(BF16)


def attn_prompt(qkv_slabs, B, S):
    n_slabs, M, _ = qkv_slabs.shape
    SB = ATTN_SUPER
    assert n_slabs == 3 * N_GROUPS * GROUP_HEADS and S % SB == 0 and M == B * S
    assert all(w // d == ATTN_BLK for w, d in ATTN_GROUPS) and (SB // ATTN_BLK) % ATTN_BATCH == 0
    nsb = S // SB
    slopes = jnp.asarray(_alibi_slopes(), F32).T
    slopes = jnp.broadcast_to(slopes[:, :, None], (GROUP_HEADS, N_GROUPS, 2 * ATTN_BLK))

    def slab(g, comp):
        base = (g * 3 + comp) * GROUP_HEADS
        return pl.BlockSpec((None, SB, HEAD_DIM), lambda b, h, n: (base + h, b * nsb + n, 0))

    ext = [pltpu.VMEM((w + SB, HEAD_DIM), F32) for w, _ in ATTN_GROUPS for _kv in range(2)]
    return pl.pallas_call(
        _attn_prompt_kernel,
        grid=(B, GROUP_HEADS, nsb),
        in_specs=[pl.BlockSpec((None, N_GROUPS, 2 * ATTN_BLK), lambda b, h, n: (h, 0, 0))]
        + [slab(g, comp) for g in range(N_GROUPS) for comp in range(3)],
        out_specs=pl.BlockSpec((None, SB, HEAD_DIM), lambda b, h, n: (h, b * nsb + n, 0)),
        out_shape=jax.ShapeDtypeStruct((GROUP_HEADS, M, HEAD_DIM), BF16),
        scratch_shapes=ext + [pltpu.VMEM((N_GROUPS, SB, HEAD_DIM), F32), pltpu.VMEM((N_GROUPS, SB, HEAD_DIM), F32)],
        compiler_params=_params(3),
        name="attn_prompt",
    )(slopes, *([qkv_slabs] * (3 * N_GROUPS)))


def _attn_sample_kernel(sl_ref, qkv_ref, c0_ref, c1_ref, c2_ref, o_ref):
    T = qkv_ref.shape[0]
    H = GROUP_HEADS
    caches = (c0_ref, c1_ref, c2_ref)
    rows = c0_ref.shape[0]
    l_idx = lax.broadcasted_iota(jnp.int32, (rows, H, 1), 0)
    l_f = l_idx.astype(F32)
    inv_sqrt = 1.0 / math.sqrt(HEAD_DIM)
    for t in range(T):
        outs, lses = [], []
        for g, (win, dil) in enumerate(ATTN_GROUPS):
            base = g * 3 * H
            slope = sl_ref[g][:, 0:1]
            q = qkv_ref[t, base:base + H, :]
            ph = 0 if dil == 1 else t
            kc = caches[g][:, ph * 2 * H:ph * 2 * H + H, :]
            vc = caches[g][:, ph * 2 * H + H:(ph + 1) * 2 * H, :]
            sc = jnp.sum(kc * q[None], axis=-1, keepdims=True) * inv_sqrt
            if dil == 1:
                sc = jnp.where(l_idx >= t, sc - slope[None] * (float(rows + t) - l_f), NEG)
                new_ts = list(range(t + 1))
            else:
                sc = sc - (slope[None] * float(dil)) * (float(rows) - l_f)
                new_ts = [t]
            m = jnp.max(sc, axis=0)
            s_new = []
            for t2 in new_ts:
                k2 = qkv_ref[t2, base + H:base + 2 * H, :]
                s2 = jnp.sum(q * k2, axis=-1, keepdims=True) * inv_sqrt - slope * float((t - t2) * dil)
                s_new.append(s2)
                m = jnp.maximum(m, s2)
            ec = jnp.exp(sc - m[None])
            den = jnp.sum(ec, axis=0)
            acc = jnp.sum(ec * vc, axis=0)
            for t2, s2 in zip(new_ts, s_new):
                v2 = qkv_ref[t2, base + 2 * H:base + 3 * H, :]
                e2 = jnp.exp(s2 - m)
                den = den + e2
                acc = acc + e2 * v2
            outs.append(acc / den)
            lses.append(m + jnp.log(den))
        mm = jnp.maximum(jnp.maximum(lses[0], lses[1]), lses[2])
        es = [jnp.exp(l - mm) for l in lses]
        o_ref[t] = (es[0] * outs[0] + es[1] * outs[1] + es[2] * outs[2]) / (es[0] + es[1] + es[2])


def attn_sample(qkv, caches, layer, T, Bs):
    H = GROUP_HEADS
    n_rows = qkv.shape[-1] // HEAD_DIM
    views, specs = [], []
    for (win, dil), c in zip(ATTN_GROUPS, caches):
        assert c.shape[1] == Bs and c.shape[2] == win and win // dil == ATTN_BLK and (dil == 1 or T <= dil)
        phases = min(dil, -(-T // 4) * 4)
        views.append(c.reshape(c.shape[0], Bs, win // dil, dil * 2 * H, HEAD_DIM))
        specs.append(pl.BlockSpec((None, None, win // dil, phases * 2 * H, HEAD_DIM), lambda b: (layer, b, 0, 0, 0)))
    slopes = jnp.broadcast_to(jnp.asarray(_alibi_slopes(), F32)[:, :, None], (N_GROUPS, H, HEAD_DIM))
    q_rows = jnp.swapaxes(qkv.reshape(T, Bs, n_rows, HEAD_DIM), 0, 1)
    out = pl.pallas_call(
        _attn_sample_kernel,
        grid=(Bs,),
        in_specs=[pl.BlockSpec((N_GROUPS, H, HEAD_DIM), lambda b: (0, 0, 0)),
                  pl.BlockSpec((None, T, n_rows, HEAD_DIM), lambda b: (b, 0, 0, 0)), *specs],
        out_specs=pl.BlockSpec((None, T, H, HEAD_DIM), lambda b: (b, 0, 0, 0)),
        out_shape=jax.ShapeDtypeStruct((Bs, T, H, HEAD_DIM), F32),
        compiler_params=_params(1),
        name="attn_sample",
    )(slopes, q_rows, *views)
    return jnp.swapaxes(out, 0, 1).reshape(T * Bs, ATTN_WIDTH)


class Tiles(NamedTuple):
    seq: int
    mm: int
    ffn: int
    tf: int


def _run_trunk(x, nseq, R, tiles, lru_h, lru_conv, kv_caches, ffn_conv, P, dims):
    depth = P["norm_mix"].shape[0]
    new_h, new_lconv, new_fconv, qkvs = [], [], [], []
    for layer in range(depth):
        j = layer // 2
        if layer % 2 == 0:
            proj = norm_matmul(x, P["norm_mix"], layer, P["lru_w_in"], P["lru_b_in"], layer=j, tm=tiles.mm)
            hg, c_rows, h_rows = lru_core(proj, P["lru_conv_w"], P["lru_conv_b"], P["lru_w_a"], P["lru_b_a"],
                                          P["lru_w_i"], P["lru_b_i"], P["lru_lambda"], lru_conv[j], lru_h[j],
                                          layer=j, tm=tiles.seq, R=R)
            x = matmul_res(hg, P["lru_w_out"], P["lru_b_out"], x, layer=j, tm=tiles.mm)
            new_h.append(h_rows)
            new_lconv.append(c_rows)
        else:
            prompt = kv_caches is None
            qkv = norm_matmul(x, P["norm_mix"], layer, P["attn_w_qkv"], P["attn_b_zero"], layer=j, tm=tiles.mm,
                              slab_out=prompt)
            qkvs.append(qkv)
            if prompt:
                o = attn_prompt(qkv, *dims)
            else:
                o = attn_sample(qkv, kv_caches, layer=j, T=dims[0], Bs=dims[1])
            x = matmul_res(o, P["attn_w_o"], P["attn_bo_zero"], x, layer=j, tm=tiles.mm)
        x, f_rows = conv_ffn(x, P["norm_ffn"], P["ffn_w_up"], P["ffn_conv_w"], P["ffn_conv_b"], P["ffn_w_down"],
                             ffn_conv[layer], layer=layer, tm=tiles.ffn, tf=tiles.tf, R=R,
                             final_gain=P["norm_final"] if layer == depth - 1 else None)
        new_fconv.append(f_rows)
    return x, new_h, new_lconv, qkvs, new_fconv


def kernel(x_prompt, x_sample, cache_kv_w128, cache_kv_w512, cache_kv_w2048, state_lru_h, state_lru_conv, state_ffn_conv, norm_mix, norm_ffn, norm_final, lru_w_in, lru_b_in, lru_conv_w, lru_conv_b, lru_w_a, lru_b_a, lru_w_i, lru_b_i, lru_lambda, lru_w_out, lru_b_out, attn_w_qkv, attn_w_o, ffn_w_up, ffn_conv_w, ffn_conv_b, ffn_w_down):
    B, S, D = x_prompt.shape
    Bs, T, _ = x_sample.shape
    depth = norm_mix.shape[0]
    n_lru, W = lru_lambda.shape
    n_attn = attn_w_qkv.shape[0]
    F2 = ffn_w_up.shape[-1]
    assert Bs == SUBLANES, "the sample group is laid out time-major with one sublane per sequence"
    row3 = lambda a: a.reshape(a.shape[0], 1, a.shape[-1])
    P = {
        "norm_mix": row3(norm_mix), "norm_ffn": row3(norm_ffn), "norm_final": norm_final.reshape(1, D),
        "lru_w_in": lru_w_in.astype(BF16), "lru_b_in": row3(lru_b_in),
        "lru_conv_w": lru_conv_w, "lru_conv_b": row3(lru_conv_b),
        "lru_w_a": lru_w_a.astype(BF16), "lru_b_a": row3(lru_b_a),
        "lru_w_i": lru_w_i.astype(BF16), "lru_b_i": row3(lru_b_i),
        "lru_lambda": row3(lru_lambda),
        "lru_w_out": lru_w_out.astype(BF16), "lru_b_out": row3(lru_b_out),
        "attn_w_qkv": attn_w_qkv.astype(BF16), "attn_w_o": attn_w_o.astype(BF16),
        "attn_b_zero": jnp.zeros((n_attn, 1, attn_w_qkv.shape[-1]), F32),
        "attn_bo_zero": jnp.zeros((n_attn, 1, D), F32),
        "ffn_w_up": ffn_w_up.astype(BF16), "ffn_conv_w": ffn_conv_w, "ffn_conv_b": row3(ffn_conv_b),
        "ffn_w_down": ffn_w_down.astype(BF16),
    }
    kc = lru_conv_w.shape[1] - 1
    kf = ffn_conv_w.shape[1] - 1

    row_tile = lambda cap: cap if S % cap == 0 else S
    tiles_p = Tiles(seq=row_tile(512), mm=row_tile(1024), ffn=row_tile(1024), tf=_pick(F2 // 2, 512))
    zeros = lambda n, w: [jnp.zeros((B, SUBLANES, w), F32)] * n
    y_p, h_p, lc_p, qkv_p, fc_p = _run_trunk(
        x_prompt.reshape(B * S, D), B, 1, tiles_p, zeros(n_lru, W), zeros(n_lru, W), None, zeros(depth, F2), P, (B, S))

    tmaj = lambda a: jnp.swapaxes(a, 0, 1).reshape(1, a.shape[1] * a.shape[0], a.shape[-1])
    y_s, h_s, lc_s, qkv_s, fc_s = _run_trunk(
        jnp.swapaxes(x_sample, 0, 1).reshape(T * Bs, D), 1, Bs,
        Tiles(seq=T * Bs, mm=T * Bs, ffn=T * Bs, tf=_pick(F2 // 2, 1024)),
        [state_lru_h[j].reshape(1, Bs, W) for j in range(n_lru)],
        [tmaj(state_lru_conv[j]) for j in range(n_lru)],
        (cache_kv_w128, cache_kv_w512, cache_kv_w2048),
        [tmaj(state_ffn_conv[l]) for l in range(depth)], P, (T, Bs))

    bmaj = lambda a, k: jnp.swapaxes(a.reshape(-1, Bs, a.shape[-1])[-k:], 0, 1)
    kv_p, kv_s = [], []
    for g, (win, dil) in enumerate(ATTN_GROUPS):
        lo = (g * 3 + 1) * ATTN_WIDTH
        keep = min(win, S)
        kv_p.append(jnp.stack([jnp.transpose(
            q.reshape(N_GROUPS, 3, GROUP_HEADS, B, S, HEAD_DIM)[g, 1:3, :, :, S - keep:, :], (2, 3, 0, 1, 4))
            for q in qkv_p], axis=0))
        kv_s.append(jnp.stack([jnp.swapaxes(q.reshape(T, Bs, -1), 0, 1)[:, :, lo:lo + 2 * ATTN_WIDTH]
                               .reshape(Bs, T, 2, GROUP_HEADS, HEAD_DIM) for q in qkv_s], axis=0))
    return (
        y_p.reshape(B, S, D),
        jnp.swapaxes(y_s.reshape(T, Bs, D), 0, 1),
        kv_p[0], kv_p[1], kv_p[2],
        jnp.stack([h[:, SUBLANES - 1] for h in h_p], axis=0),
        jnp.stack([c[:, SUBLANES - kc:] for c in lc_p], axis=0),
        jnp.stack([f[:, SUBLANES - kf:] for f in fc_p], axis=0),
        kv_s[0], kv_s[1], kv_s[2],
        jnp.stack([h[0] for h in h_s], axis=0),
        jnp.stack([bmaj(c[0], kc) for c in lc_s], axis=0),
        jnp.stack([bmaj(f[0], kf) for f in fc_s], axis=0),
    )
```

```python
import functools
import math
from typing import NamedTuple

import jax
import jax.numpy as jnp
from jax import lax
from jax.experimental import pallas as pl
from jax.experimental.pallas import tpu as pltpu

EPS = 1e-6
NEG = -1e30
LRU_C = 8.0
ATTN_GROUPS = ((128, 1), (512, 4), (2048, 16))
N_GROUPS = len(ATTN_GROUPS)
GROUP_HEADS = 8
HEAD_DIM = 128
ATTN_WIDTH = GROUP_HEADS * HEAD_DIM
ATTN_BLK = 128
SUBLANES = 8
LANES = 128
VMEM_LIMIT = 56 * 1024 * 1024

F32 = jnp.float32
F32_TINY = 1.1754944e-38
BF16 = jnp.bfloat16


def _alibi_slopes():
    n = N_GROUPS * GROUP_HEADS
    return [[2.0 ** (-8.0 * (g * GROUP_HEADS + h + 1) / n) for h in range(GROUP_HEADS)] for g in range(N_GROUPS)]


def _gelu(x):
    c = math.sqrt(2.0 / math.pi)
    return x * (0.5 * (1.0 + jnp.tanh(c * (x + 0.044715 * (x * x * x)))))


def _rms(x, g):
    ms = jnp.mean(x * x, axis=-1, keepdims=True)
    return x * lax.rsqrt(ms + EPS) * g


def _pick(n, cap):
    best = None
    for t in range(LANES, min(n, cap) + 1, LANES):
        if n % t == 0:
            best = t
    assert best is not None, (n, cap)
    return best


def _params(n_axes):
    return pltpu.CompilerParams(dimension_semantics=("arbitrary",) * n_axes, vmem_limit_bytes=VMEM_LIMIT)


def _as_bf16_weight(w_ref, emit_refs):
    w = w_ref[...]
    if w.dtype != BF16:
        w = w.astype(BF16)
        emit_refs[0][...] = w
    return w


def _norm_matmul_kernel(x_ref, g_ref, w_ref, b_ref, o_ref, *rest):
    xn_ref = rest[-1]

    @pl.when(pl.program_id(1) == 0)
    def _():
        xn_ref[...] = _rms(x_ref[...], g_ref[...]).astype(BF16)

    acc = jnp.dot(xn_ref[...], _as_bf16_weight(w_ref, rest), preferred_element_type=F32) + b_ref[...]
    if len(o_ref.shape) == 2:
        o_ref[...] = acc
    else:
        for c in range(o_ref.shape[0]):
            o_ref[c] = acc[:, c * LANES:(c + 1) * LANES]


class Weight(NamedTuple):
    arr: jax.Array
    layer: int


def _weight_specs(w, block, index):
    spec = pl.BlockSpec((None, *block), lambda *ij: (w.layer, *index(*ij)))
    if w.arr.dtype == BF16:
        return spec, [], []
    return (spec, [pl.BlockSpec((None, *block), lambda *ij: (0, *index(*ij)))],
            [jax.ShapeDtypeStruct((1, *w.arr.shape[1:]), BF16)])


def _tile_cap(tm, w):
    return 1024 if tm > 256 or w.arr.dtype != BF16 else 2048


def norm_matmul(x, gains, g_layer, w, bias, b_layer, tm, slab_out=False):
    M, D = x.shape
    N = w.arr.shape[-1]
    tn = _pick(N, _tile_cap(tm, w))
    if slab_out:
        out_spec = pl.BlockSpec((tn // LANES, tm, LANES), lambda i, j: (j, i, 0))
        out_shape = jax.ShapeDtypeStruct((N // LANES, M, LANES), F32)
    else:
        out_spec = pl.BlockSpec((tm, tn), lambda i, j: (i, j))
        out_shape = jax.ShapeDtypeStruct((M, N), F32)
    w_spec, emit_specs, emit_shapes = _weight_specs(w, (D, tn), lambda i, j: (0, j))
    assert not emit_specs or M == tm, "a weight tile must be visited once to be emitted"
    out, *emitted = pl.pallas_call(
        _norm_matmul_kernel,
        grid=(M // tm, N // tn),
        in_specs=[
            pl.BlockSpec((tm, D), lambda i, j: (i, 0)),
            pl.BlockSpec((None, 1, D), lambda i, j: (g_layer, 0, 0)),
            w_spec,
            pl.BlockSpec((None, 1, tn), lambda i, j: (b_layer, 0, j)),
        ],
        out_specs=[out_spec, *emit_specs],
        out_shape=[out_shape, *emit_shapes],
        scratch_shapes=[pltpu.VMEM((tm, D), BF16)],
        compiler_params=_params(2),
        name="norm_matmul",
    )(x, gains, w.arr, bias)
    return out, (emitted[0] if emitted else None)


def _matmul_res_kernel(a_ref, w_ref, b_ref, r_ref, o_ref, *emit):
    if len(a_ref.shape) == 2:
        a = a_ref[...]
    else:
        a = jnp.concatenate([a_ref[c] for c in range(a_ref.shape[0])], axis=1)
    o_ref[...] = r_ref[...] + b_ref[...] + jnp.dot(a.astype(BF16), _as_bf16_weight(w_ref, emit),
                                                   preferred_element_type=F32)


def matmul_res(a, w, bias, b_layer, res, tm):
    M, N = res.shape
    K = w.arr.shape[1]
    tn = _pick(N, _tile_cap(tm, w))
    if a.ndim == 2:
        a_spec = pl.BlockSpec((tm, K), lambda i, j: (i, 0))
    else:
        a_spec = pl.BlockSpec((K // LANES, tm, LANES), lambda i, j: (0, i, 0))
    w_spec, emit_specs, emit_shapes = _weight_specs(w, (K, tn), lambda i, j: (0, j))
    assert not emit_specs or M == tm, "a weight tile must be visited once to be emitted"
    out, *emitted = pl.pallas_call(
        _matmul_res_kernel,
        grid=(M // tm, N // tn),
        in_specs=[
            a_spec,
            w_spec,
            pl.BlockSpec((None, 1, tn), lambda i, j: (b_layer, 0, j)),
            pl.BlockSpec((tm, tn), lambda i, j: (i, j)),
        ],
        out_specs=[pl.BlockSpec((tm, tn), lambda i, j: (i, j)), *emit_specs],
        out_shape=[jax.ShapeDtypeStruct((M, N), F32), *emit_shapes],
        compiler_params=_params(2),
        name="matmul_res",
    )(a, w.arr, bias, res)
    return out, (emitted[0] if emitted else None)


def _lru_kernel(gate_ref, u_ref, cw_ref, cb_ref, wa_ref, ba_ref, wi_ref, bi_ref, lam_ref, ci_ref, hi_ref,
                o_ref, cs_ref, hs_ref,
                e_ref, a_scr, b_scr, h_scr, cc_ref, hc_ref, *, R, cru, tps):
    i = pl.program_id(0)
    c = pl.program_id(1)
    tm, tc = u_ref.shape

    @pl.when(i % tps == 0)
    def _():
        cc_ref[c] = ci_ref[...]
        hc_ref[c] = hi_ref[...]

    u = u_ref[...]
    e_ref[0:cru, :] = cc_ref[c]
    e_ref[cru:cru + tm, :] = u
    cw = cw_ref[...]
    uc = (cb_ref[...] + cw[3:4] * u + cw[2:3] * e_ref[cru - R:cru - R + tm, :]
          + cw[1:2] * e_ref[cru - 2 * R:cru - 2 * R + tm, :] + cw[0:1] * e_ref[cru - 3 * R:cru - 3 * R + tm, :])
    tail = e_ref[tm:tm + cru, :]
    cc_ref[c] = tail
    cs_ref[c] = tail

    hd = wa_ref.shape[-1]
    ucb = uc.astype(BF16)
    ra, ri = [], []
    for hh in range(tc // hd):
        ub = ucb[:, hh * hd:(hh + 1) * hd]
        ra.append(jnp.dot(ub, wa_ref[hh].astype(BF16), preferred_element_type=F32))
        ri.append(jnp.dot(ub, wi_ref[hh].astype(BF16), preferred_element_type=F32))
    r = jax.nn.sigmoid(jnp.concatenate(ra, axis=1) + ba_ref[...])
    ig = jax.nn.sigmoid(jnp.concatenate(ri, axis=1) + bi_ref[...])
    nlam = -lam_ref[...]
    softplus = jnp.maximum(nlam, 0.0) + jnp.log1p(jnp.exp(-jnp.abs(nlam)))
    log_a = (-LRU_C) * r * softplus
    a = jnp.exp(log_a)
    th = jnp.tanh(log_a)
    z = (-2.0 * th) / (1.0 - th)
    bx = (z * lax.rsqrt(jnp.maximum(z, F32_TINY))) * (ig * uc)

    if R == 1:
        A = a.reshape(tm // SUBLANES, SUBLANES, tc)
        B = bx.reshape(tm // SUBLANES, SUBLANES, tc)
        row = lax.broadcasted_iota(jnp.int32, A.shape, 1)
        s = 1
        while s < SUBLANES:
            m = row >= s
            B = jnp.where(m, A * pltpu.roll(B, s, axis=1) + B, B)
            A = jnp.where(m, A * pltpu.roll(A, s, axis=1), A)
            s *= 2
        a_scr[...] = A.reshape(tm, tc)
        b_scr[...] = B.reshape(tm, tc)
        h0 = hc_ref[c][SUBLANES - 1:SUBLANES, :]
    else:
        assert R == SUBLANES
        a_scr[...] = a
        b_scr[...] = bx
        h0 = hc_ref[c]

    def body(g, h):
        r0 = pl.multiple_of(g * SUBLANES, SUBLANES)
        hg = b_scr[pl.ds(r0, SUBLANES), :] + a_scr[pl.ds(r0, SUBLANES), :] * h
        h_scr[pl.ds(r0, SUBLANES), :] = hg
        return hg[SUBLANES - 1:SUBLANES, :] if R == 1 else hg

    n_groups = tm // SUBLANES
    lax.fori_loop(0, n_groups, body, h0, unroll=min(n_groups, 8))
    h_tail = h_scr[tm - SUBLANES:tm, :]
    hc_ref[c] = h_tail
    hs_ref[c] = h_tail
    o_ref[...] = (h_scr[...] * _gelu(gate_ref[...])).astype(BF16)


def lru_core(proj, conv_w, conv_b, w_a, b_a, w_i, b_i, lam, conv_init, h_init, layer, tm, R):
    M, W2 = proj.shape
    W = W2 // 2
    hd = w_a.shape[-1]
    tc = max(hd, _pick(W, 1024))
    assert tc % hd == 0 and W % tc == 0
    nc = W // tc
    nseq, cru, _ = conv_init.shape
    tps = (M // tm) // nseq
    kern = functools.partial(_lru_kernel, R=R, cru=cru, tps=tps)
    row_vec = lambda: pl.BlockSpec((None, 1, tc), lambda i, c: (layer, 0, c))
    gate_w = lambda: pl.BlockSpec((None, tc // hd, hd, hd), lambda i, c: (layer, c, 0, 0))
    unfold = lambda s: jnp.swapaxes(s, 1, 2).reshape(nseq, s.shape[2], W)
    hg, c_rows, h_rows = pl.pallas_call(
        kern,
        grid=(M // tm, nc),
        in_specs=[
            pl.BlockSpec((tm, tc), lambda i, c: (i, c)),
            pl.BlockSpec((tm, tc), lambda i, c: (i, nc + c)),
            pl.BlockSpec((None, conv_w.shape[1], tc), lambda i, c: (layer, 0, c)),
            row_vec(), gate_w(), row_vec(), gate_w(), row_vec(), row_vec(),
            pl.BlockSpec((None, cru, tc), lambda i, c: (i // tps, 0, c)),
            pl.BlockSpec((None, SUBLANES, tc), lambda i, c: (i // tps, 0, c)),
        ],
        out_specs=[
            pl.BlockSpec((tm, tc), lambda i, c: (i, c)),
            pl.BlockSpec((None, nc, cru, tc), lambda i, c: (i // tps, 0, 0, 0)),
            pl.BlockSpec((None, nc, SUBLANES, tc), lambda i, c: (i // tps, 0, 0, 0)),
        ],
        out_shape=[
            jax.ShapeDtypeStruct((M, W), BF16),
            jax.ShapeDtypeStruct((nseq, nc, cru, tc), F32),
            jax.ShapeDtypeStruct((nseq, nc, SUBLANES, tc), F32),
        ],
        scratch_shapes=[
            pltpu.VMEM((cru + tm, tc), F32),
            pltpu.VMEM((tm, tc), F32),
            pltpu.VMEM((tm, tc), F32),
            pltpu.VMEM((tm, tc), F32),
            pltpu.VMEM((nc, cru, tc), F32),
            pltpu.VMEM((nc, SUBLANES, tc), F32),
        ],
        compiler_params=_params(2),
        name="lru_core",
    )(proj, proj, conv_w, conv_b, w_a, b_a, w_i, b_i, lam, conv_init, h_init)
    return hg, unfold(c_rows), unfold(h_rows)


def _ffn_kernel(x_ref, g_ref, wg_ref, wv_ref, cwg_ref, cwv_ref, cbg_ref, cbv_ref, wd_ref, ig_ref, iv_ref, *rest,
                R, cr, tps, final):
    fg_ref = rest[0] if final else None
    o_ref, sg_ref, sv_ref, *emit, xn_ref, eg_ref, ev_ref, cg_ref, cv_ref = rest[1:] if final else rest
    i = pl.program_id(0)
    j = pl.program_id(1)
    tm = x_ref.shape[0]

    @pl.when(j == 0)
    def _():
        x = x_ref[...]
        xn_ref[...] = _rms(x, g_ref[...]).astype(BF16)
        o_ref[...] = x

    @pl.when(i % tps == 0)
    def _():
        cg_ref[j] = ig_ref[...]
        cv_ref[j] = iv_ref[...]

    def side(w_ref, cw_ref, cb_ref, e_ref, c_ref, s_ref, emit_ref):
        up = jnp.dot(xn_ref[...], _as_bf16_weight(w_ref, emit_ref), preferred_element_type=F32)
        e_ref[0:cr, :] = c_ref[j]
        e_ref[cr:cr + tm, :] = up
        cw = cw_ref[...]
        conv = (cb_ref[...] + cw[2:3] * up + cw[1:2] * e_ref[cr - R:cr - R + tm, :]
                + cw[0:1] * e_ref[cr - 2 * R:cr - 2 * R + tm, :])
        tail = e_ref[tm:tm + cr, :]
        c_ref[j] = tail
        s_ref[j] = tail
        return conv

    cg = side(wg_ref, cwg_ref, cbg_ref, eg_ref, cg_ref, sg_ref, emit[0:1])
    cv = side(wv_ref, cwv_ref, cbv_ref, ev_ref, cv_ref, sv_ref, emit[1:2])
    act = (_gelu(cg) * cv).astype(BF16)
    o_ref[...] += jnp.dot(act, _as_bf16_weight(wd_ref, emit[2:3]), preferred_element_type=F32)
    if final:
        @pl.when(j == pl.num_programs(1) - 1)
        def _():
            o_ref[...] = _rms(o_ref[...], fg_ref[...])


def conv_ffn(x, gains, w_gate, w_val, conv_w, conv_b, w_down, init, layer, tm, tf, R, final_gain=None):
    M, D = x.shape
    F = w_down.arr.shape[1]
    emitting = w_down.arr.dtype != BF16
    if emitting:
        tf = min(tf, 512)
    nf = F // tf
    nseq, cr, _ = init.shape
    tps = (M // tm) // nseq
    final = final_gain is not None
    kern = functools.partial(_ffn_kernel, R=R, cr=cr, tps=tps, final=final)
    K = conv_w.shape[1]
    halves = lambda mk: [mk(0), mk(nf)]
    state_spec = lambda off: pl.BlockSpec((None, cr, tf), lambda i, j: (i // tps, 0, off + j))
    w_specs, emit_specs, emit_shapes = [], [], []
    for w, col0 in (w_gate, w_val):
        assert col0 % tf == 0 and (w.arr.dtype != BF16) == emitting
        spec, e_spec, e_shape = _weight_specs(w, (D, tf), lambda i, j, off=col0 // tf: (0, off + j))
        w_specs.append(spec)
        emit_specs += [pl.BlockSpec((None, D, tf), lambda i, j: (0, 0, j))] if e_spec else []
        emit_shapes += [jax.ShapeDtypeStruct((1, D, F), BF16)] if e_shape else []
    wd_spec, e_spec, e_shape = _weight_specs(w_down, (tf, D), lambda i, j: (j, 0))
    emit_specs += e_spec
    emit_shapes += e_shape
    assert not emit_specs or M == tm, "a weight tile must be visited once to be emitted"
    once = pl.Buffered(1)
    out, sg, sv, *emitted = pl.pallas_call(
        kern,
        grid=(M // tm, nf),
        in_specs=[
            pl.BlockSpec((tm, D), lambda i, j: (i, 0)),
            pl.BlockSpec((None, 1, D), lambda i, j: (layer, 0, 0)),
            *w_specs,
            *halves(lambda off: pl.BlockSpec((None, K, tf), lambda i, j: (layer, 0, off + j))),
            *halves(lambda off: pl.BlockSpec((None, 1, tf), lambda i, j: (layer, 0, off + j))),
            wd_spec,
            *halves(state_spec),
            *([pl.BlockSpec((1, D), lambda i, j: (0, 0))] if final else []),
        ],
        out_specs=[
            pl.BlockSpec((tm, D), lambda i, j: (i, 0), pipeline_mode=once),
            pl.BlockSpec((None, nf, cr, tf), lambda i, j: (i // tps, 0, 0, 0)),
            pl.BlockSpec((None, nf, cr, tf), lambda i, j: (i // tps, 0, 0, 0)),
            *emit_specs,
        ],
        out_shape=[
            jax.ShapeDtypeStruct((M, D), F32),
            jax.ShapeDtypeStruct((nseq, nf, cr, tf), F32),
            jax.ShapeDtypeStruct((nseq, nf, cr, tf), F32),
            *emit_shapes,
        ],
        scratch_shapes=[
            pltpu.VMEM((tm, D), BF16),
            pltpu.VMEM((cr + tm, tf), F32),
            pltpu.VMEM((cr + tm, tf), F32),
            pltpu.VMEM((nf, cr, tf), F32),
            pltpu.VMEM((nf, cr, tf), F32),
        ],
        compiler_params=_params(2),
        name="conv_ffn",
    )(x, gains, w_gate[0].arr, w_val[0].arr, conv_w, conv_w, conv_b, conv_b, w_down.arr, init, init,
      *([final_gain] if final else []))
    unfold = lambda s: jnp.swapaxes(s, 1, 2).reshape(nseq, cr, F)
    return out, jnp.concatenate([unfold(sg), unfold(sv)], axis=-1), (tuple(emitted) if emitted else None)


ATTN_SUPER = max(w for w, _ in ATTN_GROUPS)
ATTN_BATCH = 8
ATTN_EINSHAPE_DIL = SUBLANES


def _attn_prompt_kernel(sl_ref, q0, k0, v0, q1, k1, v1, q2, k2, v2, o_ref,
                        kc0, vc0, kc1, vc1, kc2, vc2, o_scr, l_scr):
    n = pl.program_id(2)
    SB = o_ref.shape[0]
    blk = ATTN_BLK
    row = lax.broadcasted_iota(jnp.int32, (blk, 2 * blk), 0)
    col = lax.broadcasted_iota(jnp.int32, (blk, 2 * blk), 1)
    steps = row + blk - col
    band = jnp.logical_and(steps >= 0, steps <= blk)
    band_cur = jnp.logical_and(band, col >= blk)
    steps_f = steps.astype(F32)
    ones_v = jnp.ones((2 * blk, HEAD_DIM), BF16)
    inv_sqrt = 1.0 / math.sqrt(HEAD_DIM)
    nt = (((1,), (1,)), ((), ()))
    groups = zip(ATTN_GROUPS, (q0, q1, q2), (k0, k1, k2), (v0, v1, v2), (kc0, kc1, kc2), (vc0, vc1, vc2))
    for g, ((win, dil), q_ref, k_ref, v_ref, kcar, vcar) in enumerate(groups):
        @pl.when(n == 0)
        def _():
            kcar[...] = jnp.zeros(kcar.shape, kcar.dtype)
            vcar[...] = jnp.zeros(vcar.shape, vcar.dtype)

        alibi = (sl_ref[g:g + 1, :] * float(-dil)) * steps_f
        bias_full = jnp.where(band, alibi, NEG)
        bias_head = jnp.where(n > 0, bias_full, jnp.where(band_cur, alibi, NEG))

        def rows(start):
            return pl.ds(start, blk, stride=dil) if dil > 1 else pl.ds(start, blk)

        transposed = dil >= ATTN_EINSHAPE_DIL
        if transposed:
            assert win == SB
            split = lambda ref: pltpu.einshape("lrd->rld", ref[...].astype(BF16).reshape(blk, dil, HEAD_DIM))
            q_ph, k_ph, v_ph = split(q_ref), split(k_ref), split(v_ref)
            k_prev, v_prev = kcar[...], vcar[...]
        o_parts, l_parts = [], []
        n_blocks = SB // blk
        for b0 in range(0, n_blocks, ATTN_BATCH):
            infos = []
            for bidx in range(b0, b0 + ATTN_BATCH):
                j, r = divmod(bidx, dil)
                infos.append((j, r, j * win + r))
            scores = []
            for j, r, p0 in infos:
                if transposed:
                    q, kp, kc = q_ph[r], k_prev[r], k_ph[r]
                else:
                    q = q_ref[rows(p0), :].astype(BF16)
                    kp = (kcar[rows(r), :] if j == 0 else k_ref[rows(p0 - win), :]).astype(BF16)
                    kc = k_ref[rows(p0), :].astype(BF16)
                scores.append(lax.dot_general(q, jnp.concatenate([kp, kc], axis=0), nt, preferred_element_type=F32))
            probs = []
            for (j, r, p0), s in zip(infos, scores):
                s = s * inv_sqrt + (bias_head if j == 0 else bias_full)
                m = jnp.max(jnp.maximum(s[:, :blk], s[:, blk:]), axis=-1, keepdims=True)
                probs.append((jnp.exp(s - m).astype(BF16), m))
            for (j, r, p0), (e, m) in zip(infos, probs):
                if transposed:
                    vp, vc = v_prev[r], v_ph[r]
                else:
                    vp = (vcar[rows(r), :] if j == 0 else v_ref[rows(p0 - win), :]).astype(BF16)
                    vc = v_ref[rows(p0), :].astype(BF16)
                vcat = jnp.concatenate([vp, vc], axis=0)
                res = jnp.dot(e, jnp.concatenate([vcat, ones_v], axis=1), preferred_element_type=F32)
                den = res[:, HEAD_DIM:]
                o_blk, l_blk = res[:, :HEAD_DIM] / den, m + jnp.log(den)
                if transposed:
                    o_parts.append(o_blk)
                    l_parts.append(l_blk)
                else:
                    o_scr[g, rows(p0), :] = o_blk
                    l_scr[g, rows(p0), :] = l_blk
        if transposed:
            merge = lambda parts: pltpu.einshape("rld->lrd", jnp.stack(parts, axis=0)).reshape(SB, HEAD_DIM)
            o_scr[g] = merge(o_parts)
            l_scr[g] = merge(l_parts)
            kcar[...] = k_ph
            vcar[...] = v_ph
        else:
            kcar[...] = k_ref[SB - win:SB, :]
            vcar[...] = v_ref[SB - win:SB, :]

    chunk = 2 * blk
    for c0 in range(0, SB, chunk):
        ls = [l_scr[g, c0:c0 + chunk, :] for g in range(N_GROUPS)]
        mm = jnp.maximum(jnp.maximum(ls[0], ls[1]), ls[2])
        es = [jnp.exp(l - mm) for l in ls]
        acc = es[0] * o_scr[0, c0:c0 + chunk, :]
        for g in range(1, N_GROUPS):
            acc = acc + es[g] * o_scr[g, c0:c0 + chunk, :]
        o_ref[c0:c0 + chunk, :] = (acc / (es[0] + es[1] + es[2])).astype(BF16)


def attn_prompt(qkv_slabs, B, S):
    n_slabs, M, _ = qkv_slabs.shape
    SB = ATTN_SUPER
    assert n_slabs == 3 * N_GROUPS * GROUP_HEADS and S % SB == 0 and M == B * S
    assert all(w // d == ATTN_BLK for w, d in ATTN_GROUPS) and (SB // ATTN_BLK) % ATTN_BATCH == 0
    nsb = S // SB
    slopes = jnp.asarray(_alibi_slopes(), F32).T
    slopes = jnp.broadcast_to(slopes[:, :, None], (GROUP_HEADS, N_GROUPS, 2 * ATTN_BLK))

    def slab(g, comp):
        base = (g * 3 + comp) * GROUP_HEADS
        return pl.BlockSpec((None, SB, HEAD_DIM), lambda b, h, n: (base + h, b * nsb + n, 0))

    carry = [pltpu.VMEM((d, w // d, HEAD_DIM), BF16) if d >= ATTN_EINSHAPE_DIL else pltpu.VMEM((w, HEAD_DIM), F32)
             for w, d in ATTN_GROUPS for _kv in range(2)]
    return pl.pallas_call(
        _attn_prompt_kernel,
        grid=(B, GROUP_HEADS, nsb),
        in_specs=[pl.BlockSpec((None, N_GROUPS, 2 * ATTN_BLK), lambda b, h, n: (h, 0, 0))]
        + [slab(g, comp) for g in range(N_GROUPS) for comp in range(3)],
        out_specs=pl.BlockSpec((None, SB, HEAD_DIM), lambda b, h, n: (h, b * nsb + n, 0)),
        out_shape=jax.ShapeDtypeStruct((GROUP_HEADS, M, HEAD_DIM), BF16),
        scratch_shapes=carry + [pltpu.VMEM((N_GROUPS, SB, HEAD_DIM), F32), pltpu.VMEM((N_GROUPS, SB, HEAD_DIM), F32)],
        compiler_params=_params(3),
        name="attn_prompt",
    )(slopes, *([qkv_slabs] * (3 * N_GROUPS)))


def _attn_sample_kernel(sl_ref, qkv_ref, c0_ref, c1_ref, c2_ref, o_ref):
    T = qkv_ref.shape[0]
    H = GROUP_HEADS
    caches = (c0_ref, c1_ref, c2_ref)
    rows = c0_ref.shape[0]
    l_idx = lax.broadcasted_iota(jnp.int32, (rows, H, 1), 0)
    l_f = l_idx.astype(F32)
    inv_sqrt = 1.0 / math.sqrt(HEAD_DIM)
    for t in range(T):
        outs, lses = [], []
        for g, (win, dil) in enumerate(ATTN_GROUPS):
            base = g * 3 * H
            slope = sl_ref[g][:, 0:1]
            q = qkv_ref[t, base:base + H, :]
            ph = 0 if dil == 1 else t
            kc = caches[g][:, ph * 2 * H:ph * 2 * H + H, :]
            vc = caches[g][:, ph * 2 * H + H:(ph + 1) * 2 * H, :]
            sc = jnp.sum(kc * q[None], axis=-1, keepdims=True) * inv_sqrt
            if dil == 1:
                sc = jnp.where(l_idx >= t, sc - slope[None] * (float(rows + t) - l_f), NEG)
                new_ts = list(range(t + 1))
            else:
                sc = sc - (slope[None] * float(dil)) * (float(rows) - l_f)
                new_ts = [t]
            m = jnp.max(sc, axis=0)
            s_new = []
            for t2 in new_ts:
                k2 = qkv_ref[t2, base + H:base + 2 * H, :]
                s2 = jnp.sum(q * k2, axis=-1, keepdims=True) * inv_sqrt - slope * float((t - t2) * dil)
                s_new.append(s2)
                m = jnp.maximum(m, s2)
            ec = jnp.exp(sc - m[None])
            den = jnp.sum(ec, axis=0)
            acc = jnp.sum(ec * vc, axis=0)
            for t2, s2 in zip(new_ts, s_new):
                v2 = qkv_ref[t2, base + 2 * H:base + 3 * H, :]
                e2 = jnp.exp(s2 - m)
                den = den + e2
                acc = acc + e2 * v2
            outs.append(acc / den)
            lses.append(m + jnp.log(den))
        mm = jnp.maximum(jnp.maximum(lses[0], lses[1]), lses[2])
        es = [jnp.exp(l - mm) for l in lses]
        o_ref[t] = (es[0] * outs[0] + es[1] * outs[1] + es[2] * outs[2]) / (es[0] + es[1] + es[2])


def attn_sample(qkv, caches, layer, T, Bs):
    H = GROUP_HEADS
    n_rows = qkv.shape[-1] // HEAD_DIM
    views, specs = [], []
    for (win, dil), c in zip(ATTN_GROUPS, caches):
        assert c.shape[1] == Bs and c.shape[2] == win and win // dil == ATTN_BLK and (dil == 1 or T <= dil)
        phases = min(dil, -(-T // 4) * 4)
        views.append(c.reshape(c.shape[0], Bs, win // dil, dil * 2 * H, HEAD_DIM))
        specs.append(pl.BlockSpec((None, None, win // dil, phases * 2 * H, HEAD_DIM), lambda b: (layer, b, 0, 0, 0)))
    slopes = jnp.broadcast_to(jnp.asarray(_alibi_slopes(), F32)[:, :, None], (N_GROUPS, H, HEAD_DIM))
    q_rows = jnp.swapaxes(qkv.reshape(T, Bs, n_rows, HEAD_DIM), 0, 1)
    out = pl.pallas_call(
        _attn_sample_kernel,
        grid=(Bs,),
        in_specs=[pl.BlockSpec((N_GROUPS, H, HEAD_DIM), lambda b: (0, 0, 0)),
                  pl.BlockSpec((None, T, n_rows, HEAD_DIM), lambda b: (b, 0, 0, 0)), *specs],
        out_specs=pl.BlockSpec((None, T, H, HEAD_DIM), lambda b: (b, 0, 0, 0)),
        out_shape=jax.ShapeDtypeStruct((Bs, T, H, HEAD_DIM), F32),
        compiler_params=_params(1),
        name="attn_sample",
    )(slopes, q_rows, *views)
    return jnp.swapaxes(out, 0, 1).reshape(T * Bs, ATTN_WIDTH)


class Tiles(NamedTuple):
    seq: int
    mm: int
    ffn: int
    tf: int


def _run_trunk(x, nseq, R, tiles, lru_h, lru_conv, kv_caches, ffn_conv, P, W, dims):
    depth = P["norm_mix"].shape[0]
    new_h, new_lconv, new_fconv, qkvs = [], [], [], []
    bf16_w = {name: [] for name in W}
    for layer in range(depth):
        j = layer // 2
        if layer % 2 == 0:
            proj, wb = norm_matmul(x, P["norm_mix"], layer, W["lru_w_in"][j], P["lru_b_in"], j, tm=tiles.mm)
            bf16_w["lru_w_in"].append(wb)
            hg, c_rows, h_rows = lru_core(proj, P["lru_conv_w"], P["lru_conv_b"], P["lru_w_a"], P["lru_b_a"],
                                          P["lru_w_i"], P["lru_b_i"], P["lru_lambda"], lru_conv[j], lru_h[j],
                                          layer=j, tm=tiles.seq, R=R)
            x, wb = matmul_res(hg, W["lru_w_out"][j], P["lru_b_out"], j, x, tm=tiles.mm)
            bf16_w["lru_w_out"].append(wb)
            new_h.append(h_rows)
            new_lconv.append(c_rows)
        else:
            prompt = kv_caches is None
            qkv, wb = norm_matmul(x, P["norm_mix"], layer, W["attn_w_qkv"][j], P["attn_b_zero"], j, tm=tiles.mm,
                                  slab_out=prompt)
            bf16_w["attn_w_qkv"].append(wb)
            qkvs.append(qkv)
            if prompt:
                o = attn_prompt(qkv, *dims)
            else:
                o = attn_sample(qkv, kv_caches, layer=j, T=dims[0], Bs=dims[1])
            x, wb = matmul_res(o, W["attn_w_o"][j], P["attn_bo_zero"], j, x, tm=tiles.mm)
            bf16_w["attn_w_o"].append(wb)
        x, f_rows, wbs = conv_ffn(x, P["norm_ffn"], W["ffn_w_gate"][layer], W["ffn_w_val"][layer], P["ffn_conv_w"],
                                  P["ffn_conv_b"], W["ffn_w_down"][layer], ffn_conv[layer], layer=layer,
                                  tm=tiles.ffn, tf=tiles.tf, R=R,
                                  final_gain=P["norm_final"] if layer == depth - 1 else None)
        for name, wb in zip(("ffn_w_gate", "ffn_w_val", "ffn_w_down"), wbs or (None,) * 3):
            bf16_w[name].append(wb)
        new_fconv.append(f_rows)
    return x, new_h, new_lconv, qkvs, new_fconv, bf16_w


def kernel(x_prompt, x_sample, cache_kv_w128, cache_kv_w512, cache_kv_w2048, state_lru_h, state_lru_conv, state_ffn_conv, norm_mix, norm_ffn, norm_final, lru_w_in, lru_b_in, lru_conv_w, lru_conv_b, lru_w_a, lru_b_a, lru_w_i, lru_b_i, lru_lambda, lru_w_out, lru_b_out, attn_w_qkv, attn_w_o, ffn_w_up, ffn_conv_w, ffn_conv_b, ffn_w_down):
    B, S, D = x_prompt.shape
    Bs, T, _ = x_sample.shape
    depth = norm_mix.shape[0]
    n_lru, W = lru_lambda.shape
    n_attn = attn_w_qkv.shape[0]
    F2 = ffn_w_up.shape[-1]
    assert Bs == SUBLANES, "the sample group is laid out time-major with one sublane per sequence"
    row3 = lambda a: a.reshape(a.shape[0], 1, a.shape[-1])
    P = {
        "norm_mix": row3(norm_mix), "norm_ffn": row3(norm_ffn), "norm_final": norm_final.reshape(1, D),
        "lru_b_in": row3(lru_b_in), "lru_conv_w": lru_conv_w, "lru_conv_b": row3(lru_conv_b),
        "lru_w_a": lru_w_a, "lru_b_a": row3(lru_b_a), "lru_w_i": lru_w_i, "lru_b_i": row3(lru_b_i),
        "lru_lambda": row3(lru_lambda), "lru_b_out": row3(lru_b_out),
        "attn_b_zero": jnp.zeros((n_attn, 1, attn_w_qkv.shape[-1]), F32),
        "attn_bo_zero": jnp.zeros((n_attn, 1, D), F32),
        "ffn_conv_w": ffn_conv_w, "ffn_conv_b": row3(ffn_conv_b),
    }
    stack = lambda w: [Weight(w, i) for i in range(w.shape[0])]
    W_f32 = {
        "lru_w_in": stack(lru_w_in), "lru_w_out": stack(lru_w_out),
        "attn_w_qkv": stack(attn_w_qkv), "attn_w_o": stack(attn_w_o),
        "ffn_w_gate": [(w, 0) for w in stack(ffn_w_up)], "ffn_w_val": [(w, F2 // 2) for w in stack(ffn_w_up)],
        "ffn_w_down": stack(ffn_w_down),
    }
    kc = lru_conv_w.shape[1] - 1
    kf = ffn_conv_w.shape[1] - 1

    tmaj = lambda a: jnp.swapaxes(a, 0, 1).reshape(1, a.shape[1] * a.shape[0], a.shape[-1])
    y_s, h_s, lc_s, qkv_s, fc_s, bf16_w = _run_trunk(
        jnp.swapaxes(x_sample, 0, 1).reshape(T * Bs, D), 1, Bs,
        Tiles(seq=T * Bs, mm=T * Bs, ffn=T * Bs, tf=_pick(F2 // 2, 1024)),
        [state_lru_h[j].reshape(1, Bs, W) for j in range(n_lru)],
        [tmaj(state_lru_conv[j]) for j in range(n_lru)],
        (cache_kv_w128, cache_kv_w512, cache_kv_w2048),
        [tmaj(state_ffn_conv[l]) for l in range(depth)], P, W_f32, (T, Bs))

    W_bf16 = {name: [(Weight(w, 0), 0) if name in ("ffn_w_gate", "ffn_w_val") else Weight(w, 0) for w in ws]
              for name, ws in bf16_w.items()}

    row_tile = lambda cap: cap if S % cap == 0 else S
    tiles_p = Tiles(seq=row_tile(512), mm=row_tile(1024), ffn=row_tile(1024), tf=_pick(F2 // 2, 512))
    zeros = lambda n, w: [jnp.zeros((B, SUBLANES, w), F32)] * n
    y_p, h_p, lc_p, qkv_p, fc_p, _ = _run_trunk(
        x_prompt.reshape(B * S, D), B, 1, tiles_p, zeros(n_lru, W), zeros(n_lru, W), None, zeros(depth, F2), P, W_bf16, (B, S))

    bmaj = lambda a, k: jnp.swapaxes(a.reshape(-1, Bs, a.shape[-1])[-k:], 0, 1)
    kv_p, kv_s = [], []
    for g, (win, dil) in enumerate(ATTN_GROUPS):
        lo = (g * 3 + 1) * ATTN_WIDTH
        keep = min(win, S)
        kv_p.append(jnp.stack([jnp.transpose(
            q.reshape(N_GROUPS, 3, GROUP_HEADS, B, S, HEAD_DIM)[g, 1:3, :, :, S - keep:, :], (2, 3, 0, 1, 4))
            for q in qkv_p], axis=0))
        kv_s.append(jnp.stack([jnp.swapaxes(q.reshape(T, Bs, -1), 0, 1)[:, :, lo:lo + 2 * ATTN_WIDTH]
                               .reshape(Bs, T, 2, GROUP_HEADS, HEAD_DIM) for q in qkv_s], axis=0))
    return (
        y_p.reshape(B, S, D),
        jnp.swapaxes(y_s.reshape(T, Bs, D), 0, 1),
        kv_p[0], kv_p[1], kv_p[2],
        jnp.stack([h[:, SUBLANES - 1] for h in h_p], axis=0),
        jnp.stack([c[:, SUBLANES - kc:] for c in lc_p], axis=0),
        jnp.stack([f[:, SUBLANES - kf:] for f in fc_p], axis=0),
        kv_s[0], kv_s[1], kv_s[2],
        jnp.stack([h[0] for h in h_s], axis=0),
        jnp.stack([bmaj(c[0], kc) for c in lc_s], axis=0),
        jnp.stack([bmaj(f[0], kf) for f in fc_s], axis=0),
    )
```

```python
import functools
import math
from typing import NamedTuple

import jax
import jax.numpy as jnp
from jax import lax
from jax.experimental import pallas as pl
from jax.experimental.pallas import tpu as pltpu

EPS = 1e-6
NEG = -1e30
LRU_C = 8.0
ATTN_GROUPS = ((128, 1), (512, 4), (2048, 16))
N_GROUPS = len(ATTN_GROUPS)
GROUP_HEADS = 8
HEAD_DIM = 128
ATTN_WIDTH = GROUP_HEADS * HEAD_DIM
ATTN_BLK = 128
SUBLANES = 8
LANES = 128
VMEM_LIMIT = 56 * 1024 * 1024

F32 = jnp.float32
F32_TINY = 1.1754944e-38
BF16 = jnp.bfloat16


def _alibi_slopes():
    n = N_GROUPS * GROUP_HEADS
    return [[2.0 ** (-8.0 * (g * GROUP_HEADS + h + 1) / n) for h in range(GROUP_HEADS)] for g in range(N_GROUPS)]


def _gelu(x):
    c = math.sqrt(2.0 / math.pi)
    return x * (0.5 * (1.0 + jnp.tanh(c * (x + 0.044715 * (x * x * x)))))


def _rms(x, g):
    ms = jnp.mean(x * x, axis=-1, keepdims=True)
    return x * lax.rsqrt(ms + EPS) * g


def _pick(n, cap):
    best = None
    for t in range(LANES, min(n, cap) + 1, LANES):
        if n % t == 0:
            best = t
    assert best is not None, (n, cap)
    return best


def _params(n_axes):
    return pltpu.CompilerParams(dimension_semantics=("arbitrary",) * n_axes, vmem_limit_bytes=VMEM_LIMIT)


def _as_bf16_weight(w_ref, emit_refs):
    w = w_ref[...]
    if w.dtype != BF16:
        w = w.astype(BF16)
        emit_refs[0][...] = w
    return w


def _norm_matmul_kernel(x_ref, g_ref, w_ref, b_ref, o_ref, *rest):
    xn_ref = rest[-1]

    @pl.when(pl.program_id(1) == 0)
    def _():
        xn_ref[...] = _rms(x_ref[...], g_ref[...]).astype(BF16)

    acc = jnp.dot(xn_ref[...], _as_bf16_weight(w_ref, rest), preferred_element_type=F32) + b_ref[...]
    if len(o_ref.shape) == 2:
        o_ref[...] = acc
    else:
        for c in range(o_ref.shape[0]):
            o_ref[c] = acc[:, c * LANES:(c + 1) * LANES]


class Weight(NamedTuple):
    arr: jax.Array
    layer: int


def _weight_specs(w, block, index):
    spec = pl.BlockSpec((None, *block), lambda *ij: (w.layer, *index(*ij)))
    if w.arr.dtype == BF16:
        return spec, [], []
    return (spec, [pl.BlockSpec((None, *block), lambda *ij: (0, *index(*ij)))],
            [jax.ShapeDtypeStruct((1, *w.arr.shape[1:]), BF16)])


def _tile_cap(tm, w):
    return 1024 if tm > 256 or w.arr.dtype != BF16 else 2048


def norm_matmul(x, gains, g_layer, w, bias, b_layer, tm, slab_out=False):
    M, D = x.shape
    N = w.arr.shape[-1]
    tn = _pick(N, _tile_cap(tm, w))
    if slab_out:
        out_spec = pl.BlockSpec((tn // LANES, tm, LANES), lambda i, j: (j, i, 0))
        out_shape = jax.ShapeDtypeStruct((N // LANES, M, LANES), F32)
    else:
        out_spec = pl.BlockSpec((tm, tn), lambda i, j: (i, j))
        out_shape = jax.ShapeDtypeStruct((M, N), F32)
    w_spec, emit_specs, emit_shapes = _weight_specs(w, (D, tn), lambda i, j: (0, j))
    assert not emit_specs or M == tm, "a weight tile must be visited once to be emitted"
    out, *emitted = pl.pallas_call(
        _norm_matmul_kernel,
        grid=(M // tm, N // tn),
        in_specs=[
            pl.BlockSpec((tm, D), lambda i, j: (i, 0)),
            pl.BlockSpec((None, 1, D), lambda i, j: (g_layer, 0, 0)),
            w_spec,
            pl.BlockSpec((None, 1, tn), lambda i, j: (b_layer, 0, j)),
        ],
        out_specs=[out_spec, *emit_specs],
        out_shape=[out_shape, *emit_shapes],
        scratch_shapes=[pltpu.VMEM((tm, D), BF16)],
        compiler_params=_params(2),
        name="norm_matmul",
    )(x, gains, w.arr, bias)
    return out, (emitted[0] if emitted else None)


def _matmul_res_kernel(a_ref, w_ref, b_ref, r_ref, o_ref, *emit):
    if len(a_ref.shape) == 2:
        a = a_ref[...]
    else:
        a = jnp.concatenate([a_ref[c] for c in range(a_ref.shape[0])], axis=1)
    o_ref[...] = r_ref[...] + b_ref[...] + jnp.dot(a.astype(BF16), _as_bf16_weight(w_ref, emit),
                                                   preferred_element_type=F32)


def matmul_res(a, w, bias, b_layer, res, tm):
    M, N = res.shape
    K = w.arr.shape[1]
    tn = _pick(N, _tile_cap(tm, w))
    if a.ndim == 2:
        a_spec = pl.BlockSpec((tm, K), lambda i, j: (i, 0))
    else:
        a_spec = pl.BlockSpec((K // LANES, tm, LANES), lambda i, j: (0, i, 0))
    w_spec, emit_specs, emit_shapes = _weight_specs(w, (K, tn), lambda i, j: (0, j))
    assert not emit_specs or M == tm, "a weight tile must be visited once to be emitted"
    out, *emitted = pl.pallas_call(
        _matmul_res_kernel,
        grid=(M // tm, N // tn),
        in_specs=[
            a_spec,
            w_spec,
            pl.BlockSpec((None, 1, tn), lambda i, j: (b_layer, 0, j)),
            pl.BlockSpec((tm, tn), lambda i, j: (i, j)),
        ],
        out_specs=[pl.BlockSpec((tm, tn), lambda i, j: (i, j)), *emit_specs],
        out_shape=[jax.ShapeDtypeStruct((M, N), F32), *emit_shapes],
        compiler_params=_params(2),
        name="matmul_res",
    )(a, w.arr, bias, res)
    return out, (emitted[0] if emitted else None)


def _lru_kernel(gate_ref, u_ref, cw_ref, cb_ref, wa_ref, ba_ref, wi_ref, bi_ref, lam_ref, ci_ref, hi_ref,
                o_ref, cs_ref, hs_ref,
                e_ref, a_scr, b_scr, h_scr, cc_ref, hc_ref, *, R, cru, tps):
    i = pl.program_id(0)
    c = pl.program_id(1)
    tm, tc = u_ref.shape

    @pl.when(i % tps == 0)
    def _():
        cc_ref[c] = ci_ref[...]
        hc_ref[c] = hi_ref[...]

    u = u_ref[...]
    e_ref[0:cru, :] = cc_ref[c]
    e_ref[cru:cru + tm, :] = u
    cw = cw_ref[...]
    uc = (cb_ref[...] + cw[3:4] * u + cw[2:3] * e_ref[cru - R:cru - R + tm, :]
          + cw[1:2] * e_ref[cru - 2 * R:cru - 2 * R + tm, :] + cw[0:1] * e_ref[cru - 3 * R:cru - 3 * R + tm, :])
    tail = e_ref[tm:tm + cru, :]
    cc_ref[c] = tail
    cs_ref[c] = tail

    hd = wa_ref.shape[-1]
    ucb = uc.astype(BF16)
    ra, ri = [], []
    for hh in range(tc // hd):
        ub = ucb[:, hh * hd:(hh + 1) * hd]
        ra.append(jnp.dot(ub, wa_ref[hh].astype(BF16), preferred_element_type=F32))
        ri.append(jnp.dot(ub, wi_ref[hh].astype(BF16), preferred_element_type=F32))
    r = jax.nn.sigmoid(jnp.concatenate(ra, axis=1) + ba_ref[...])
    ig = jax.nn.sigmoid(jnp.concatenate(ri, axis=1) + bi_ref[...])
    nlam = -lam_ref[...]
    softplus = jnp.maximum(nlam, 0.0) + jnp.log1p(jnp.exp(-jnp.abs(nlam)))
    log_a = (-LRU_C) * r * softplus
    a = jnp.exp(log_a)
    th = jnp.tanh(log_a)
    z = (-2.0 * th) / (1.0 - th)
    bx = (z * lax.rsqrt(jnp.maximum(z, F32_TINY))) * (ig * uc)

    if R == 1:
        A = a.reshape(tm // SUBLANES, SUBLANES, tc)
        B = bx.reshape(tm // SUBLANES, SUBLANES, tc)
        row = lax.broadcasted_iota(jnp.int32, A.shape, 1)
        s = 1
        while s < SUBLANES:
            m = row >= s
            B = jnp.where(m, A * pltpu.roll(B, s, axis=1) + B, B)
            A = jnp.where(m, A * pltpu.roll(A, s, axis=1), A)
            s *= 2
        a_scr[...] = A.reshape(tm, tc)
        b_scr[...] = B.reshape(tm, tc)
        h0 = hc_ref[c][SUBLANES - 1:SUBLANES, :]
    else:
        assert R == SUBLANES
        a_scr[...] = a
        b_scr[...] = bx
        h0 = hc_ref[c]

    def body(g, h):
        r0 = pl.multiple_of(g * SUBLANES, SUBLANES)
        hg = b_scr[pl.ds(r0, SUBLANES), :] + a_scr[pl.ds(r0, SUBLANES), :] * h
        h_scr[pl.ds(r0, SUBLANES), :] = hg
        return hg[SUBLANES - 1:SUBLANES, :] if R == 1 else hg

    n_groups = tm // SUBLANES
    lax.fori_loop(0, n_groups, body, h0, unroll=min(n_groups, 8))
    h_tail = h_scr[tm - SUBLANES:tm, :]
    hc_ref[c] = h_tail
    hs_ref[c] = h_tail
    o_ref[...] = (h_scr[...] * _gelu(gate_ref[...])).astype(BF16)


def lru_core(proj, conv_w, conv_b, w_a, b_a, w_i, b_i, lam, conv_init, h_init, layer, tm, R):
    M, W2 = proj.shape
    W = W2 // 2
    hd = w_a.shape[-1]
    tc = max(hd, _pick(W, 1024))
    assert tc % hd == 0 and W % tc == 0
    nc = W // tc
    nseq, cru, _ = conv_init.shape
    tps = (M // tm) // nseq
    kern = functools.partial(_lru_kernel, R=R, cru=cru, tps=tps)
    row_vec = lambda: pl.BlockSpec((None, 1, tc), lambda i, c: (layer, 0, c))
    gate_w = lambda: pl.BlockSpec((None, tc // hd, hd, hd), lambda i, c: (layer, c, 0, 0))
    unfold = lambda s: jnp.swapaxes(s, 1, 2).reshape(nseq, s.shape[2], W)
    hg, c_rows, h_rows = pl.pallas_call(
        kern,
        grid=(M // tm, nc),
        in_specs=[
            pl.BlockSpec((tm, tc), lambda i, c: (i, c)),
            pl.BlockSpec((tm, tc), lambda i, c: (i, nc + c)),
            pl.BlockSpec((None, conv_w.shape[1], tc), lambda i, c: (layer, 0, c)),
            row_vec(), gate_w(), row_vec(), gate_w(), row_vec(), row_vec(),
            pl.BlockSpec((None, cru, tc), lambda i, c: (i // tps, 0, c)),
            pl.BlockSpec((None, SUBLANES, tc), lambda i, c: (i // tps, 0, c)),
        ],
        out_specs=[
            pl.BlockSpec((tm, tc), lambda i, c: (i, c)),
            pl.BlockSpec((None, nc, cru, tc), lambda i, c: (i // tps, 0, 0, 0)),
            pl.BlockSpec((None, nc, SUBLANES, tc), lambda i, c: (i // tps, 0, 0, 0)),
        ],
        out_shape=[
            jax.ShapeDtypeStruct((M, W), BF16),
            jax.ShapeDtypeStruct((nseq, nc, cru, tc), F32),
            jax.ShapeDtypeStruct((nseq, nc, SUBLANES, tc), F32),
        ],
        scratch_shapes=[
            pltpu.VMEM((cru + tm, tc), F32),
            pltpu.VMEM((tm, tc), F32),
            pltpu.VMEM((tm, tc), F32),
            pltpu.VMEM((tm, tc), F32),
            pltpu.VMEM((nc, cru, tc), F32),
            pltpu.VMEM((nc, SUBLANES, tc), F32),
        ],
        compiler_params=_params(2),
        name="lru_core",
    )(proj, proj, conv_w, conv_b, w_a, b_a, w_i, b_i, lam, conv_init, h_init)
    return hg, unfold(c_rows), unfold(h_rows)


def _ffn_kernel(x_ref, g_ref, wg_ref, wv_ref, cwg_ref, cwv_ref, cbg_ref, cbv_ref, wd_ref, ig_ref, iv_ref, *rest,
                R, cr, tps, final):
    fg_ref = rest[0] if final else None
    o_ref, sg_ref, sv_ref, *emit, xn_ref, eg_ref, ev_ref, cg_ref, cv_ref = rest[1:] if final else rest
    i = pl.program_id(0)
    j = pl.program_id(1)
    tm = x_ref.shape[0]

    @pl.when(j == 0)
    def _():
        x = x_ref[...]
        xn_ref[...] = _rms(x, g_ref[...]).astype(BF16)
        o_ref[...] = x

    @pl.when(i % tps == 0)
    def _():
        cg_ref[j] = ig_ref[...]
        cv_ref[j] = iv_ref[...]

    def side(w_ref, cw_ref, cb_ref, e_ref, c_ref, s_ref, emit_ref):
        up = jnp.dot(xn_ref[...], _as_bf16_weight(w_ref, emit_ref), preferred_element_type=F32)
        e_ref[0:cr, :] = c_ref[j]
        e_ref[cr:cr + tm, :] = up
        cw = cw_ref[...]
        conv = (cb_ref[...] + cw[2:3] * up + cw[1:2] * e_ref[cr - R:cr - R + tm, :]
                + cw[0:1] * e_ref[cr - 2 * R:cr - 2 * R + tm, :])
        tail = e_ref[tm:tm + cr, :]
        c_ref[j] = tail
        s_ref[j] = tail
        return conv

    cg = side(wg_ref, cwg_ref, cbg_ref, eg_ref, cg_ref, sg_ref, emit[0:1])
    cv = side(wv_ref, cwv_ref, cbv_ref, ev_ref, cv_ref, sv_ref, emit[1:2])
    act = (_gelu(cg) * cv).astype(BF16)
    o_ref[...] += jnp.dot(act, _as_bf16_weight(wd_ref, emit[2:3]), preferred_element_type=F32)
    if final:
        @pl.when(j == pl.num_programs(1) - 1)
        def _():
            o_ref[...] = _rms(o_ref[...], fg_ref[...])


def conv_ffn(x, gains, w_gate, w_val, conv_w, conv_b, w_down, init, layer, tm, tf, R, final_gain=None):
    M, D = x.shape
    F = w_down.arr.shape[1]
    emitting = w_down.arr.dtype != BF16
    if emitting:
        tf = min(tf, 512)
    nf = F // tf
    nseq, cr, _ = init.shape
    tps = (M // tm) // nseq
    final = final_gain is not None
    kern = functools.partial(_ffn_kernel, R=R, cr=cr, tps=tps, final=final)
    K = conv_w.shape[1]
    halves = lambda mk: [mk(0), mk(nf)]
    state_spec = lambda off: pl.BlockSpec((None, cr, tf), lambda i, j: (i // tps, 0, off + j))
    w_specs, emit_specs, emit_shapes = [], [], []
    for w, col0 in (w_gate, w_val):
        assert col0 % tf == 0 and (w.arr.dtype != BF16) == emitting
        spec, e_spec, e_shape = _weight_specs(w, (D, tf), lambda i, j, off=col0 // tf: (0, off + j))
        w_specs.append(spec)
        emit_specs += [pl.BlockSpec((None, D, tf), lambda i, j: (0, 0, j))] if e_spec else []
        emit_shapes += [jax.ShapeDtypeStruct((1, D, F), BF16)] if e_shape else []
    wd_spec, e_spec, e_shape = _weight_specs(w_down, (tf, D), lambda i, j: (j, 0))
    emit_specs += e_spec
    emit_shapes += e_shape
    assert not emit_specs or M == tm, "a weight tile must be visited once to be emitted"
    once = pl.Buffered(1)
    out, sg, sv, *emitted = pl.pallas_call(
        kern,
        grid=(M // tm, nf),
        in_specs=[
            pl.BlockSpec((tm, D), lambda i, j: (i, 0)),
            pl.BlockSpec((None, 1, D), lambda i, j: (layer, 0, 0)),
            *w_specs,
            *halves(lambda off: pl.BlockSpec((None, K, tf), lambda i, j: (layer, 0, off + j))),
            *halves(lambda off: pl.BlockSpec((None, 1, tf), lambda i, j: (layer, 0, off + j))),
            wd_spec,
            *halves(state_spec),
            *([pl.BlockSpec((1, D), lambda i, j: (0, 0))] if final else []),
        ],
        out_specs=[
            pl.BlockSpec((tm, D), lambda i, j: (i, 0), pipeline_mode=once),
            pl.BlockSpec((None, nf, cr, tf), lambda i, j: (i // tps, 0, 0, 0)),
            pl.BlockSpec((None, nf, cr, tf), lambda i, j: (i // tps, 0, 0, 0)),
            *emit_specs,
        ],
        out_shape=[
            jax.ShapeDtypeStruct((M, D), F32),
            jax.ShapeDtypeStruct((nseq, nf, cr, tf), F32),
            jax.ShapeDtypeStruct((nseq, nf, cr, tf), F32),
            *emit_shapes,
        ],
        scratch_shapes=[
            pltpu.VMEM((tm, D), BF16),
            pltpu.VMEM((cr + tm, tf), F32),
            pltpu.VMEM((cr + tm, tf), F32),
            pltpu.VMEM((nf, cr, tf), F32),
            pltpu.VMEM((nf, cr, tf), F32),
        ],
        compiler_params=_params(2),
        name="conv_ffn",
    )(x, gains, w_gate[0].arr, w_val[0].arr, conv_w, conv_w, conv_b, conv_b, w_down.arr, init, init,
      *([final_gain] if final else []))
    unfold = lambda s: jnp.swapaxes(s, 1, 2).reshape(nseq, cr, F)
    return out, jnp.concatenate([unfold(sg), unfold(sv)], axis=-1), (tuple(emitted) if emitted else None)


ATTN_SUPER = max(w for w, _ in ATTN_GROUPS)
ATTN_BATCH = 8
ATTN_TRANSPOSE_DIL = SUBLANES


def _attn_prompt_kernel(sl_ref, q0, k0, v0, q1, k1, v1, q2, k2, v2, o_ref,
                        kc0, vc0, kc1, vc1, kc2, vc2, o_scr, l_scr):
    n = pl.program_id(2)
    SB = o_ref.shape[0]
    blk = ATTN_BLK
    row = lax.broadcasted_iota(jnp.int32, (blk, 2 * blk), 0)
    col = lax.broadcasted_iota(jnp.int32, (blk, 2 * blk), 1)
    steps = row + blk - col
    band = jnp.logical_and(steps >= 0, steps <= blk)
    band_cur = jnp.logical_and(band, col >= blk)
    steps_f = steps.astype(F32)
    ones_v = jnp.ones((2 * blk, HEAD_DIM), BF16)
    inv_sqrt = 1.0 / math.sqrt(HEAD_DIM)
    nt = (((1,), (1,)), ((), ()))
    groups = zip(ATTN_GROUPS, (q0, q1, q2), (k0, k1, k2), (v0, v1, v2), (kc0, kc1, kc2), (vc0, vc1, vc2))
    for g, ((win, dil), q_ref, k_ref, v_ref, kcar, vcar) in enumerate(groups):
        @pl.when(n == 0)
        def _():
            kcar[...] = jnp.zeros(kcar.shape, kcar.dtype)
            vcar[...] = jnp.zeros(vcar.shape, vcar.dtype)

        alibi = (sl_ref[g:g + 1, :] * float(-dil)) * steps_f
        bias_full = jnp.where(band, alibi, NEG)
        bias_head = jnp.where(n > 0, bias_full, jnp.where(band_cur, alibi, NEG))

        def rows(start):
            return pl.ds(start, blk, stride=dil) if dil > 1 else pl.ds(start, blk)

        transposed = dil >= ATTN_TRANSPOSE_DIL
        if transposed:
            assert win == SB
            split = lambda ref: jnp.swapaxes(ref[...].astype(BF16).reshape(blk, dil, HEAD_DIM), 0, 1)
            q_ph, k_ph, v_ph = split(q_ref), split(k_ref), split(v_ref)
            k_prev, v_prev = kcar[...], vcar[...]
        o_parts, l_parts = [], []
        n_blocks = SB // blk
        for b0 in range(0, n_blocks, ATTN_BATCH):
            infos = []
            for bidx in range(b0, b0 + ATTN_BATCH):
                j, r = divmod(bidx, dil)
                infos.append((j, r, j * win + r))
            scores = []
            for j, r, p0 in infos:
                if transposed:
                    q, kp, kc = q_ph[r], k_prev[r], k_ph[r]
                else:
                    q = q_ref[rows(p0), :].astype(BF16)
                    kp = (kcar[rows(r), :] if j == 0 else k_ref[rows(p0 - win), :]).astype(BF16)
                    kc = k_ref[rows(p0), :].astype(BF16)
                scores.append(lax.dot_general(q, jnp.concatenate([kp, kc], axis=0), nt, preferred_element_type=F32))
            probs = []
            for (j, r, p0), s in zip(infos, scores):
                s = s * inv_sqrt + (bias_head if j == 0 else bias_full)
                m = jnp.max(jnp.maximum(s[:, :blk], s[:, blk:]), axis=-1, keepdims=True)
                probs.append((jnp.exp(s - m).astype(BF16), m))
            for (j, r, p0), (e, m) in zip(infos, probs):
                if transposed:
                    vp, vc = v_prev[r], v_ph[r]
                else:
                    vp = (vcar[rows(r), :] if j == 0 else v_ref[rows(p0 - win), :]).astype(BF16)
                    vc = v_ref[rows(p0), :].astype(BF16)
                vcat = jnp.concatenate([vp, vc], axis=0)
                res = jnp.dot(e, jnp.concatenate([vcat, ones_v], axis=1), preferred_element_type=F32)
                den = res[:, HEAD_DIM:]
                o_blk, l_blk = res[:, :HEAD_DIM] / den, m + jnp.log(den)
                if transposed:
                    o_parts.append(o_blk)
                    l_parts.append(l_blk)
                else:
                    o_scr[g, rows(p0), :] = o_blk
                    l_scr[g, rows(p0), :] = l_blk
        if transposed:
            merge = lambda parts: jnp.swapaxes(jnp.stack(parts, axis=0), 0, 1).reshape(SB, HEAD_DIM)
            o_scr[g] = merge(o_parts)
            l_scr[g] = merge(l_parts)
            kcar[...] = k_ph
            vcar[...] = v_ph
        else:
            kcar[...] = k_ref[SB - win:SB, :]
            vcar[...] = v_ref[SB - win:SB, :]

    chunk = 2 * blk
    for c0 in range(0, SB, chunk):
        ls = [l_scr[g, c0:c0 + chunk, :] for g in range(N_GROUPS)]
        mm = jnp.maximum(jnp.maximum(ls[0], ls[1]), ls[2])
        es = [jnp.exp(l - mm) for l in ls]
        acc = es[0] * o_scr[0, c0:c0 + chunk, :]
        for g in range(1, N_GROUPS):
            acc = acc + es[g] * o_scr[g, c0:c0 + chunk, :]
        o_ref[c0:c0 + chunk, :] = (acc / (es[0] + es[1] + es[2])).astype(BF16)


def attn_prompt(qkv_slabs, B, S):
    n_slabs, M, _ = qkv_slabs.shape
    SB = ATTN_SUPER
    assert n_slabs == 3 * N_GROUPS * GROUP_HEADS and S % SB == 0 and M == B * S
    assert all(w // d == ATTN_BLK for w, d in ATTN_GROUPS) and (SB // ATTN_BLK) % ATTN_BATCH == 0
    nsb = S // SB
    slopes = jnp.asarray(_alibi_slopes(), F32).T
    slopes = jnp.broadcast_to(slopes[:, :, None], (GROUP_HEADS, N_GROUPS, 2 * ATTN_BLK))

    def slab(g, comp):
        base = (g * 3 + comp) * GROUP_HEADS
        return pl.BlockSpec((None, SB, HEAD_DIM), lambda b, h, n: (base + h, b * nsb + n, 0))

    carry = [pltpu.VMEM((d, w // d, HEAD_DIM), BF16) if d >= ATTN_TRANSPOSE_DIL else pltpu.VMEM((w, HEAD_DIM), F32)
             for w, d in ATTN_GROUPS for _kv in range(2)]
    return pl.pallas_call(
        _attn_prompt_kernel,
        grid=(B, GROUP_HEADS, nsb),
        in_specs=[pl.BlockSpec((None, N_GROUPS, 2 * ATTN_BLK), lambda b, h, n: (h, 0, 0))]
        + [slab(g, comp) for g in range(N_GROUPS) for comp in range(3)],
        out_specs=pl.BlockSpec((None, SB, HEAD_DIM), lambda b, h, n: (h, b * nsb + n, 0)),
        out_shape=jax.ShapeDtypeStruct((GROUP_HEADS, M, HEAD_DIM), BF16),
        scratch_shapes=carry + [pltpu.VMEM((N_GROUPS, SB, HEAD_DIM), F32), pltpu.VMEM((N_GROUPS, SB, HEAD_DIM), F32)],
        compiler_params=_params(3),
        name="attn_prompt",
    )(slopes, *([qkv_slabs] * (3 * N_GROUPS)))


def _kv_tail_kernel(*refs):
    o_ref = refs[-1]
    for i, ref in enumerate(refs[:-1]):
        o_ref[i // 2, :, i % 2] = jnp.swapaxes(ref[...], 0, 1)


def kv_tails(qkv_layers, g, B, S, keep):
    chunk = min(keep, 256)
    assert keep % chunk == 0 and S % chunk == 0
    first = (S - keep) // chunk
    per_seq = S // chunk

    def slabs(comp):
        return pl.BlockSpec((GROUP_HEADS, chunk, HEAD_DIM), lambda b, t: (g * 3 + comp, b * per_seq + first + t, 0))

    n = len(qkv_layers)
    return pl.pallas_call(
        _kv_tail_kernel,
        grid=(B, keep // chunk),
        in_specs=[slabs(comp) for _ in range(n) for comp in (1, 2)],
        out_specs=pl.BlockSpec((n, None, chunk, 2, GROUP_HEADS, HEAD_DIM), lambda b, t: (0, b, t, 0, 0, 0)),
        out_shape=jax.ShapeDtypeStruct((n, B, keep, 2, GROUP_HEADS, HEAD_DIM), F32),
        compiler_params=_params(2),
        name="kv_tails",
    )(*[q for q in qkv_layers for _ in range(2)])


def _attn_sample_kernel(sl_ref, qkv_ref, c0_ref, c1_ref, c2_ref, o_ref):
    T = qkv_ref.shape[0]
    H = GROUP_HEADS
    caches = (c0_ref, c1_ref, c2_ref)
    rows = c0_ref.shape[0]
    l_idx = lax.broadcasted_iota(jnp.int32, (rows, H, 1), 0)
    l_f = l_idx.astype(F32)
    inv_sqrt = 1.0 / math.sqrt(HEAD_DIM)
    for t in range(T):
        outs, lses = [], []
        for g, (win, dil) in enumerate(ATTN_GROUPS):
            base = g * 3 * H
            slope = sl_ref[g][:, 0:1]
            q = qkv_ref[t, base:base + H, :]
            ph = 0 if dil == 1 else t
            kc = caches[g][:, ph * 2 * H:ph * 2 * H + H, :]
            vc = caches[g][:, ph * 2 * H + H:(ph + 1) * 2 * H, :]
            sc = jnp.sum(kc * q[None], axis=-1, keepdims=True) * inv_sqrt
            if dil == 1:
                sc = jnp.where(l_idx >= t, sc - slope[None] * (float(rows + t) - l_f), NEG)
                new_ts = list(range(t + 1))
            else:
                sc = sc - (slope[None] * float(dil)) * (float(rows) - l_f)
                new_ts = [t]
            m = jnp.max(sc, axis=0)
            s_new = []
            for t2 in new_ts:
                k2 = qkv_ref[t2, base + H:base + 2 * H, :]
                s2 = jnp.sum(q * k2, axis=-1, keepdims=True) * inv_sqrt - slope * float((t - t2) * dil)
                s_new.append(s2)
                m = jnp.maximum(m, s2)
            ec = jnp.exp(sc - m[None])
            den = jnp.sum(ec, axis=0)
            acc = jnp.sum(ec * vc, axis=0)
            for t2, s2 in zip(new_ts, s_new):
                v2 = qkv_ref[t2, base + 2 * H:base + 3 * H, :]
                e2 = jnp.exp(s2 - m)
                den = den + e2
                acc = acc + e2 * v2
            outs.append(acc / den)
            lses.append(m + jnp.log(den))
        mm = jnp.maximum(jnp.maximum(lses[0], lses[1]), lses[2])
        es = [jnp.exp(l - mm) for l in lses]
        o_ref[t] = (es[0] * outs[0] + es[1] * outs[1] + es[2] * outs[2]) / (es[0] + es[1] + es[2])


def attn_sample(qkv, caches, layer, T, Bs):
    H = GROUP_HEADS
    n_rows = qkv.shape[-1] // HEAD_DIM
    views, specs = [], []
    for (win, dil), c in zip(ATTN_GROUPS, caches):
        assert c.shape[1] == Bs and c.shape[2] == win and win // dil == ATTN_BLK and (dil == 1 or T <= dil)
        phases = min(dil, -(-T // 4) * 4)
        views.append(c.reshape(c.shape[0], Bs, win // dil, dil * 2 * H, HEAD_DIM))
        specs.append(pl.BlockSpec((None, None, win // dil, phases * 2 * H, HEAD_DIM), lambda b: (layer, b, 0, 0, 0)))
    slopes = jnp.broadcast_to(jnp.asarray(_alibi_slopes(), F32)[:, :, None], (N_GROUPS, H, HEAD_DIM))
    q_rows = jnp.swapaxes(qkv.reshape(T, Bs, n_rows, HEAD_DIM), 0, 1)
    out = pl.pallas_call(
        _attn_sample_kernel,
        grid=(Bs,),
        in_specs=[pl.BlockSpec((N_GROUPS, H, HEAD_DIM), lambda b: (0, 0, 0)),
                  pl.BlockSpec((None, T, n_rows, HEAD_DIM), lambda b: (b, 0, 0, 0)), *specs],
        out_specs=pl.BlockSpec((None, T, H, HEAD_DIM), lambda b: (b, 0, 0, 0)),
        out_shape=jax.ShapeDtypeStruct((Bs, T, H, HEAD_DIM), F32),
        compiler_params=_params(1),
        name="attn_sample",
    )(slopes, q_rows, *views)
    return jnp.swapaxes(out, 0, 1).reshape(T * Bs, ATTN_WIDTH)


class Tiles(NamedTuple):
    seq: int
    mm: int
    ffn: int
    tf: int


def _run_trunk(x, nseq, R, tiles, lru_h, lru_conv, kv_caches, ffn_conv, P, W, dims):
    depth = P["norm_mix"].shape[0]
    new_h, new_lconv, new_fconv, qkvs = [], [], [], []
    bf16_w = {name: [] for name in W}
    for layer in range(depth):
        j = layer // 2
        if layer % 2 == 0:
            proj, wb = norm_matmul(x, P["norm_mix"], layer, W["lru_w_in"][j], P["lru_b_in"], j, tm=tiles.mm)
            bf16_w["lru_w_in"].append(wb)
            hg, c_rows, h_rows = lru_core(proj, P["lru_conv_w"], P["lru_conv_b"], P["lru_w_a"], P["lru_b_a"],
                                          P["lru_w_i"], P["lru_b_i"], P["lru_lambda"], lru_conv[j], lru_h[j],
                                          layer=j, tm=tiles.seq, R=R)
            x, wb = matmul_res(hg, W["lru_w_out"][j], P["lru_b_out"], j, x, tm=tiles.mm)
            bf16_w["lru_w_out"].append(wb)
            new_h.append(h_rows)
            new_lconv.append(c_rows)
        else:
            prompt = kv_caches is None
            qkv, wb = norm_matmul(x, P["norm_mix"], layer, W["attn_w_qkv"][j], P["attn_b_zero"], j, tm=tiles.mm,
                                  slab_out=prompt)
            bf16_w["attn_w_qkv"].append(wb)
            qkvs.append(qkv)
            if prompt:
                o = attn_prompt(qkv, *dims)
            else:
                o = attn_sample(qkv, kv_caches, layer=j, T=dims[0], Bs=dims[1])
            x, wb = matmul_res(o, W["attn_w_o"][j], P["attn_bo_zero"], j, x, tm=tiles.mm)
            bf16_w["attn_w_o"].append(wb)
        x, f_rows, wbs = conv_ffn(x, P["norm_ffn"], W["ffn_w_gate"][layer], W["ffn_w_val"][layer], P["ffn_conv_w"],
                                  P["ffn_conv_b"], W["ffn_w_down"][layer], ffn_conv[layer], layer=layer,
                                  tm=tiles.ffn, tf=tiles.tf, R=R,
                                  final_gain=P["norm_final"] if layer == depth - 1 else None)
        for name, wb in zip(("ffn_w_gate", "ffn_w_val", "ffn_w_down"), wbs or (None,) * 3):
            bf16_w[name].append(wb)
        new_fconv.append(f_rows)
    return x, new_h, new_lconv, qkvs, new_fconv, bf16_w


def kernel(x_prompt, x_sample, cache_kv_w128, cache_kv_w512, cache_kv_w2048, state_lru_h, state_lru_conv, state_ffn_conv, norm_mix, norm_ffn, norm_final, lru_w_in, lru_b_in, lru_conv_w, lru_conv_b, lru_w_a, lru_b_a, lru_w_i, lru_b_i, lru_lambda, lru_w_out, lru_b_out, attn_w_qkv, attn_w_o, ffn_w_up, ffn_conv_w, ffn_conv_b, ffn_w_down):
    B, S, D = x_prompt.shape
    Bs, T, _ = x_sample.shape
    depth = norm_mix.shape[0]
    n_lru, W = lru_lambda.shape
    n_attn = attn_w_qkv.shape[0]
    F2 = ffn_w_up.shape[-1]
    assert Bs == SUBLANES, "the sample group is laid out time-major with one sublane per sequence"
    row3 = lambda a: a.reshape(a.shape[0], 1, a.shape[-1])
    P = {
        "norm_mix": row3(norm_mix), "norm_ffn": row3(norm_ffn), "norm_final": norm_final.reshape(1, D),
        "lru_b_in": row3(lru_b_in), "lru_conv_w": lru_conv_w, "lru_conv_b": row3(lru_conv_b),
        "lru_w_a": lru_w_a, "lru_b_a": row3(lru_b_a), "lru_w_i": lru_w_i, "lru_b_i": row3(lru_b_i),
        "lru_lambda": row3(lru_lambda), "lru_b_out": row3(lru_b_out),
        "attn_b_zero": jnp.zeros((n_attn, 1, attn_w_qkv.shape[-1]), F32),
        "attn_bo_zero": jnp.zeros((n_attn, 1, D), F32),
        "ffn_conv_w": ffn_conv_w, "ffn_conv_b": row3(ffn_conv_b),
    }
    stack = lambda w: [Weight(w, i) for i in range(w.shape[0])]
    W_f32 = {
        "lru_w_in": stack(lru_w_in), "lru_w_out": stack(lru_w_out),
        "attn_w_qkv": stack(attn_w_qkv), "attn_w_o": stack(attn_w_o),
        "ffn_w_gate": [(w, 0) for w in stack(ffn_w_up)], "ffn_w_val": [(w, F2 // 2) for w in stack(ffn_w_up)],
        "ffn_w_down": stack(ffn_w_down),
    }
    kc = lru_conv_w.shape[1] - 1
    kf = ffn_conv_w.shape[1] - 1

    tmaj = lambda a: jnp.swapaxes(a, 0, 1).reshape(1, a.shape[1] * a.shape[0], a.shape[-1])
    y_s, h_s, lc_s, qkv_s, fc_s, bf16_w = _run_trunk(
        jnp.swapaxes(x_sample, 0, 1).reshape(T * Bs, D), 1, Bs,
        Tiles(seq=T * Bs, mm=T * Bs, ffn=T * Bs, tf=_pick(F2 // 2, 1024)),
        [state_lru_h[j].reshape(1, Bs, W) for j in range(n_lru)],
        [tmaj(state_lru_conv[j]) for j in range(n_lru)],
        (cache_kv_w128, cache_kv_w512, cache_kv_w2048),
        [tmaj(state_ffn_conv[l]) for l in range(depth)], P, W_f32, (T, Bs))

    W_bf16 = {name: [(Weight(w, 0), 0) if name in ("ffn_w_gate", "ffn_w_val") else Weight(w, 0) for w in ws]
              for name, ws in bf16_w.items()}

    row_tile = lambda cap: cap if S % cap == 0 else S
    tiles_p = Tiles(seq=row_tile(512), mm=row_tile(1024), ffn=row_tile(1024), tf=_pick(F2 // 2, 512))
    zeros = lambda n, w: [jnp.zeros((B, SUBLANES, w), F32)] * n
    y_p, h_p, lc_p, qkv_p, fc_p, _ = _run_trunk(
        x_prompt.reshape(B * S, D), B, 1, tiles_p, zeros(n_lru, W), zeros(n_lru, W), None, zeros(depth, F2), P, W_bf16, (B, S))

    bmaj = lambda a, k: jnp.swapaxes(a.reshape(-1, Bs, a.shape[-1])[-k:], 0, 1)
    kv_p, kv_s = [], []
    for g, (win, dil) in enumerate(ATTN_GROUPS):
        lo = (g * 3 + 1) * ATTN_WIDTH
        keep = min(win, S)
        kv_p.append(kv_tails(qkv_p, g, B, S, keep))
        kv_s.append(jnp.stack([jnp.swapaxes(q.reshape(T, Bs, -1), 0, 1)[:, :, lo:lo + 2 * ATTN_WIDTH]
                               .reshape(Bs, T, 2, GROUP_HEADS, HEAD_DIM) for q in qkv_s], axis=0))
    return (
        y_p.reshape(B, S, D),
        jnp.swapaxes(y_s.reshape(T, Bs, D), 0, 1),
        kv_p[0], kv_p[1], kv_p[2],
        jnp.stack([h[:, SUBLANES - 1] for h in h_p], axis=0),
        jnp.stack([c[:, SUBLANES - kc:] for c in lc_p], axis=0),
        jnp.stack([f[:, SUBLANES - kf:] for f in fc_p], axis=0),
        kv_s[0], kv_s[1], kv_s[2],
        jnp.stack([h[0] for h in h_s], axis=0),
        jnp.stack([bmaj(c[0], kc) for c in lc_s], axis=0),
        jnp.stack([bmaj(f[0], kf) for f in fc_s], axis=0),
    )
```

```python
import functools
import math
from typing import NamedTuple

import jax
import jax.numpy as jnp
from jax import lax
from jax.experimental import pallas as pl
from jax.experimental.pallas import tpu as pltpu

EPS = 1e-6
NEG = -1e30
LRU_C = 8.0
ATTN_GROUPS = ((128, 1), (512, 4), (2048, 16))
N_GROUPS = len(ATTN_GROUPS)
GROUP_HEADS = 8
HEAD_DIM = 128
ATTN_WIDTH = GROUP_HEADS * HEAD_DIM
ATTN_BLK = 128
SUBLANES = 8
LANES = 128
VMEM_LIMIT = 56 * 1024 * 1024

ROW_TILE_MATMUL = 1024
ROW_TILE_LRU = 512
COL_TILE_MATMUL = 1536
COL_TILE_FEW_ROWS = 2048
COL_TILE_F32_WEIGHT = 1024
FEW_ROWS = 256
LRU_CHANNEL_TILE = 1024
FFN_TILE = 768
FFN_TILE_FEW_ROWS = 1024
FFN_TILE_F32_WEIGHT = 512
KV_TAIL_ROWS = 256

F32 = jnp.float32
F32_TINY = 1.1754944e-38
BF16 = jnp.bfloat16


def _alibi_slopes():
    n = N_GROUPS * GROUP_HEADS
    return [[2.0 ** (-8.0 * (g * GROUP_HEADS + h + 1) / n) for h in range(GROUP_HEADS)] for g in range(N_GROUPS)]


def _gelu(x):
    c = math.sqrt(2.0 / math.pi)
    return x * (0.5 * (1.0 + jnp.tanh(c * (x + 0.044715 * (x * x * x)))))


def _rms(x, g):
    ms = jnp.mean(x * x, axis=-1, keepdims=True)
    return x * lax.rsqrt(ms + EPS) * g


def _pick(n, cap):
    best = None
    for t in range(LANES, min(n, cap) + 1, LANES):
        if n % t == 0:
            best = t
    assert best is not None, (n, cap)
    return best


def _params(n_axes):
    return pltpu.CompilerParams(dimension_semantics=("arbitrary",) * n_axes, vmem_limit_bytes=VMEM_LIMIT)


def _as_bf16_weight(w_ref, emit_refs):
    w = w_ref[...]
    if w.dtype != BF16:
        w = w.astype(BF16)
        emit_refs[0][...] = w
    return w


def _norm_matmul_kernel(x_ref, g_ref, w_ref, b_ref, o_ref, *rest):
    xn_ref = rest[-1]

    @pl.when(pl.program_id(1) == 0)
    def _():
        xn_ref[...] = _rms(x_ref[...], g_ref[...]).astype(BF16)

    acc = jnp.dot(xn_ref[...], _as_bf16_weight(w_ref, rest), preferred_element_type=F32) + b_ref[...]
    if len(o_ref.shape) == 2:
        o_ref[...] = acc
    else:
        for c in range(o_ref.shape[0]):
            o_ref[c] = acc[:, c * LANES:(c + 1) * LANES]


class Weight(NamedTuple):
    arr: jax.Array
    layer: int


def _weight_specs(w, block, index):
    spec = pl.BlockSpec((None, *block), lambda *ij: (w.layer, *index(*ij)))
    if w.arr.dtype == BF16:
        return spec, [], []
    return (spec, [pl.BlockSpec((None, *block), lambda *ij: (0, *index(*ij)))],
            [jax.ShapeDtypeStruct((1, *w.arr.shape[1:]), BF16)])


def _tile_cap(tm, w):
    if w.arr.dtype != BF16:
        return COL_TILE_F32_WEIGHT
    return COL_TILE_MATMUL if tm > FEW_ROWS else COL_TILE_FEW_ROWS


def norm_matmul(x, gains, g_layer, w, bias, b_layer, tm, slab_out=False):
    M, D = x.shape
    N = w.arr.shape[-1]
    tn = _pick(N, _tile_cap(tm, w))
    if slab_out:
        out_spec = pl.BlockSpec((tn // LANES, tm, LANES), lambda i, j: (j, i, 0))
        out_shape = jax.ShapeDtypeStruct((N // LANES, M, LANES), F32)
    else:
        out_spec = pl.BlockSpec((tm, tn), lambda i, j: (i, j))
        out_shape = jax.ShapeDtypeStruct((M, N), F32)
    w_spec, emit_specs, emit_shapes = _weight_specs(w, (D, tn), lambda i, j: (0, j))
    assert not emit_specs or M == tm, "a weight tile must be visited once to be emitted"
    out, *emitted = pl.pallas_call(
        _norm_matmul_kernel,
        grid=(M // tm, N // tn),
        in_specs=[
            pl.BlockSpec((tm, D), lambda i, j: (i, 0)),
            pl.BlockSpec((None, 1, D), lambda i, j: (g_layer, 0, 0)),
            w_spec,
            pl.BlockSpec((None, 1, tn), lambda i, j: (b_layer, 0, j)),
        ],
        out_specs=[out_spec, *emit_specs],
        out_shape=[out_shape, *emit_shapes],
        scratch_shapes=[pltpu.VMEM((tm, D), BF16)],
        compiler_params=_params(2),
        name="norm_matmul",
    )(x, gains, w.arr, bias)
    return out, (emitted[0] if emitted else None)


def _matmul_res_kernel(a_ref, w_ref, b_ref, r_ref, o_ref, *emit):
    if len(a_ref.shape) == 2:
        a = a_ref[...]
    else:
        a = jnp.concatenate([a_ref[c] for c in range(a_ref.shape[0])], axis=1)
    o_ref[...] = r_ref[...] + b_ref[...] + jnp.dot(a.astype(BF16), _as_bf16_weight(w_ref, emit),
                                                   preferred_element_type=F32)


def matmul_res(a, w, bias, b_layer, res, tm):
    M, N = res.shape
    K = w.arr.shape[1]
    tn = _pick(N, _tile_cap(tm, w))
    if a.ndim == 2:
        a_spec = pl.BlockSpec((tm, K), lambda i, j: (i, 0))
    else:
        a_spec = pl.BlockSpec((K // LANES, tm, LANES), lambda i, j: (0, i, 0))
    w_spec, emit_specs, emit_shapes = _weight_specs(w, (K, tn), lambda i, j: (0, j))
    assert not emit_specs or M == tm, "a weight tile must be visited once to be emitted"
    out, *emitted = pl.pallas_call(
        _matmul_res_kernel,
        grid=(M // tm, N // tn),
        in_specs=[
            a_spec,
            w_spec,
            pl.BlockSpec((None, 1, tn), lambda i, j: (b_layer, 0, j)),
            pl.BlockSpec((tm, tn), lambda i, j: (i, j)),
        ],
        out_specs=[pl.BlockSpec((tm, tn), lambda i, j: (i, j)), *emit_specs],
        out_shape=[jax.ShapeDtypeStruct((M, N), F32), *emit_shapes],
        compiler_params=_params(2),
        name="matmul_res",
    )(a, w.arr, bias, res)
    return out, (emitted[0] if emitted else None)


def _lru_kernel(gate_ref, u_ref, cw_ref, cb_ref, wa_ref, ba_ref, wi_ref, bi_ref, lam_ref, ci_ref, hi_ref,
                o_ref, cs_ref, hs_ref,
                e_ref, a_scr, b_scr, h_scr, cc_ref, hc_ref, *, R, cru, tps):
    i = pl.program_id(0)
    c = pl.program_id(1)
    tm, tc = u_ref.shape

    @pl.when(i % tps == 0)
    def _():
        cc_ref[c] = ci_ref[...]
        hc_ref[c] = hi_ref[...]

    u = u_ref[...]
    e_ref[0:cru, :] = cc_ref[c]
    e_ref[cru:cru + tm, :] = u
    cw = cw_ref[...]
    uc = (cb_ref[...] + cw[3:4] * u + cw[2:3] * e_ref[cru - R:cru - R + tm, :]
          + cw[1:2] * e_ref[cru - 2 * R:cru - 2 * R + tm, :] + cw[0:1] * e_ref[cru - 3 * R:cru - 3 * R + tm, :])
    tail = e_ref[tm:tm + cru, :]
    cc_ref[c] = tail
    cs_ref[c] = tail

    hd = wa_ref.shape[-1]
    ucb = uc.astype(BF16)
    ra, ri = [], []
    for hh in range(tc // hd):
        ub = ucb[:, hh * hd:(hh + 1) * hd]
        ra.append(jnp.dot(ub, wa_ref[hh].astype(BF16), preferred_element_type=F32))
        ri.append(jnp.dot(ub, wi_ref[hh].astype(BF16), preferred_element_type=F32))
    r = jax.nn.sigmoid(jnp.concatenate(ra, axis=1) + ba_ref[...])
    ig = jax.nn.sigmoid(jnp.concatenate(ri, axis=1) + bi_ref[...])
    nlam = -lam_ref[...]
    softplus = jnp.maximum(nlam, 0.0) + jnp.log1p(jnp.exp(-jnp.abs(nlam)))
    log_a = (-LRU_C) * r * softplus
    a = jnp.exp(log_a)
    th = jnp.tanh(log_a)
    z = (-2.0 * th) / (1.0 - th)
    bx = (z * lax.rsqrt(jnp.maximum(z, F32_TINY))) * (ig * uc)

    if R == 1:
        A = a.reshape(tm // SUBLANES, SUBLANES, tc)
        B = bx.reshape(tm // SUBLANES, SUBLANES, tc)
        row = lax.broadcasted_iota(jnp.int32, A.shape, 1)
        s = 1
        while s < SUBLANES:
            m = row >= s
            B = jnp.where(m, A * pltpu.roll(B, s, axis=1) + B, B)
            A = jnp.where(m, A * pltpu.roll(A, s, axis=1), A)
            s *= 2
        a_scr[...] = A.reshape(tm, tc)
        b_scr[...] = B.reshape(tm, tc)
        h0 = hc_ref[c][SUBLANES - 1:SUBLANES, :]
    else:
        assert R == SUBLANES
        a_scr[...] = a
        b_scr[...] = bx
        h0 = hc_ref[c]

    def body(g, h):
        r0 = pl.multiple_of(g * SUBLANES, SUBLANES)
        hg = b_scr[pl.ds(r0, SUBLANES), :] + a_scr[pl.ds(r0, SUBLANES), :] * h
        h_scr[pl.ds(r0, SUBLANES), :] = hg
        return hg[SUBLANES - 1:SUBLANES, :] if R == 1 else hg

    n_groups = tm // SUBLANES
    lax.fori_loop(0, n_groups, body, h0, unroll=min(n_groups, 8))
    h_tail = h_scr[tm - SUBLANES:tm, :]
    hc_ref[c] = h_tail
    hs_ref[c] = h_tail
    o_ref[...] = (h_scr[...] * _gelu(gate_ref[...])).astype(BF16)


def lru_core(proj, conv_w, conv_b, w_a, b_a, w_i, b_i, lam, conv_init, h_init, layer, tm, R):
    M, W2 = proj.shape
    W = W2 // 2
    hd = w_a.shape[-1]
    tc = max(hd, _pick(W, LRU_CHANNEL_TILE))
    assert tc % hd == 0 and W % tc == 0
    nc = W // tc
    nseq, cru, _ = conv_init.shape
    tps = (M // tm) // nseq
    kern = functools.partial(_lru_kernel, R=R, cru=cru, tps=tps)
    row_vec = lambda: pl.BlockSpec((None, 1, tc), lambda i, c: (layer, 0, c))
    gate_w = lambda: pl.BlockSpec((None, tc // hd, hd, hd), lambda i, c: (layer, c, 0, 0))
    unfold = lambda s: jnp.swapaxes(s, 1, 2).reshape(nseq, s.shape[2], W)
    hg, c_rows, h_rows = pl.pallas_call(
        kern,
        grid=(M // tm, nc),
        in_specs=[
            pl.BlockSpec((tm, tc), lambda i, c: (i, c)),
            pl.BlockSpec((tm, tc), lambda i, c: (i, nc + c)),
            pl.BlockSpec((None, conv_w.shape[1], tc), lambda i, c: (layer, 0, c)),
            row_vec(), gate_w(), row_vec(), gate_w(), row_vec(), row_vec(),
            pl.BlockSpec((None, cru, tc), lambda i, c: (i // tps, 0, c)),
            pl.BlockSpec((None, SUBLANES, tc), lambda i, c: (i // tps, 0, c)),
        ],
        out_specs=[
            pl.BlockSpec((tm, tc), lambda i, c: (i, c)),
            pl.BlockSpec((None, nc, cru, tc), lambda i, c: (i // tps, 0, 0, 0)),
            pl.BlockSpec((None, nc, SUBLANES, tc), lambda i, c: (i // tps, 0, 0, 0)),
        ],
        out_shape=[
            jax.ShapeDtypeStruct((M, W), BF16),
            jax.ShapeDtypeStruct((nseq, nc, cru, tc), F32),
            jax.ShapeDtypeStruct((nseq, nc, SUBLANES, tc), F32),
        ],
        scratch_shapes=[
            pltpu.VMEM((cru + tm, tc), F32),
            pltpu.VMEM((tm, tc), F32),
            pltpu.VMEM((tm, tc), F32),
            pltpu.VMEM((tm, tc), F32),
            pltpu.VMEM((nc, cru, tc), F32),
            pltpu.VMEM((nc, SUBLANES, tc), F32),
        ],
        compiler_params=_params(2),
        name="lru_core",
    )(proj, proj, conv_w, conv_b, w_a, b_a, w_i, b_i, lam, conv_init, h_init)
    return hg, unfold(c_rows), unfold(h_rows)


def _ffn_kernel(x_ref, g_ref, wg_ref, wv_ref, cwg_ref, cwv_ref, cbg_ref, cbv_ref, wd_ref, ig_ref, iv_ref, *rest,
                R, cr, tps, final):
    fg_ref = rest[0] if final else None
    o_ref, sg_ref, sv_ref, *emit, xn_ref, eg_ref, ev_ref, cg_ref, cv_ref = rest[1:] if final else rest
    i = pl.program_id(0)
    j = pl.program_id(1)
    tm = x_ref.shape[0]

    @pl.when(j == 0)
    def _():
        x = x_ref[...]
        xn_ref[...] = _rms(x, g_ref[...]).astype(BF16)
        o_ref[...] = x

    @pl.when(i % tps == 0)
    def _():
        cg_ref[j] = ig_ref[...]
        cv_ref[j] = iv_ref[...]

    def side(w_ref, cw_ref, cb_ref, e_ref, c_ref, s_ref, emit_ref):
        up = jnp.dot(xn_ref[...], _as_bf16_weight(w_ref, emit_ref), preferred_element_type=F32)
        e_ref[0:cr, :] = c_ref[j]
        e_ref[cr:cr + tm, :] = up
        cw = cw_ref[...]
        conv = (cb_ref[...] + cw[2:3] * up + cw[1:2] * e_ref[cr - R:cr - R + tm, :]
                + cw[0:1] * e_ref[cr - 2 * R:cr - 2 * R + tm, :])
        tail = e_ref[tm:tm + cr, :]
        c_ref[j] = tail
        s_ref[j] = tail
        return conv

    cg = side(wg_ref, cwg_ref, cbg_ref, eg_ref, cg_ref, sg_ref, emit[0:1])
    cv = side(wv_ref, cwv_ref, cbv_ref, ev_ref, cv_ref, sv_ref, emit[1:2])
    act = (_gelu(cg) * cv).astype(BF16)
    o_ref[...] += jnp.dot(act, _as_bf16_weight(wd_ref, emit[2:3]), preferred_element_type=F32)
    if final:
        @pl.when(j == pl.num_programs(1) - 1)
        def _():
            o_ref[...] = _rms(o_ref[...], fg_ref[...])


def conv_ffn(x, gains, w_gate, w_val, conv_w, conv_b, w_down, init, layer, tm, tf, R, final_gain=None):
    M, D = x.shape
    F = w_down.arr.shape[1]
    emitting = w_down.arr.dtype != BF16
    if emitting:
        tf = min(tf, FFN_TILE_F32_WEIGHT)
    nf = F // tf
    nseq, cr, _ = init.shape
    tps = (M // tm) // nseq
    final = final_gain is not None
    kern = functools.partial(_ffn_kernel, R=R, cr=cr, tps=tps, final=final)
    K = conv_w.shape[1]
    halves = lambda mk: [mk(0), mk(nf)]
    state_spec = lambda off: pl.BlockSpec((None, cr, tf), lambda i, j: (i // tps, 0, off + j))
    w_specs, emit_specs, emit_shapes = [], [], []
    for w, col0 in (w_gate, w_val):
        assert col0 % tf == 0 and (w.arr.dtype != BF16) == emitting
        spec, e_spec, e_shape = _weight_specs(w, (D, tf), lambda i, j, off=col0 // tf: (0, off + j))
        w_specs.append(spec)
        emit_specs += [pl.BlockSpec((None, D, tf), lambda i, j: (0, 0, j))] if e_spec else []
        emit_shapes += [jax.ShapeDtypeStruct((1, D, F), BF16)] if e_shape else []
    wd_spec, e_spec, e_shape = _weight_specs(w_down, (tf, D), lambda i, j: (j, 0))
    emit_specs += e_spec
    emit_shapes += e_shape
    assert not emit_specs or M == tm, "a weight tile must be visited once to be emitted"
    once = pl.Buffered(1)
    out, sg, sv, *emitted = pl.pallas_call(
        kern,
        grid=(M // tm, nf),
        in_specs=[
            pl.BlockSpec((tm, D), lambda i, j: (i, 0), pipeline_mode=once),
            pl.BlockSpec((None, 1, D), lambda i, j: (layer, 0, 0)),
            *w_specs,
            *halves(lambda off: pl.BlockSpec((None, K, tf), lambda i, j: (layer, 0, off + j))),
            *halves(lambda off: pl.BlockSpec((None, 1, tf), lambda i, j: (layer, 0, off + j))),
            wd_spec,
            *halves(state_spec),
            *([pl.BlockSpec((1, D), lambda i, j: (0, 0))] if final else []),
        ],
        out_specs=[
            pl.BlockSpec((tm, D), lambda i, j: (i, 0), pipeline_mode=once),
            pl.BlockSpec((None, nf, cr, tf), lambda i, j: (i // tps, 0, 0, 0)),
            pl.BlockSpec((None, nf, cr, tf), lambda i, j: (i // tps, 0, 0, 0)),
            *emit_specs,
        ],
        out_shape=[
            jax.ShapeDtypeStruct((M, D), F32),
            jax.ShapeDtypeStruct((nseq, nf, cr, tf), F32),
            jax.ShapeDtypeStruct((nseq, nf, cr, tf), F32),
            *emit_shapes,
        ],
        scratch_shapes=[
            pltpu.VMEM((tm, D), BF16),
            pltpu.VMEM((cr + tm, tf), F32),
            pltpu.VMEM((cr + tm, tf), F32),
            pltpu.VMEM((nf, cr, tf), F32),
            pltpu.VMEM((nf, cr, tf), F32),
        ],
        compiler_params=_params(2),
        name="conv_ffn",
    )(x, gains, w_gate[0].arr, w_val[0].arr, conv_w, conv_w, conv_b, conv_b, w_down.arr, init, init,
      *([final_gain] if final else []))
    unfold = lambda s: jnp.swapaxes(s, 1, 2).reshape(nseq, cr, F)
    return out, jnp.concatenate([unfold(sg), unfold(sv)], axis=-1), (tuple(emitted) if emitted else None)


ATTN_SUPER = max(w for w, _ in ATTN_GROUPS)
ATTN_BATCH = 8
ATTN_TRANSPOSE_DIL = SUBLANES


def _attn_prompt_kernel(sl_ref, q0, k0, v0, q1, k1, v1, q2, k2, v2, o_ref,
                        kc0, vc0, kc1, vc1, kc2, vc2, o_scr, l_scr):
    n = pl.program_id(2)
    SB = o_ref.shape[0]
    blk = ATTN_BLK
    row = lax.broadcasted_iota(jnp.int32, (blk, 2 * blk), 0)
    col = lax.broadcasted_iota(jnp.int32, (blk, 2 * blk), 1)
    steps = row + blk - col
    band = jnp.logical_and(steps >= 0, steps <= blk)
    band_cur = jnp.logical_and(band, col >= blk)
    steps_f = steps.astype(F32)
    ones_v = jnp.ones((2 * blk, HEAD_DIM), BF16)
    inv_sqrt = 1.0 / math.sqrt(HEAD_DIM)
    nt = (((1,), (1,)), ((), ()))
    groups = zip(ATTN_GROUPS, (q0, q1, q2), (k0, k1, k2), (v0, v1, v2), (kc0, kc1, kc2), (vc0, vc1, vc2))
    for g, ((win, dil), q_ref, k_ref, v_ref, kcar, vcar) in enumerate(groups):
        @pl.when(n == 0)
        def _():
            kcar[...] = jnp.zeros(kcar.shape, kcar.dtype)
            vcar[...] = jnp.zeros(vcar.shape, vcar.dtype)

        alibi = (sl_ref[g:g + 1, :] * float(-dil)) * steps_f
        bias_full = jnp.where(band, alibi, NEG)
        bias_head = jnp.where(n > 0, bias_full, jnp.where(band_cur, alibi, NEG))

        def rows(start):
            return pl.ds(start, blk, stride=dil) if dil > 1 else pl.ds(start, blk)

        transposed = dil >= ATTN_TRANSPOSE_DIL
        if transposed:
            assert win == SB
            split = lambda ref: jnp.swapaxes(ref[...].astype(BF16).reshape(blk, dil, HEAD_DIM), 0, 1)
            q_ph, k_ph, v_ph = split(q_ref), split(k_ref), split(v_ref)
            k_prev, v_prev = kcar[...], vcar[...]
        o_parts, l_parts = [], []
        n_blocks = SB // blk
        for b0 in range(0, n_blocks, ATTN_BATCH):
            infos = []
            for bidx in range(b0, b0 + ATTN_BATCH):
                j, r = divmod(bidx, dil)
                infos.append((j, r, j * win + r))
            scores = []
            for j, r, p0 in infos:
                if transposed:
                    q, kp, kc = q_ph[r], k_prev[r], k_ph[r]
                else:
                    q = q_ref[rows(p0), :].astype(BF16)
                    kp = (kcar[rows(r), :] if j == 0 else k_ref[rows(p0 - win), :]).astype(BF16)
                    kc = k_ref[rows(p0), :].astype(BF16)
                scores.append(lax.dot_general(q, jnp.concatenate([kp, kc], axis=0), nt, preferred_element_type=F32))
            probs = []
            for (j, r, p0), s in zip(infos, scores):
                s = s * inv_sqrt + (bias_head if j == 0 else bias_full)
                m = jnp.max(jnp.maximum(s[:, :blk], s[:, blk:]), axis=-1, keepdims=True)
                probs.append((jnp.exp(s - m).astype(BF16), m))
            for (j, r, p0), (e, m) in zip(infos, probs):
                if transposed:
                    vp, vc = v_prev[r], v_ph[r]
                else:
                    vp = (vcar[rows(r), :] if j == 0 else v_ref[rows(p0 - win), :]).astype(BF16)
                    vc = v_ref[rows(p0), :].astype(BF16)
                vcat = jnp.concatenate([vp, vc], axis=0)
                res = jnp.dot(e, jnp.concatenate([vcat, ones_v], axis=1), preferred_element_type=F32)
                den = res[:, HEAD_DIM:]
                o_blk, l_blk = res[:, :HEAD_DIM] / den, m + jnp.log(den)
                if transposed:
                    o_parts.append(o_blk)
                    l_parts.append(l_blk)
                else:
                    o_scr[g, rows(p0), :] = o_blk
                    l_scr[g, rows(p0), :] = l_blk
        if transposed:
            merge = lambda parts: jnp.swapaxes(jnp.stack(parts, axis=0), 0, 1).reshape(SB, HEAD_DIM)
            o_scr[g] = merge(o_parts)
            l_scr[g] = merge(l_parts)
            kcar[...] = k_ph
            vcar[...] = v_ph
        else:
            kcar[...] = k_ref[SB - win:SB, :]
            vcar[...] = v_ref[SB - win:SB, :]

    chunk = 2 * blk
    for c0 in range(0, SB, chunk):
        ls = [l_scr[g, c0:c0 + chunk, :] for g in range(N_GROUPS)]
        mm = jnp.maximum(jnp.maximum(ls[0], ls[1]), ls[2])
        es = [jnp.exp(l - mm) for l in ls]
        acc = es[0] * o_scr[0, c0:c0 + chunk, :]
        for g in range(1, N_GROUPS):
            acc = acc + es[g] * o_scr[g, c0:c0 + chunk, :]
        o_ref[c0:c0 + chunk, :] = (acc / (es[0] + es[1] + es[2])).astype(BF16)


def attn_prompt(qkv_slabs, B, S):
    n_slabs, M, _ = qkv_slabs.shape
    SB = ATTN_SUPER
    assert n_slabs == 3 * N_GROUPS * GROUP_HEADS and S % SB == 0 and M == B * S
    assert all(w // d == ATTN_BLK for w, d in ATTN_GROUPS) and (SB // ATTN_BLK) % ATTN_BATCH == 0
    nsb = S // SB
    slopes = jnp.asarray(_alibi_slopes(), F32).T
    slopes = jnp.broadcast_to(slopes[:, :, None], (GROUP_HEADS, N_GROUPS, 2 * ATTN_BLK))

    def slab(g, comp):
        base = (g * 3 + comp) * GROUP_HEADS
        return pl.BlockSpec((None, SB, HEAD_DIM), lambda b, h, n: (base + h, b * nsb + n, 0))

    carry = [pltpu.VMEM((d, w // d, HEAD_DIM), BF16) if d >= ATTN_TRANSPOSE_DIL else pltpu.VMEM((w, HEAD_DIM), F32)
             for w, d in ATTN_GROUPS for _kv in range(2)]
    return pl.pallas_call(
        _attn_prompt_kernel,
        grid=(B, GROUP_HEADS, nsb),
        in_specs=[pl.BlockSpec((None, N_GROUPS, 2 * ATTN_BLK), lambda b, h, n: (h, 0, 0))]
        + [slab(g, comp) for g in range(N_GROUPS) for comp in range(3)],
        out_specs=pl.BlockSpec((None, SB, HEAD_DIM), lambda b, h, n: (h, b * nsb + n, 0)),
        out_shape=jax.ShapeDtypeStruct((GROUP_HEADS, M, HEAD_DIM), BF16),
        scratch_shapes=carry + [pltpu.VMEM((N_GROUPS, SB, HEAD_DIM), F32), pltpu.VMEM((N_GROUPS, SB, HEAD_DIM), F32)],
        compiler_params=_params(3),
        name="attn_prompt",
    )(slopes, *([qkv_slabs] * (3 * N_GROUPS)))


def _kv_tail_kernel(*refs):
    o_ref = refs[-1]
    for i, ref in enumerate(refs[:-1]):
        o_ref[i // 2, :, i % 2] = jnp.swapaxes(ref[...], 0, 1)


def kv_tails(qkv_layers, g, B, S, keep):
    chunk = min(keep, KV_TAIL_ROWS)
    assert keep % chunk == 0 and S % chunk == 0
    first = (S - keep) // chunk
    per_seq = S // chunk

    def slabs(comp):
        return pl.BlockSpec((GROUP_HEADS, chunk, HEAD_DIM), lambda b, t: (g * 3 + comp, b * per_seq + first + t, 0))

    n = len(qkv_layers)
    return pl.pallas_call(
        _kv_tail_kernel,
        grid=(B, keep // chunk),
        in_specs=[slabs(comp) for _ in range(n) for comp in (1, 2)],
        out_specs=pl.BlockSpec((n, None, chunk, 2, GROUP_HEADS, HEAD_DIM), lambda b, t: (0, b, t, 0, 0, 0)),
        out_shape=jax.ShapeDtypeStruct((n, B, keep, 2, GROUP_HEADS, HEAD_DIM), F32),
        compiler_params=_params(2),
        name="kv_tails",
    )(*[q for q in qkv_layers for _ in range(2)])


def _attn_sample_kernel(sl_ref, qkv_ref, c0_ref, c1_ref, c2_ref, o_ref):
    T = qkv_ref.shape[0]
    H = GROUP_HEADS
    caches = (c0_ref, c1_ref, c2_ref)
    rows = c0_ref.shape[0]
    l_idx = lax.broadcasted_iota(jnp.int32, (rows, H, 1), 0)
    l_f = l_idx.astype(F32)
    inv_sqrt = 1.0 / math.sqrt(HEAD_DIM)
    for t in range(T):
        outs, lses = [], []
        for g, (win, dil) in enumerate(ATTN_GROUPS):
            base = g * 3 * H
            slope = sl_ref[g][:, 0:1]
            q = qkv_ref[t, base:base + H, :]
            ph = 0 if dil == 1 else t
            kc = caches[g][:, ph * 2 * H:ph * 2 * H + H, :]
            vc = caches[g][:, ph * 2 * H + H:(ph + 1) * 2 * H, :]
            sc = jnp.sum(kc * q[None], axis=-1, keepdims=True) * inv_sqrt
            if dil == 1:
                sc = jnp.where(l_idx >= t, sc - slope[None] * (float(rows + t) - l_f), NEG)
                new_ts = list(range(t + 1))
            else:
                sc = sc - (slope[None] * float(dil)) * (float(rows) - l_f)
                new_ts = [t]
            m = jnp.max(sc, axis=0)
            s_new = []
            for t2 in new_ts:
                k2 = qkv_ref[t2, base + H:base + 2 * H, :]
                s2 = jnp.sum(q * k2, axis=-1, keepdims=True) * inv_sqrt - slope * float((t - t2) * dil)
                s_new.append(s2)
                m = jnp.maximum(m, s2)
            ec = jnp.exp(sc - m[None])
            den = jnp.sum(ec, axis=0)
            acc = jnp.sum(ec * vc, axis=0)
            for t2, s2 in zip(new_ts, s_new):
                v2 = qkv_ref[t2, base + 2 * H:base + 3 * H, :]
                e2 = jnp.exp(s2 - m)
                den = den + e2
                acc = acc + e2 * v2
            outs.append(acc / den)
            lses.append(m + jnp.log(den))
        mm = jnp.maximum(jnp.maximum(lses[0], lses[1]), lses[2])
        es = [jnp.exp(l - mm) for l in lses]
        o_ref[t] = (es[0] * outs[0] + es[1] * outs[1] + es[2] * outs[2]) / (es[0] + es[1] + es[2])


def attn_sample(qkv, caches, layer, T, Bs):
    H = GROUP_HEADS
    n_rows = qkv.shape[-1] // HEAD_DIM
    views, specs = [], []
    for (win, dil), c in zip(ATTN_GROUPS, caches):
        assert c.shape[1] == Bs and c.shape[2] == win and win // dil == ATTN_BLK and (dil == 1 or T <= dil)
        phases = min(dil, -(-T // 4) * 4)
        views.append(c.reshape(c.shape[0], Bs, win // dil, dil * 2 * H, HEAD_DIM))
        specs.append(pl.BlockSpec((None, None, win // dil, phases * 2 * H, HEAD_DIM), lambda b: (layer, b, 0, 0, 0)))
    slopes = jnp.broadcast_to(jnp.asarray(_alibi_slopes(), F32)[:, :, None], (N_GROUPS, H, HEAD_DIM))
    q_rows = jnp.swapaxes(qkv.reshape(T, Bs, n_rows, HEAD_DIM), 0, 1)
    out = pl.pallas_call(
        _attn_sample_kernel,
        grid=(Bs,),
        in_specs=[pl.BlockSpec((N_GROUPS, H, HEAD_DIM), lambda b: (0, 0, 0)),
                  pl.BlockSpec((None, T, n_rows, HEAD_DIM), lambda b: (b, 0, 0, 0)), *specs],
        out_specs=pl.BlockSpec((None, T, H, HEAD_DIM), lambda b: (b, 0, 0, 0)),
        out_shape=jax.ShapeDtypeStruct((Bs, T, H, HEAD_DIM), F32),
        compiler_params=_params(1),
        name="attn_sample",
    )(slopes, q_rows, *views)
    return jnp.swapaxes(out, 0, 1).reshape(T * Bs, ATTN_WIDTH)


class Tiles(NamedTuple):
    seq: int
    mm: int
    ffn: int
    tf: int


def _run_trunk(x, nseq, R, tiles, lru_h, lru_conv, kv_caches, ffn_conv, P, W, dims):
    depth = P["norm_mix"].shape[0]
    new_h, new_lconv, new_fconv, qkvs = [], [], [], []
    bf16_w = {name: [] for name in W}
    for layer in range(depth):
        j = layer // 2
        if layer % 2 == 0:
            proj, wb = norm_matmul(x, P["norm_mix"], layer, W["lru_w_in"][j], P["lru_b_in"], j, tm=tiles.mm)
            bf16_w["lru_w_in"].append(wb)
            hg, c_rows, h_rows = lru_core(proj, P["lru_conv_w"], P["lru_conv_b"], P["lru_w_a"], P["lru_b_a"],
                                          P["lru_w_i"], P["lru_b_i"], P["lru_lambda"], lru_conv[j], lru_h[j],
                                          layer=j, tm=tiles.seq, R=R)
            x, wb = matmul_res(hg, W["lru_w_out"][j], P["lru_b_out"], j, x, tm=tiles.mm)
            bf16_w["lru_w_out"].append(wb)
            new_h.append(h_rows)
            new_lconv.append(c_rows)
        else:
            prompt = kv_caches is None
            qkv, wb = norm_matmul(x, P["norm_mix"], layer, W["attn_w_qkv"][j], P["attn_b_zero"], j, tm=tiles.mm,
                                  slab_out=prompt)
            bf16_w["attn_w_qkv"].append(wb)
            qkvs.append(qkv)
            if prompt:
                o = attn_prompt(qkv, *dims)
            else:
                o = attn_sample(qkv, kv_caches, layer=j, T=dims[0], Bs=dims[1])
            x, wb = matmul_res(o, W["attn_w_o"][j], P["attn_bo_zero"], j, x, tm=tiles.mm)
            bf16_w["attn_w_o"].append(wb)
        x, f_rows, wbs = conv_ffn(x, P["norm_ffn"], W["ffn_w_gate"][layer], W["ffn_w_val"][layer], P["ffn_conv_w"],
                                  P["ffn_conv_b"], W["ffn_w_down"][layer], ffn_conv[layer], layer=layer,
                                  tm=tiles.ffn, tf=tiles.tf, R=R,
                                  final_gain=P["norm_final"] if layer == depth - 1 else None)
        for name, wb in zip(("ffn_w_gate", "ffn_w_val", "ffn_w_down"), wbs or (None,) * 3):
            bf16_w[name].append(wb)
        new_fconv.append(f_rows)
    return x, new_h, new_lconv, qkvs, new_fconv, bf16_w


def kernel(x_prompt, x_sample, cache_kv_w128, cache_kv_w512, cache_kv_w2048, state_lru_h, state_lru_conv, state_ffn_conv, norm_mix, norm_ffn, norm_final, lru_w_in, lru_b_in, lru_conv_w, lru_conv_b, lru_w_a, lru_b_a, lru_w_i, lru_b_i, lru_lambda, lru_w_out, lru_b_out, attn_w_qkv, attn_w_o, ffn_w_up, ffn_conv_w, ffn_conv_b, ffn_w_down):
    B, S, D = x_prompt.shape
    Bs, T, _ = x_sample.shape
    depth = norm_mix.shape[0]
    n_lru, W = lru_lambda.shape
    n_attn = attn_w_qkv.shape[0]
    F2 = ffn_w_up.shape[-1]
    assert Bs == SUBLANES, "the sample group is laid out time-major with one sublane per sequence"
    row3 = lambda a: a.reshape(a.shape[0], 1, a.shape[-1])
    P = {
        "norm_mix": row3(norm_mix), "norm_ffn": row3(norm_ffn), "norm_final": norm_final.reshape(1, D),
        "lru_b_in": row3(lru_b_in), "lru_conv_w": lru_conv_w, "lru_conv_b": row3(lru_conv_b),
        "lru_w_a": lru_w_a, "lru_b_a": row3(lru_b_a), "lru_w_i": lru_w_i, "lru_b_i": row3(lru_b_i),
        "lru_lambda": row3(lru_lambda), "lru_b_out": row3(lru_b_out),
        "attn_b_zero": jnp.zeros((n_attn, 1, attn_w_qkv.shape[-1]), F32),
        "attn_bo_zero": jnp.zeros((n_attn, 1, D), F32),
        "ffn_conv_w": ffn_conv_w, "ffn_conv_b": row3(ffn_conv_b),
    }
    stack = lambda w: [Weight(w, i) for i in range(w.shape[0])]
    W_f32 = {
        "lru_w_in": stack(lru_w_in), "lru_w_out": stack(lru_w_out),
        "attn_w_qkv": stack(attn_w_qkv), "attn_w_o": stack(attn_w_o),
        "ffn_w_gate": [(w, 0) for w in stack(ffn_w_up)], "ffn_w_val": [(w, F2 // 2) for w in stack(ffn_w_up)],
        "ffn_w_down": stack(ffn_w_down),
    }
    kc = lru_conv_w.shape[1] - 1
    kf = ffn_conv_w.shape[1] - 1

    tmaj = lambda a: jnp.swapaxes(a, 0, 1).reshape(1, a.shape[1] * a.shape[0], a.shape[-1])
    y_s, h_s, lc_s, qkv_s, fc_s, bf16_w = _run_trunk(
        jnp.swapaxes(x_sample, 0, 1).reshape(T * Bs, D), 1, Bs,
        Tiles(seq=T * Bs, mm=T * Bs, ffn=T * Bs, tf=_pick(F2 // 2, FFN_TILE_FEW_ROWS)),
        [state_lru_h[j].reshape(1, Bs, W) for j in range(n_lru)],
        [tmaj(state_lru_conv[j]) for j in range(n_lru)],
        (cache_kv_w128, cache_kv_w512, cache_kv_w2048),
        [tmaj(state_ffn_conv[l]) for l in range(depth)], P, W_f32, (T, Bs))

    W_bf16 = {name: [(Weight(w, 0), 0) if name in ("ffn_w_gate", "ffn_w_val") else Weight(w, 0) for w in ws]
              for name, ws in bf16_w.items()}

    row_tile = lambda cap: cap if S % cap == 0 else S
    tiles_p = Tiles(seq=row_tile(ROW_TILE_LRU), mm=row_tile(ROW_TILE_MATMUL), ffn=row_tile(ROW_TILE_MATMUL),
                    tf=_pick(F2 // 2, FFN_TILE))
    zeros = lambda n, w: [jnp.zeros((B, SUBLANES, w), F32)] * n
    y_p, h_p, lc_p, qkv_p, fc_p, _ = _run_trunk(
        x_prompt.reshape(B * S, D), B, 1, tiles_p, zeros(n_lru, W), zeros(n_lru, W), None, zeros(depth, F2), P, W_bf16, (B, S))

    bmaj = lambda a, k: jnp.swapaxes(a.reshape(-1, Bs, a.shape[-1])[-k:], 0, 1)
    kv_p, kv_s = [], []
    for g, (win, dil) in enumerate(ATTN_GROUPS):
        lo = (g * 3 + 1) * ATTN_WIDTH
        keep = min(win, S)
        kv_p.append(kv_tails(qkv_p, g, B, S, keep))
        kv_s.append(jnp.stack([jnp.swapaxes(q.reshape(T, Bs, -1), 0, 1)[:, :, lo:lo + 2 * ATTN_WIDTH]
                               .reshape(Bs, T, 2, GROUP_HEADS, HEAD_DIM) for q in qkv_s], axis=0))
    return (
        y_p.reshape(B, S, D),
        jnp.swapaxes(y_s.reshape(T, Bs, D), 0, 1),
        kv_p[0], kv_p[1], kv_p[2],
        jnp.stack([h[:, SUBLANES - 1] for h in h_p], axis=0),
        jnp.stack([c[:, SUBLANES - kc:] for c in lc_p], axis=0),
        jnp.stack([f[:, SUBLANES - kf:] for f in fc_p], axis=0),
        kv_s[0], kv_s[1], kv_s[2],
        jnp.stack([h[0] for h in h_s], axis=0),
        jnp.stack([bmaj(c[0], kc) for c in lc_s], axis=0),
        jnp.stack([bmaj(f[0], kf) for f in fc_s], axis=0),
    )
```

```python
import functools
import math
from typing import NamedTuple

import jax
import jax.numpy as jnp
from jax import lax
from jax.experimental import pallas as pl
from jax.experimental.pallas import tpu as pltpu

EPS = 1e-6
NEG = -1e30
LRU_C = 8.0
ATTN_GROUPS = ((128, 1), (512, 4), (2048, 16))
N_GROUPS = len(ATTN_GROUPS)
GROUP_HEADS = 8
HEAD_DIM = 128
ATTN_WIDTH = GROUP_HEADS * HEAD_DIM
ATTN_BLK = 128
SUBLANES = 8
LANES = 128
VMEM_LIMIT = 56 * 1024 * 1024

ROW_TILE_MATMUL = 1024
ROW_TILE_LRU = 512
COL_TILE_MATMUL = 1536
COL_TILE_FEW_ROWS = 2048
COL_TILE_F32_WEIGHT = 1024
FEW_ROWS = 256
LRU_CHANNEL_TILE = 1024
FFN_TILE = 512
FFN_TILE_FEW_ROWS = 1024
FFN_TILE_F32_WEIGHT = 512
KV_TAIL_ROWS = 256

F32 = jnp.float32
F32_TINY = 1.1754944e-38
BF16 = jnp.bfloat16


def _alibi_slopes():
    n = N_GROUPS * GROUP_HEADS
    return [[2.0 ** (-8.0 * (g * GROUP_HEADS + h + 1) / n) for h in range(GROUP_HEADS)] for g in range(N_GROUPS)]


def _gelu(x):
    c = math.sqrt(2.0 / math.pi)
    return x * (0.5 * (1.0 + jnp.tanh(c * (x + 0.044715 * (x * x * x)))))


def _rms(x, g):
    ms = jnp.mean(x * x, axis=-1, keepdims=True)
    return x * lax.rsqrt(ms + EPS) * g


def _pick(n, cap):
    best = None
    for t in range(LANES, min(n, cap) + 1, LANES):
        if n % t == 0:
            best = t
    assert best is not None, (n, cap)
    return best


def _params(n_axes):
    return pltpu.CompilerParams(dimension_semantics=("arbitrary",) * n_axes, vmem_limit_bytes=VMEM_LIMIT)


def _as_bf16_weight(w_ref, emit_refs):
    w = w_ref[...]
    if w.dtype != BF16:
        w = w.astype(BF16)
        emit_refs[0][...] = w
    return w


def _norm_matmul_kernel(x_ref, g_ref, w_ref, b_ref, o_ref, *rest):
    xn_ref = rest[-1]

    @pl.when(pl.program_id(1) == 0)
    def _():
        xn_ref[...] = _rms(x_ref[...], g_ref[...]).astype(BF16)

    acc = jnp.dot(xn_ref[...], _as_bf16_weight(w_ref, rest), preferred_element_type=F32) + b_ref[...]
    if len(o_ref.shape) == 2:
        o_ref[...] = acc
    else:
        for c in range(o_ref.shape[0]):
            o_ref[c] = acc[:, c * LANES:(c + 1) * LANES]


class Weight(NamedTuple):
    arr: jax.Array
    layer: int


def _weight_specs(w, block, index):
    spec = pl.BlockSpec((None, *block), lambda *ij: (w.layer, *index(*ij)))
    if w.arr.dtype == BF16:
        return spec, [], []
    return (spec, [pl.BlockSpec((None, *block), lambda *ij: (0, *index(*ij)))],
            [jax.ShapeDtypeStruct((1, *w.arr.shape[1:]), BF16)])


def _tile_cap(tm, w):
    if w.arr.dtype != BF16:
        return COL_TILE_F32_WEIGHT
    return COL_TILE_MATMUL if tm > FEW_ROWS else COL_TILE_FEW_ROWS


def norm_matmul(x, gains, g_layer, w, bias, b_layer, tm, slab_out=False):
    M, D = x.shape
    N = w.arr.shape[-1]
    tn = _pick(N, _tile_cap(tm, w))
    if slab_out:
        out_spec = pl.BlockSpec((tn // LANES, tm, LANES), lambda i, j: (j, i, 0))
        out_shape = jax.ShapeDtypeStruct((N // LANES, M, LANES), F32)
    else:
        out_spec = pl.BlockSpec((tm, tn), lambda i, j: (i, j))
        out_shape = jax.ShapeDtypeStruct((M, N), F32)
    w_spec, emit_specs, emit_shapes = _weight_specs(w, (D, tn), lambda i, j: (0, j))
    assert not emit_specs or M == tm, "a weight tile must be visited once to be emitted"
    out, *emitted = pl.pallas_call(
        _norm_matmul_kernel,
        grid=(M // tm, N // tn),
        in_specs=[
            pl.BlockSpec((tm, D), lambda i, j: (i, 0)),
            pl.BlockSpec((None, 1, D), lambda i, j: (g_layer, 0, 0)),
            w_spec,
            pl.BlockSpec((None, 1, tn), lambda i, j: (b_layer, 0, j)),
        ],
        out_specs=[out_spec, *emit_specs],
        out_shape=[out_shape, *emit_shapes],
        scratch_shapes=[pltpu.VMEM((tm, D), BF16)],
        compiler_params=_params(2),
        name="norm_matmul",
    )(x, gains, w.arr, bias)
    return out, (emitted[0] if emitted else None)


def _matmul_res_kernel(a_ref, w_ref, b_ref, r_ref, o_ref, *emit):
    if len(a_ref.shape) == 2:
        a = a_ref[...]
    else:
        a = jnp.concatenate([a_ref[c] for c in range(a_ref.shape[0])], axis=1)
    o_ref[...] = r_ref[...] + b_ref[...] + jnp.dot(a.astype(BF16), _as_bf16_weight(w_ref, emit),
                                                   preferred_element_type=F32)


def matmul_res(a, w, bias, b_layer, res, tm):
    M, N = res.shape
    K = w.arr.shape[1]
    tn = _pick(N, _tile_cap(tm, w))
    if a.ndim == 2:
        a_spec = pl.BlockSpec((tm, K), lambda i, j: (i, 0))
    else:
        a_spec = pl.BlockSpec((K // LANES, tm, LANES), lambda i, j: (0, i, 0))
    w_spec, emit_specs, emit_shapes = _weight_specs(w, (K, tn), lambda i, j: (0, j))
    assert not emit_specs or M == tm, "a weight tile must be visited once to be emitted"
    out, *emitted = pl.pallas_call(
        _matmul_res_kernel,
        grid=(M // tm, N // tn),
        in_specs=[
            a_spec,
            w_spec,
            pl.BlockSpec((None, 1, tn), lambda i, j: (b_layer, 0, j)),
            pl.BlockSpec((tm, tn), lambda i, j: (i, j)),
        ],
        out_specs=[pl.BlockSpec((tm, tn), lambda i, j: (i, j)), *emit_specs],
        out_shape=[jax.ShapeDtypeStruct((M, N), F32), *emit_shapes],
        compiler_params=_params(2),
        name="matmul_res",
    )(a, w.arr, bias, res)
    return out, (emitted[0] if emitted else None)


def _lru_kernel(gate_ref, u_ref, cw_ref, cb_ref, wa_ref, ba_ref, wi_ref, bi_ref, lam_ref, ci_ref, hi_ref,
                o_ref, cs_ref, hs_ref,
                e_ref, a_scr, b_scr, h_scr, cc_ref, hc_ref, *, R, cru, tps):
    i = pl.program_id(0)
    c = pl.program_id(1)
    tm, tc = u_ref.shape

    @pl.when(i % tps == 0)
    def _():
        cc_ref[c] = ci_ref[...]
        hc_ref[c] = hi_ref[...]

    u = u_ref[...]
    e_ref[0:cru, :] = cc_ref[c]
    e_ref[cru:cru + tm, :] = u
    cw = cw_ref[...]
    uc = (cb_ref[...] + cw[3:4] * u + cw[2:3] * e_ref[cru - R:cru - R + tm, :]
          + cw[1:2] * e_ref[cru - 2 * R:cru - 2 * R + tm, :] + cw[0:1] * e_ref[cru - 3 * R:cru - 3 * R + tm, :])
    tail = e_ref[tm:tm + cru, :]
    cc_ref[c] = tail
    cs_ref[c] = tail

    hd = wa_ref.shape[-1]
    ucb = uc.astype(BF16)
    ra, ri = [], []
    for hh in range(tc // hd):
        ub = ucb[:, hh * hd:(hh + 1) * hd]
        ra.append(jnp.dot(ub, wa_ref[hh].astype(BF16), preferred_element_type=F32))
        ri.append(jnp.dot(ub, wi_ref[hh].astype(BF16), preferred_element_type=F32))
    r = jax.nn.sigmoid(jnp.concatenate(ra, axis=1) + ba_ref[...])
    ig = jax.nn.sigmoid(jnp.concatenate(ri, axis=1) + bi_ref[...])
    nlam = -lam_ref[...]
    softplus = jnp.maximum(nlam, 0.0) + jnp.log1p(jnp.exp(-jnp.abs(nlam)))
    log_a = (-LRU_C) * r * softplus
    a = jnp.exp(log_a)
    th = jnp.tanh(log_a)
    z = (-2.0 * th) / (1.0 - th)
    bx = (z * lax.rsqrt(jnp.maximum(z, F32_TINY))) * (ig * uc)

    if R == 1:
        A = a.reshape(tm // SUBLANES, SUBLANES, tc)
        B = bx.reshape(tm // SUBLANES, SUBLANES, tc)
        row = lax.broadcasted_iota(jnp.int32, A.shape, 1)
        s = 1
        while s < SUBLANES:
            m = row >= s
            B = jnp.where(m, A * pltpu.roll(B, s, axis=1) + B, B)
            A = jnp.where(m, A * pltpu.roll(A, s, axis=1), A)
            s *= 2
        a_scr[...] = A.reshape(tm, tc)
        b_scr[...] = B.reshape(tm, tc)
        h0 = hc_ref[c][SUBLANES - 1:SUBLANES, :]
    else:
        assert R == SUBLANES
        a_scr[...] = a
        b_scr[...] = bx
        h0 = hc_ref[c]

    def body(g, h):
        r0 = pl.multiple_of(g * SUBLANES, SUBLANES)
        hg = b_scr[pl.ds(r0, SUBLANES), :] + a_scr[pl.ds(r0, SUBLANES), :] * h
        h_scr[pl.ds(r0, SUBLANES), :] = hg
        return hg[SUBLANES - 1:SUBLANES, :] if R == 1 else hg

    n_groups = tm // SUBLANES
    lax.fori_loop(0, n_groups, body, h0, unroll=min(n_groups, 8))
    h_tail = h_scr[tm - SUBLANES:tm, :]
    hc_ref[c] = h_tail
    hs_ref[c] = h_tail
    o_ref[...] = (h_scr[...] * _gelu(gate_ref[...])).astype(BF16)


def lru_core(proj, conv_w, conv_b, w_a, b_a, w_i, b_i, lam, conv_init, h_init, layer, tm, R):
    M, W2 = proj.shape
    W = W2 // 2
    hd = w_a.shape[-1]
    tc = max(hd, _pick(W, LRU_CHANNEL_TILE))
    assert tc % hd == 0 and W % tc == 0
    nc = W // tc
    nseq, cru, _ = conv_init.shape
    tps = (M // tm) // nseq
    kern = functools.partial(_lru_kernel, R=R, cru=cru, tps=tps)
    row_vec = lambda: pl.BlockSpec((None, 1, tc), lambda i, c: (layer, 0, c))
    gate_w = lambda: pl.BlockSpec((None, tc // hd, hd, hd), lambda i, c: (layer, c, 0, 0))
    unfold = lambda s: jnp.swapaxes(s, 1, 2).reshape(nseq, s.shape[2], W)
    hg, c_rows, h_rows = pl.pallas_call(
        kern,
        grid=(M // tm, nc),
        in_specs=[
            pl.BlockSpec((tm, tc), lambda i, c: (i, c)),
            pl.BlockSpec((tm, tc), lambda i, c: (i, nc + c)),
            pl.BlockSpec((None, conv_w.shape[1], tc), lambda i, c: (layer, 0, c)),
            row_vec(), gate_w(), row_vec(), gate_w(), row_vec(), row_vec(),
            pl.BlockSpec((None, cru, tc), lambda i, c: (i // tps, 0, c)),
            pl.BlockSpec((None, SUBLANES, tc), lambda i, c: (i // tps, 0, c)),
        ],
        out_specs=[
            pl.BlockSpec((tm, tc), lambda i, c: (i, c)),
            pl.BlockSpec((None, nc, cru, tc), lambda i, c: (i // tps, 0, 0, 0)),
            pl.BlockSpec((None, nc, SUBLANES, tc), lambda i, c: (i // tps, 0, 0, 0)),
        ],
        out_shape=[
            jax.ShapeDtypeStruct((M, W), BF16),
            jax.ShapeDtypeStruct((nseq, nc, cru, tc), F32),
            jax.ShapeDtypeStruct((nseq, nc, SUBLANES, tc), F32),
        ],
        scratch_shapes=[
            pltpu.VMEM((cru + tm, tc), F32),
            pltpu.VMEM((tm, tc), F32),
            pltpu.VMEM((tm, tc), F32),
            pltpu.VMEM((tm, tc), F32),
            pltpu.VMEM((nc, cru, tc), F32),
            pltpu.VMEM((nc, SUBLANES, tc), F32),
        ],
        compiler_params=_params(2),
        name="lru_core",
    )(proj, proj, conv_w, conv_b, w_a, b_a, w_i, b_i, lam, conv_init, h_init)
    return hg, unfold(c_rows), unfold(h_rows)


def _ffn_kernel(x_ref, g_ref, wg_ref, wv_ref, cwg_ref, cwv_ref, cbg_ref, cbv_ref, wd_ref, ig_ref, iv_ref, *rest,
                R, cr, tps, final):
    fg_ref = rest[0] if final else None
    o_ref, sg_ref, sv_ref, *emit, xn_ref, eg_ref, ev_ref, cg_ref, cv_ref = rest[1:] if final else rest
    i = pl.program_id(0)
    j = pl.program_id(1)
    tm = x_ref.shape[0]

    @pl.when(j == 0)
    def _():
        x = x_ref[...]
        xn_ref[...] = _rms(x, g_ref[...]).astype(BF16)
        o_ref[...] = x

    @pl.when(i % tps == 0)
    def _():
        cg_ref[j] = ig_ref[...]
        cv_ref[j] = iv_ref[...]

    def side(w_ref, cw_ref, cb_ref, e_ref, c_ref, s_ref, emit_ref):
        up = jnp.dot(xn_ref[...], _as_bf16_weight(w_ref, emit_ref), preferred_element_type=F32)
        e_ref[0:cr, :] = c_ref[j]
        e_ref[cr:cr + tm, :] = up
        cw = cw_ref[...]
        conv = (cb_ref[...] + cw[2:3] * up + cw[1:2] * e_ref[cr - R:cr - R + tm, :]
                + cw[0:1] * e_ref[cr - 2 * R:cr - 2 * R + tm, :])
        tail = e_ref[tm:tm + cr, :]
        c_ref[j] = tail
        s_ref[j] = tail
        return conv

    cg = side(wg_ref, cwg_ref, cbg_ref, eg_ref, cg_ref, sg_ref, emit[0:1])
    cv = side(wv_ref, cwv_ref, cbv_ref, ev_ref, cv_ref, sv_ref, emit[1:2])
    act = (_gelu(cg) * cv).astype(BF16)
    o_ref[...] += jnp.dot(act, _as_bf16_weight(wd_ref, emit[2:3]), preferred_element_type=F32)
    if final:
        @pl.when(j == pl.num_programs(1) - 1)
        def _():
            o_ref[...] = _rms(o_ref[...], fg_ref[...])


def conv_ffn(x, gains, w_gate, w_val, conv_w, conv_b, w_down, init, layer, tm, tf, R, final_gain=None):
    M, D = x.shape
    F = w_down.arr.shape[1]
    emitting = w_down.arr.dtype != BF16
    if emitting:
        tf = min(tf, FFN_TILE_F32_WEIGHT)
    nf = F // tf
    nseq, cr, _ = init.shape
    tps = (M // tm) // nseq
    final = final_gain is not None
    kern = functools.partial(_ffn_kernel, R=R, cr=cr, tps=tps, final=final)
    K = conv_w.shape[1]
    halves = lambda mk: [mk(0), mk(nf)]
    state_spec = lambda off: pl.BlockSpec((None, cr, tf), lambda i, j: (i // tps, 0, off + j))
    w_specs, emit_specs, emit_shapes = [], [], []
    for w, col0 in (w_gate, w_val):
        assert col0 % tf == 0 and (w.arr.dtype != BF16) == emitting
        spec, e_spec, e_shape = _weight_specs(w, (D, tf), lambda i, j, off=col0 // tf: (0, off + j))
        w_specs.append(spec)
        emit_specs += [pl.BlockSpec((None, D, tf), lambda i, j: (0, 0, j))] if e_spec else []
        emit_shapes += [jax.ShapeDtypeStruct((1, D, F), BF16)] if e_shape else []
    wd_spec, e_spec, e_shape = _weight_specs(w_down, (tf, D), lambda i, j: (j, 0))
    emit_specs += e_spec
    emit_shapes += e_shape
    assert not emit_specs or M == tm, "a weight tile must be visited once to be emitted"
    once = pl.Buffered(1)
    out, sg, sv, *emitted = pl.pallas_call(
        kern,
        grid=(M // tm, nf),
        in_specs=[
            pl.BlockSpec((tm, D), lambda i, j: (i, 0)),
            pl.BlockSpec((None, 1, D), lambda i, j: (layer, 0, 0)),
            *w_specs,
            *halves(lambda off: pl.BlockSpec((None, K, tf), lambda i, j: (layer, 0, off + j))),
            *halves(lambda off: pl.BlockSpec((None, 1, tf), lambda i, j: (layer, 0, off + j))),
            wd_spec,
            *halves(state_spec),
            *([pl.BlockSpec((1, D), lambda i, j: (0, 0))] if final else []),
        ],
        out_specs=[
            pl.BlockSpec((tm, D), lambda i, j: (i, 0), pipeline_mode=once),
            pl.BlockSpec((None, nf, cr, tf), lambda i, j: (i // tps, 0, 0, 0)),
            pl.BlockSpec((None, nf, cr, tf), lambda i, j: (i // tps, 0, 0, 0)),
            *emit_specs,
        ],
        out_shape=[
            jax.ShapeDtypeStruct((M, D), F32),
            jax.ShapeDtypeStruct((nseq, nf, cr, tf), F32),
            jax.ShapeDtypeStruct((nseq, nf, cr, tf), F32),
            *emit_shapes,
        ],
        scratch_shapes=[
            pltpu.VMEM((tm, D), BF16),
            pltpu.VMEM((cr + tm, tf), F32),
            pltpu.VMEM((cr + tm, tf), F32),
            pltpu.VMEM((nf, cr, tf), F32),
            pltpu.VMEM((nf, cr, tf), F32),
        ],
        compiler_params=_params(2),
        name="conv_ffn",
    )(x, gains, w_gate[0].arr, w_val[0].arr, conv_w, conv_w, conv_b, conv_b, w_down.arr, init, init,
      *([final_gain] if final else []))
    unfold = lambda s: jnp.swapaxes(s, 1, 2).reshape(nseq, cr, F)
    return out, jnp.concatenate([unfold(sg), unfold(sv)], axis=-1), (tuple(emitted) if emitted else None)


ATTN_SUPER = max(w for w, _ in ATTN_GROUPS)
ATTN_BATCH = 8
ATTN_TRANSPOSE_DIL = SUBLANES


def _attn_prompt_kernel(sl_ref, q0, k0, v0, q1, k1, v1, q2, k2, v2, o_ref,
                        kc0, vc0, kc1, vc1, kc2, vc2, o_scr, l_scr):
    n = pl.program_id(2)
    SB = o_ref.shape[0]
    blk = ATTN_BLK
    row = lax.broadcasted_iota(jnp.int32, (blk, 2 * blk), 0)
    col = lax.broadcasted_iota(jnp.int32, (blk, 2 * blk), 1)
    steps = row + blk - col
    band = jnp.logical_and(steps >= 0, steps <= blk)
    band_cur = jnp.logical_and(band, col >= blk)
    steps_f = steps.astype(F32)
    ones_v = jnp.ones((2 * blk, HEAD_DIM), BF16)
    inv_sqrt = 1.0 / math.sqrt(HEAD_DIM)
    nt = (((1,), (1,)), ((), ()))
    groups = zip(ATTN_GROUPS, (q0, q1, q2), (k0, k1, k2), (v0, v1, v2), (kc0, kc1, kc2), (vc0, vc1, vc2))
    for g, ((win, dil), q_ref, k_ref, v_ref, kcar, vcar) in enumerate(groups):
        @pl.when(n == 0)
        def _():
            kcar[...] = jnp.zeros(kcar.shape, kcar.dtype)
            vcar[...] = jnp.zeros(vcar.shape, vcar.dtype)

        alibi = (sl_ref[g:g + 1, :] * float(-dil)) * steps_f
        bias_full = jnp.where(band, alibi, NEG)
        bias_head = jnp.where(n > 0, bias_full, jnp.where(band_cur, alibi, NEG))

        def rows(start):
            return pl.ds(start, blk, stride=dil) if dil > 1 else pl.ds(start, blk)

        transposed = dil >= ATTN_TRANSPOSE_DIL
        if transposed:
            assert win == SB
            split = lambda ref: jnp.swapaxes(ref[...].astype(BF16).reshape(blk, dil, HEAD_DIM), 0, 1)
            q_ph, k_ph, v_ph = split(q_ref), split(k_ref), split(v_ref)
            k_prev, v_prev = kcar[...], vcar[...]
        o_parts, l_parts = [], []
        n_blocks = SB // blk
        for b0 in range(0, n_blocks, ATTN_BATCH):
            infos = []
            for bidx in range(b0, b0 + ATTN_BATCH):
                j, r = divmod(bidx, dil)
                infos.append((j, r, j * win + r))
            scores = []
            for j, r, p0 in infos:
                if transposed:
                    q, kp, kc = q_ph[r], k_prev[r], k_ph[r]
                else:
                    q = q_ref[rows(p0), :].astype(BF16)
                    kp = (kcar[rows(r), :] if j == 0 else k_ref[rows(p0 - win), :]).astype(BF16)
                    kc = k_ref[rows(p0), :].astype(BF16)
                scores.append(lax.dot_general(q, jnp.concatenate([kp, kc], axis=0), nt, preferred_element_type=F32))
            probs = []
            for (j, r, p0), s in zip(infos, scores):
                s = s * inv_sqrt + (bias_head if j == 0 else bias_full)
                m = jnp.max(jnp.maximum(s[:, :blk], s[:, blk:]), axis=-1, keepdims=True)
                probs.append((jnp.exp(s - m).astype(BF16), m))
            for (j, r, p0), (e, m) in zip(infos, probs):
                if transposed:
                    vp, vc = v_prev[r], v_ph[r]
                else:
                    vp = (vcar[rows(r), :] if j == 0 else v_ref[rows(p0 - win), :]).astype(BF16)
                    vc = v_ref[rows(p0), :].astype(BF16)
                vcat = jnp.concatenate([vp, vc], axis=0)
                res = jnp.dot(e, jnp.concatenate([vcat, ones_v], axis=1), preferred_element_type=F32)
                den = res[:, HEAD_DIM:]
                o_blk, l_blk = res[:, :HEAD_DIM] / den, m + jnp.log(den)
                if transposed:
                    o_parts.append(o_blk)
                    l_parts.append(l_blk)
                else:
                    o_scr[g, rows(p0), :] = o_blk
                    l_scr[g, rows(p0), :] = l_blk
        if transposed:
            merge = lambda parts: jnp.swapaxes(jnp.stack(parts, axis=0), 0, 1).reshape(SB, HEAD_DIM)
            o_scr[g] = merge(o_parts)
            l_scr[g] = merge(l_parts)
            kcar[...] = k_ph
            vcar[...] = v_ph
        else:
            kcar[...] = k_ref[SB - win:SB, :]
            vcar[...] = v_ref[SB - win:SB, :]

    chunk = 2 * blk
    for c0 in range(0, SB, chunk):
        ls = [l_scr[g, c0:c0 + chunk, :] for g in range(N_GROUPS)]
        mm = jnp.maximum(jnp.maximum(ls[0], ls[1]), ls[2])
        es = [jnp.exp(l - mm) for l in ls]
        acc = es[0] * o_scr[0, c0:c0 + chunk, :]
        for g in range(1, N_GROUPS):
            acc = acc + es[g] * o_scr[g, c0:c0 + chunk, :]
        o_ref[c0:c0 + chunk, :] = (acc / (es[0] + es[1] + es[2])).astype(BF16)


def attn_prompt(qkv_slabs, B, S):
    n_slabs, M, _ = qkv_slabs.shape
    SB = ATTN_SUPER
    assert n_slabs == 3 * N_GROUPS * GROUP_HEADS and S % SB == 0 and M == B * S
    assert all(w // d == ATTN_BLK for w, d in ATTN_GROUPS) and (SB // ATTN_BLK) % ATTN_BATCH == 0
    nsb = S // SB
    slopes = jnp.asarray(_alibi_slopes(), F32).T
    slopes = jnp.broadcast_to(slopes[:, :, None], (GROUP_HEADS, N_GROUPS, 2 * ATTN_BLK))

    def slab(g, comp):
        base = (g * 3 + comp) * GROUP_HEADS
        return pl.BlockSpec((None, SB, HEAD_DIM), lambda b, h, n: (base + h, b * nsb + n, 0))

    carry = [pltpu.VMEM((d, w // d, HEAD_DIM), BF16) if d >= ATTN_TRANSPOSE_DIL else pltpu.VMEM((w, HEAD_DIM), F32)
             for w, d in ATTN_GROUPS for _kv in range(2)]
    return pl.pallas_call(
        _attn_prompt_kernel,
        grid=(B, GROUP_HEADS, nsb),
        in_specs=[pl.BlockSpec((None, N_GROUPS, 2 * ATTN_BLK), lambda b, h, n: (h, 0, 0))]
        + [slab(g, comp) for g in range(N_GROUPS) for comp in range(3)],
        out_specs=pl.BlockSpec((None, SB, HEAD_DIM), lambda b, h, n: (h, b * nsb + n, 0)),
        out_shape=jax.ShapeDtypeStruct((GROUP_HEADS, M, HEAD_DIM), BF16),
        scratch_shapes=carry + [pltpu.VMEM((N_GROUPS, SB, HEAD_DIM), F32), pltpu.VMEM((N_GROUPS, SB, HEAD_DIM), F32)],
        compiler_params=_params(3),
        name="attn_prompt",
    )(slopes, *([qkv_slabs] * (3 * N_GROUPS)))


def _kv_tail_kernel(*refs):
    o_ref = refs[-1]
    for i, ref in enumerate(refs[:-1]):
        o_ref[i // 2, :, i % 2] = jnp.swapaxes(ref[...], 0, 1)


def kv_tails(qkv_layers, g, B, S, keep):
    chunk = min(keep, KV_TAIL_ROWS)
    assert keep % chunk == 0 and S % chunk == 0
    first = (S - keep) // chunk
    per_seq = S // chunk

    def slabs(comp):
        return pl.BlockSpec((GROUP_HEADS, chunk, HEAD_DIM), lambda b, t: (g * 3 + comp, b * per_seq + first + t, 0))

    n = len(qkv_layers)
    return pl.pallas_call(
        _kv_tail_kernel,
        grid=(B, keep // chunk),
        in_specs=[slabs(comp) for _ in range(n) for comp in (1, 2)],
        out_specs=pl.BlockSpec((n, None, chunk, 2, GROUP_HEADS, HEAD_DIM), lambda b, t: (0, b, t, 0, 0, 0)),
        out_shape=jax.ShapeDtypeStruct((n, B, keep, 2, GROUP_HEADS, HEAD_DIM), F32),
        compiler_params=_params(2),
        name="kv_tails",
    )(*[q for q in qkv_layers for _ in range(2)])


def _attn_sample_kernel(sl_ref, qkv_ref, c0_ref, c1_ref, c2_ref, o_ref):
    T = qkv_ref.shape[0]
    H = GROUP_HEADS
    caches = (c0_ref, c1_ref, c2_ref)
    rows = c0_ref.shape[0]
    l_idx = lax.broadcasted_iota(jnp.int32, (rows, H, 1), 0)
    l_f = l_idx.astype(F32)
    inv_sqrt = 1.0 / math.sqrt(HEAD_DIM)
    for t in range(T):
        outs, lses = [], []
        for g, (win, dil) in enumerate(ATTN_GROUPS):
            base = g * 3 * H
            slope = sl_ref[g][:, 0:1]
            q = qkv_ref[t, base:base + H, :]
            ph = 0 if dil == 1 else t
            kc = caches[g][:, ph * 2 * H:ph * 2 * H + H, :]
            vc = caches[g][:, ph * 2 * H + H:(ph + 1) * 2 * H, :]
            sc = jnp.sum(kc * q[None], axis=-1, keepdims=True) * inv_sqrt
            if dil == 1:
                sc = jnp.where(l_idx >= t, sc - slope[None] * (float(rows + t) - l_f), NEG)
                new_ts = list(range(t + 1))
            else:
                sc = sc - (slope[None] * float(dil)) * (float(rows) - l_f)
                new_ts = [t]
            m = jnp.max(sc, axis=0)
            s_new = []
            for t2 in new_ts:
                k2 = qkv_ref[t2, base + H:base + 2 * H, :]
                s2 = jnp.sum(q * k2, axis=-1, keepdims=True) * inv_sqrt - slope * float((t - t2) * dil)
                s_new.append(s2)
                m = jnp.maximum(m, s2)
            ec = jnp.exp(sc - m[None])
            den = jnp.sum(ec, axis=0)
            acc = jnp.sum(ec * vc, axis=0)
            for t2, s2 in zip(new_ts, s_new):
                v2 = qkv_ref[t2, base + 2 * H:base + 3 * H, :]
                e2 = jnp.exp(s2 - m)
                den = den + e2
                acc = acc + e2 * v2
            outs.append(acc / den)
            lses.append(m + jnp.log(den))
        mm = jnp.maximum(jnp.maximum(lses[0], lses[1]), lses[2])
        es = [jnp.exp(l - mm) for l in lses]
        o_ref[t] = (es[0] * outs[0] + es[1] * outs[1] + es[2] * outs[2]) / (es[0] + es[1] + es[2])


def attn_sample(qkv, caches, layer, T, Bs):
    H = GROUP_HEADS
    n_rows = qkv.shape[-1] // HEAD_DIM
    views, specs = [], []
    for (win, dil), c in zip(ATTN_GROUPS, caches):
        assert c.shape[1] == Bs and c.shape[2] == win and win // dil == ATTN_BLK and (dil == 1 or T <= dil)
        phases = min(dil, -(-T // 4) * 4)
        views.append(c.reshape(c.shape[0], Bs, win // dil, dil * 2 * H, HEAD_DIM))
        specs.append(pl.BlockSpec((None, None, win // dil, phases * 2 * H, HEAD_DIM), lambda b: (layer, b, 0, 0, 0)))
    slopes = jnp.broadcast_to(jnp.asarray(_alibi_slopes(), F32)[:, :, None], (N_GROUPS, H, HEAD_DIM))
    q_rows = jnp.swapaxes(qkv.reshape(T, Bs, n_rows, HEAD_DIM), 0, 1)
    out = pl.pallas_call(
        _attn_sample_kernel,
        grid=(Bs,),
        in_specs=[pl.BlockSpec((N_GROUPS, H, HEAD_DIM), lambda b: (0, 0, 0)),
                  pl.BlockSpec((None, T, n_rows, HEAD_DIM), lambda b: (b, 0, 0, 0)), *specs],
        out_specs=pl.BlockSpec((None, T, H, HEAD_DIM), lambda b: (b, 0, 0, 0)),
        out_shape=jax.ShapeDtypeStruct((Bs, T, H, HEAD_DIM), F32),
        compiler_params=_params(1),
        name="attn_sample",
    )(slopes, q_rows, *views)
    return jnp.swapaxes(out, 0, 1).reshape(T * Bs, ATTN_WIDTH)


class Tiles(NamedTuple):
    seq: int
    mm: int
    ffn: int
    tf: int


def _run_trunk(x, nseq, R, tiles, lru_h, lru_conv, kv_caches, ffn_conv, P, W, dims):
    depth = P["norm_mix"].shape[0]
    new_h, new_lconv, new_fconv, qkvs = [], [], [], []
    bf16_w = {name: [] for name in W}
    for layer in range(depth):
        j = layer // 2
        if layer % 2 == 0:
            proj, wb = norm_matmul(x, P["norm_mix"], layer, W["lru_w_in"][j], P["lru_b_in"], j, tm=tiles.mm)
            bf16_w["lru_w_in"].append(wb)
            hg, c_rows, h_rows = lru_core(proj, P["lru_conv_w"], P["lru_conv_b"], P["lru_w_a"], P["lru_b_a"],
                                          P["lru_w_i"], P["lru_b_i"], P["lru_lambda"], lru_conv[j], lru_h[j],
                                          layer=j, tm=tiles.seq, R=R)
            x, wb = matmul_res(hg, W["lru_w_out"][j], P["lru_b_out"], j, x, tm=tiles.mm)
            bf16_w["lru_w_out"].append(wb)
            new_h.append(h_rows)
            new_lconv.append(c_rows)
        else:
            prompt = kv_caches is None
            qkv, wb = norm_matmul(x, P["norm_mix"], layer, W["attn_w_qkv"][j], P["attn_b_zero"], j, tm=tiles.mm,
                                  slab_out=prompt)
            bf16_w["attn_w_qkv"].append(wb)
            qkvs.append(qkv)
            if prompt:
                o = attn_prompt(qkv, *dims)
            else:
                o = attn_sample(qkv, kv_caches, layer=j, T=dims[0], Bs=dims[1])
            x, wb = matmul_res(o, W["attn_w_o"][j], P["attn_bo_zero"], j, x, tm=tiles.mm)
            bf16_w["attn_w_o"].append(wb)
        x, f_rows, wbs = conv_ffn(x, P["norm_ffn"], W["ffn_w_gate"][layer], W["ffn_w_val"][layer], P["ffn_conv_w"],
                                  P["ffn_conv_b"], W["ffn_w_down"][layer], ffn_conv[layer], layer=layer,
                                  tm=tiles.ffn, tf=tiles.tf, R=R,
                                  final_gain=P["norm_final"] if layer == depth - 1 else None)
        for name, wb in zip(("ffn_w_gate", "ffn_w_val", "ffn_w_down"), wbs or (None,) * 3):
            bf16_w[name].append(wb)
        new_fconv.append(f_rows)
    return x, new_h, new_lconv, qkvs, new_fconv, bf16_w


def kernel(x_prompt, x_sample, cache_kv_w128, cache_kv_w512, cache_kv_w2048, state_lru_h, state_lru_conv, state_ffn_conv, norm_mix, norm_ffn, norm_final, lru_w_in, lru_b_in, lru_conv_w, lru_conv_b, lru_w_a, lru_b_a, lru_w_i, lru_b_i, lru_lambda, lru_w_out, lru_b_out, attn_w_qkv, attn_w_o, ffn_w_up, ffn_conv_w, ffn_conv_b, ffn_w_down):
    B, S, D = x_prompt.shape
    Bs, T, _ = x_sample.shape
    depth = norm_mix.shape[0]
    n_lru, W = lru_lambda.shape
    n_attn = attn_w_qkv.shape[0]
    F2 = ffn_w_up.shape[-1]
    assert Bs == SUBLANES, "the sample group is laid out time-major with one sublane per sequence"
    row3 = lambda a: a.reshape(a.shape[0], 1, a.shape[-1])
    P = {
        "norm_mix": row3(norm_mix), "norm_ffn": row3(norm_ffn), "norm_final": norm_final.reshape(1, D),
        "lru_b_in": row3(lru_b_in), "lru_conv_w": lru_conv_w, "lru_conv_b": row3(lru_conv_b),
        "lru_w_a": lru_w_a, "lru_b_a": row3(lru_b_a), "lru_w_i": lru_w_i, "lru_b_i": row3(lru_b_i),
        "lru_lambda": row3(lru_lambda), "lru_b_out": row3(lru_b_out),
        "attn_b_zero": jnp.zeros((n_attn, 1, attn_w_qkv.shape[-1]), F32),
        "attn_bo_zero": jnp.zeros((n_attn, 1, D), F32),
        "ffn_conv_w": ffn_conv_w, "ffn_conv_b": row3(ffn_conv_b),
    }
    stack = lambda w: [Weight(w, i) for i in range(w.shape[0])]
    W_f32 = {
        "lru_w_in": stack(lru_w_in), "lru_w_out": stack(lru_w_out),
        "attn_w_qkv": stack(attn_w_qkv), "attn_w_o": stack(attn_w_o),
        "ffn_w_gate": [(w, 0) for w in stack(ffn_w_up)], "ffn_w_val": [(w, F2 // 2) for w in stack(ffn_w_up)],
        "ffn_w_down": stack(ffn_w_down),
    }
    kc = lru_conv_w.shape[1] - 1
    kf = ffn_conv_w.shape[1] - 1

    tmaj = lambda a: jnp.swapaxes(a, 0, 1).reshape(1, a.shape[1] * a.shape[0], a.shape[-1])
    y_s, h_s, lc_s, qkv_s, fc_s, bf16_w = _run_trunk(
        jnp.swapaxes(x_sample, 0, 1).reshape(T * Bs, D), 1, Bs,
        Tiles(seq=T * Bs, mm=T * Bs, ffn=T * Bs, tf=_pick(F2 // 2, FFN_TILE_FEW_ROWS)),
        [state_lru_h[j].reshape(1, Bs, W) for j in range(n_lru)],
        [tmaj(state_lru_conv[j]) for j in range(n_lru)],
        (cache_kv_w128, cache_kv_w512, cache_kv_w2048),
        [tmaj(state_ffn_conv[l]) for l in range(depth)], P, W_f32, (T, Bs))

    W_bf16 = {name: [(Weight(w, 0), 0) if name in ("ffn_w_gate", "ffn_w_val") else Weight(w, 0) for w in ws]
              for name, ws in bf16_w.items()}

    row_tile = lambda cap: cap if S % cap == 0 else S
    tiles_p = Tiles(seq=row_tile(ROW_TILE_LRU), mm=row_tile(ROW_TILE_MATMUL), ffn=row_tile(ROW_TILE_MATMUL),
                    tf=_pick(F2 // 2, FFN_TILE))
    zeros = lambda n, w: [jnp.zeros((B, SUBLANES, w), F32)] * n
    y_p, h_p, lc_p, qkv_p, fc_p, _ = _run_trunk(
        x_prompt.reshape(B * S, D), B, 1, tiles_p, zeros(n_lru, W), zeros(n_lru, W), None, zeros(depth, F2), P, W_bf16, (B, S))

    bmaj = lambda a, k: jnp.swapaxes(a.reshape(-1, Bs, a.shape[-1])[-k:], 0, 1)
    kv_p, kv_s = [], []
    for g, (win, dil) in enumerate(ATTN_GROUPS):
        lo = (g * 3 + 1) * ATTN_WIDTH
        keep = min(win, S)
        kv_p.append(kv_tails(qkv_p, g, B, S, keep))
        kv_s.append(jnp.stack([jnp.swapaxes(q.reshape(T, Bs, -1), 0, 1)[:, :, lo:lo + 2 * ATTN_WIDTH]
                               .reshape(Bs, T, 2, GROUP_HEADS, HEAD_DIM) for q in qkv_s], axis=0))
    return (
        y_p.reshape(B, S, D),
        jnp.swapaxes(y_s.reshape(T, Bs, D), 0, 1),
        kv_p[0], kv_p[1], kv_p[2],
        jnp.stack([h[:, SUBLANES - 1] for h in h_p], axis=0),
        jnp.stack([c[:, SUBLANES - kc:] for c in lc_p], axis=0),
        jnp.stack([f[:, SUBLANES - kf:] for f in fc_p], axis=0),
        kv_s[0], kv_s[1], kv_s[2],
        jnp.stack([h[0] for h in h_s], axis=0),
        jnp.stack([bmaj(c[0], kc) for c in lc_s], axis=0),
        jnp.stack([bmaj(f[0], kf) for f in fc_s], axis=0),
    )
```

```python
import functools
import math
from typing import NamedTuple

import jax
import jax.numpy as jnp
from jax import lax
from jax.experimental import pallas as pl
from jax.experimental.pallas import tpu as pltpu

EPS = 1e-6
NEG = -1e30
LRU_C = 8.0
ATTN_GROUPS = ((128, 1), (512, 4), (2048, 16))
N_GROUPS = len(ATTN_GROUPS)
GROUP_HEADS = 8
HEAD_DIM = 128
ATTN_WIDTH = GROUP_HEADS * HEAD_DIM
ATTN_BLK = 128
SUBLANES = 8
LANES = 128
VMEM_LIMIT = 56 * 1024 * 1024

ROW_TILE_MATMUL = 1024
ROW_TILE_LRU = 1024
COL_TILE_MATMUL = 1536
COL_TILE_FEW_ROWS = 2048
COL_TILE_F32_WEIGHT = 1024
FEW_ROWS = 256
LRU_CHANNEL_TILE = 1024
FFN_TILE = 512
FFN_TILE_FEW_ROWS = 1024
FFN_TILE_F32_WEIGHT = 512
KV_TAIL_ROWS = 256

F32 = jnp.float32
F32_TINY = 1.1754944e-38
BF16 = jnp.bfloat16


def _alibi_slopes():
    n = N_GROUPS * GROUP_HEADS
    return [[2.0 ** (-8.0 * (g * GROUP_HEADS + h + 1) / n) for h in range(GROUP_HEADS)] for g in range(N_GROUPS)]


def _gelu(x):
    c = math.sqrt(2.0 / math.pi)
    return x * (0.5 * (1.0 + jnp.tanh(c * (x + 0.044715 * (x * x * x)))))


def _rms(x, g):
    ms = jnp.mean(x * x, axis=-1, keepdims=True)
    return x * lax.rsqrt(ms + EPS) * g


def _pick(n, cap):
    best = None
    for t in range(LANES, min(n, cap) + 1, LANES):
        if n % t == 0:
            best = t
    assert best is not None, (n, cap)
    return best


def _params(n_axes):
    return pltpu.CompilerParams(dimension_semantics=("arbitrary",) * n_axes, vmem_limit_bytes=VMEM_LIMIT)


def _as_bf16_weight(w_ref, emit_refs):
    w = w_ref[...]
    if w.dtype != BF16:
        w = w.astype(BF16)
        emit_refs[0][...] = w
    return w


def _norm_matmul_kernel(x_ref, g_ref, w_ref, b_ref, o_ref, *rest):
    xn_ref = rest[-1]

    @pl.when(pl.program_id(1) == 0)
    def _():
        xn_ref[...] = _rms(x_ref[...], g_ref[...]).astype(BF16)

    acc = jnp.dot(xn_ref[...], _as_bf16_weight(w_ref, rest), preferred_element_type=F32) + b_ref[...]
    if len(o_ref.shape) == 2:
        o_ref[...] = acc
    else:
        for c in range(o_ref.shape[0]):
            o_ref[c] = acc[:, c * LANES:(c + 1) * LANES]


class Weight(NamedTuple):
    arr: jax.Array
    layer: int


def _weight_specs(w, block, index):
    spec = pl.BlockSpec((None, *block), lambda *ij: (w.layer, *index(*ij)))
    if w.arr.dtype == BF16:
        return spec, [], []
    return (spec, [pl.BlockSpec((None, *block), lambda *ij: (0, *index(*ij)))],
            [jax.ShapeDtypeStruct((1, *w.arr.shape[1:]), BF16)])


def _tile_cap(tm, w):
    if w.arr.dtype != BF16:
        return COL_TILE_F32_WEIGHT
    return COL_TILE_MATMUL if tm > FEW_ROWS else COL_TILE_FEW_ROWS


def norm_matmul(x, gains, g_layer, w, bias, b_layer, tm, slab_out=False):
    M, D = x.shape
    N = w.arr.shape[-1]
    tn = _pick(N, _tile_cap(tm, w))
    if slab_out:
        out_spec = pl.BlockSpec((tn // LANES, tm, LANES), lambda i, j: (j, i, 0))
        out_shape = jax.ShapeDtypeStruct((N // LANES, M, LANES), F32)
    else:
        out_spec = pl.BlockSpec((tm, tn), lambda i, j: (i, j))
        out_shape = jax.ShapeDtypeStruct((M, N), F32)
    w_spec, emit_specs, emit_shapes = _weight_specs(w, (D, tn), lambda i, j: (0, j))
    assert not emit_specs or M == tm, "a weight tile must be visited once to be emitted"
    out, *emitted = pl.pallas_call(
        _norm_matmul_kernel,
        grid=(M // tm, N // tn),
        in_specs=[
            pl.BlockSpec((tm, D), lambda i, j: (i, 0)),
            pl.BlockSpec((None, 1, D), lambda i, j: (g_layer, 0, 0)),
            w_spec,
            pl.BlockSpec((None, 1, tn), lambda i, j: (b_layer, 0, j)),
        ],
        out_specs=[out_spec, *emit_specs],
        out_shape=[out_shape, *emit_shapes],
        scratch_shapes=[pltpu.VMEM((tm, D), BF16)],
        compiler_params=_params(2),
        name="norm_matmul",
    )(x, gains, w.arr, bias)
    return out, (emitted[0] if emitted else None)


def _matmul_res_kernel(a_ref, w_ref, b_ref, r_ref, o_ref, *emit):
    if len(a_ref.shape) == 2:
        a = a_ref[...]
    else:
        a = jnp.concatenate([a_ref[c] for c in range(a_ref.shape[0])], axis=1)
    o_ref[...] = r_ref[...] + b_ref[...] + jnp.dot(a.astype(BF16), _as_bf16_weight(w_ref, emit),
                                                   preferred_element_type=F32)


def matmul_res(a, w, bias, b_layer, res, tm):
    M, N = res.shape
    K = w.arr.shape[1]
    tn = _pick(N, _tile_cap(tm, w))
    if a.ndim == 2:
        a_spec = pl.BlockSpec((tm, K), lambda i, j: (i, 0))
    else:
        a_spec = pl.BlockSpec((K // LANES, tm, LANES), lambda i, j: (0, i, 0))
    w_spec, emit_specs, emit_shapes = _weight_specs(w, (K, tn), lambda i, j: (0, j))
    assert not emit_specs or M == tm, "a weight tile must be visited once to be emitted"
    out, *emitted = pl.pallas_call(
        _matmul_res_kernel,
        grid=(M // tm, N // tn),
        in_specs=[
            a_spec,
            w_spec,
            pl.BlockSpec((None, 1, tn), lambda i, j: (b_layer, 0, j)),
            pl.BlockSpec((tm, tn), lambda i, j: (i, j)),
        ],
        out_specs=[pl.BlockSpec((tm, tn), lambda i, j: (i, j)), *emit_specs],
        out_shape=[jax.ShapeDtypeStruct((M, N), F32), *emit_shapes],
        compiler_params=_params(2),
        name="matmul_res",
    )(a, w.arr, bias, res)
    return out, (emitted[0] if emitted else None)


def _lru_kernel(gate_ref, u_ref, cw_ref, cb_ref, wa_ref, ba_ref, wi_ref, bi_ref, lam_ref, ci_ref, hi_ref,
                o_ref, cs_ref, hs_ref,
                e_ref, a_scr, b_scr, h_scr, cc_ref, hc_ref, *, R, cru, tps):
    i = pl.program_id(0)
    c = pl.program_id(1)
    tm, tc = u_ref.shape

    @pl.when(i % tps == 0)
    def _():
        cc_ref[c] = ci_ref[...]
        hc_ref[c] = hi_ref[...]

    u = u_ref[...]
    e_ref[0:cru, :] = cc_ref[c]
    e_ref[cru:cru + tm, :] = u
    cw = cw_ref[...]
    uc = (cb_ref[...] + cw[3:4] * u + cw[2:3] * e_ref[cru - R:cru - R + tm, :]
          + cw[1:2] * e_ref[cru - 2 * R:cru - 2 * R + tm, :] + cw[0:1] * e_ref[cru - 3 * R:cru - 3 * R + tm, :])
    tail = e_ref[tm:tm + cru, :]
    cc_ref[c] = tail
    cs_ref[c] = tail

    hd = wa_ref.shape[-1]
    ucb = uc.astype(BF16)
    ra, ri = [], []
    for hh in range(tc // hd):
        ub = ucb[:, hh * hd:(hh + 1) * hd]
        ra.append(jnp.dot(ub, wa_ref[hh].astype(BF16), preferred_element_type=F32))
        ri.append(jnp.dot(ub, wi_ref[hh].astype(BF16), preferred_element_type=F32))
    r = jax.nn.sigmoid(jnp.concatenate(ra, axis=1) + ba_ref[...])
    ig = jax.nn.sigmoid(jnp.concatenate(ri, axis=1) + bi_ref[...])
    nlam = -lam_ref[...]
    softplus = jnp.maximum(nlam, 0.0) + jnp.log1p(jnp.exp(-jnp.abs(nlam)))
    log_a = (-LRU_C) * r * softplus
    a = jnp.exp(log_a)
    th = jnp.tanh(log_a)
    z = (-2.0 * th) / (1.0 - th)
    bx = (z * lax.rsqrt(jnp.maximum(z, F32_TINY))) * (ig * uc)

    if R == 1:
        A = a.reshape(tm // SUBLANES, SUBLANES, tc)
        B = bx.reshape(tm // SUBLANES, SUBLANES, tc)
        row = lax.broadcasted_iota(jnp.int32, A.shape, 1)
        s = 1
        while s < SUBLANES:
            m = row >= s
            B = jnp.where(m, A * pltpu.roll(B, s, axis=1) + B, B)
            A = jnp.where(m, A * pltpu.roll(A, s, axis=1), A)
            s *= 2
        a_scr[...] = A.reshape(tm, tc)
        b_scr[...] = B.reshape(tm, tc)
        h0 = hc_ref[c][SUBLANES - 1:SUBLANES, :]
    else:
        assert R == SUBLANES
        a_scr[...] = a
        b_scr[...] = bx
        h0 = hc_ref[c]

    def body(g, h):
        r0 = pl.multiple_of(g * SUBLANES, SUBLANES)
        hg = b_scr[pl.ds(r0, SUBLANES), :] + a_scr[pl.ds(r0, SUBLANES), :] * h
        h_scr[pl.ds(r0, SUBLANES), :] = hg
        return hg[SUBLANES - 1:SUBLANES, :] if R == 1 else hg

    n_groups = tm // SUBLANES
    lax.fori_loop(0, n_groups, body, h0, unroll=min(n_groups, 8))
    h_tail = h_scr[tm - SUBLANES:tm, :]
    hc_ref[c] = h_tail
    hs_ref[c] = h_tail
    o_ref[...] = (h_scr[...] * _gelu(gate_ref[...])).astype(BF16)


def lru_core(proj, conv_w, conv_b, w_a, b_a, w_i, b_i, lam, conv_init, h_init, layer, tm, R):
    M, W2 = proj.shape
    W = W2 // 2
    hd = w_a.shape[-1]
    tc = max(hd, _pick(W, LRU_CHANNEL_TILE))
    assert tc % hd == 0 and W % tc == 0
    nc = W // tc
    nseq, cru, _ = conv_init.shape
    tps = (M // tm) // nseq
    kern = functools.partial(_lru_kernel, R=R, cru=cru, tps=tps)
    row_vec = lambda: pl.BlockSpec((None, 1, tc), lambda i, c: (layer, 0, c))
    gate_w = lambda: pl.BlockSpec((None, tc // hd, hd, hd), lambda i, c: (layer, c, 0, 0))
    unfold = lambda s: jnp.swapaxes(s, 1, 2).reshape(nseq, s.shape[2], W)
    hg, c_rows, h_rows = pl.pallas_call(
        kern,
        grid=(M // tm, nc),
        in_specs=[
            pl.BlockSpec((tm, tc), lambda i, c: (i, c)),
            pl.BlockSpec((tm, tc), lambda i, c: (i, nc + c)),
            pl.BlockSpec((None, conv_w.shape[1], tc), lambda i, c: (layer, 0, c)),
            row_vec(), gate_w(), row_vec(), gate_w(), row_vec(), row_vec(),
            pl.BlockSpec((None, cru, tc), lambda i, c: (i // tps, 0, c)),
            pl.BlockSpec((None, SUBLANES, tc), lambda i, c: (i // tps, 0, c)),
        ],
        out_specs=[
            pl.BlockSpec((tm, tc), lambda i, c: (i, c)),
            pl.BlockSpec((None, nc, cru, tc), lambda i, c: (i // tps, 0, 0, 0)),
            pl.BlockSpec((None, nc, SUBLANES, tc), lambda i, c: (i // tps, 0, 0, 0)),
        ],
        out_shape=[
            jax.ShapeDtypeStruct((M, W), BF16),
            jax.ShapeDtypeStruct((nseq, nc, cru, tc), F32),
            jax.ShapeDtypeStruct((nseq, nc, SUBLANES, tc), F32),
        ],
        scratch_shapes=[
            pltpu.VMEM((cru + tm, tc), F32),
            pltpu.VMEM((tm, tc), F32),
            pltpu.VMEM((tm, tc), F32),
            pltpu.VMEM((tm, tc), F32),
            pltpu.VMEM((nc, cru, tc), F32),
            pltpu.VMEM((nc, SUBLANES, tc), F32),
        ],
        compiler_params=_params(2),
        name="lru_core",
    )(proj, proj, conv_w, conv_b, w_a, b_a, w_i, b_i, lam, conv_init, h_init)
    return hg, unfold(c_rows), unfold(h_rows)


def _ffn_kernel(x_ref, g_ref, wg_ref, wv_ref, cwg_ref, cwv_ref, cbg_ref, cbv_ref, wd_ref, ig_ref, iv_ref, *rest,
                R, cr, tps, final):
    fg_ref = rest[0] if final else None
    o_ref, sg_ref, sv_ref, *emit, xn_ref, eg_ref, ev_ref, cg_ref, cv_ref = rest[1:] if final else rest
    i = pl.program_id(0)
    j = pl.program_id(1)
    tm = x_ref.shape[0]

    @pl.when(j == 0)
    def _():
        x = x_ref[...]
        xn_ref[...] = _rms(x, g_ref[...]).astype(BF16)
        o_ref[...] = x

    @pl.when(i % tps == 0)
    def _():
        cg_ref[j] = ig_ref[...]
        cv_ref[j] = iv_ref[...]

    def side(w_ref, cw_ref, cb_ref, e_ref, c_ref, s_ref, emit_ref):
        up = jnp.dot(xn_ref[...], _as_bf16_weight(w_ref, emit_ref), preferred_element_type=F32)
        e_ref[0:cr, :] = c_ref[j]
        e_ref[cr:cr + tm, :] = up
        cw = cw_ref[...]
        conv = (cb_ref[...] + cw[2:3] * up + cw[1:2] * e_ref[cr - R:cr - R + tm, :]
                + cw[0:1] * e_ref[cr - 2 * R:cr - 2 * R + tm, :])
        tail = e_ref[tm:tm + cr, :]
        c_ref[j] = tail
        s_ref[j] = tail
        return conv

    cg = side(wg_ref, cwg_ref, cbg_ref, eg_ref, cg_ref, sg_ref, emit[0:1])
    cv = side(wv_ref, cwv_ref, cbv_ref, ev_ref, cv_ref, sv_ref, emit[1:2])
    act = (_gelu(cg) * cv).astype(BF16)
    o_ref[...] += jnp.dot(act, _as_bf16_weight(wd_ref, emit[2:3]), preferred_element_type=F32)
    if final:
        @pl.when(j == pl.num_programs(1) - 1)
        def _():
            o_ref[...] = _rms(o_ref[...], fg_ref[...])


def conv_ffn(x, gains, w_gate, w_val, conv_w, conv_b, w_down, init, layer, tm, tf, R, final_gain=None):
    M, D = x.shape
    F = w_down.arr.shape[1]
    emitting = w_down.arr.dtype != BF16
    if emitting:
        tf = min(tf, FFN_TILE_F32_WEIGHT)
    nf = F // tf
    nseq, cr, _ = init.shape
    tps = (M // tm) // nseq
    final = final_gain is not None
    kern = functools.partial(_ffn_kernel, R=R, cr=cr, tps=tps, final=final)
    K = conv_w.shape[1]
    halves = lambda mk: [mk(0), mk(nf)]
    state_spec = lambda off: pl.BlockSpec((None, cr, tf), lambda i, j: (i // tps, 0, off + j))
    w_specs, emit_specs, emit_shapes = [], [], []
    for w, col0 in (w_gate, w_val):
        assert col0 % tf == 0 and (w.arr.dtype != BF16) == emitting
        spec, e_spec, e_shape = _weight_specs(w, (D, tf), lambda i, j, off=col0 // tf: (0, off + j))
        w_specs.append(spec)
        emit_specs += [pl.BlockSpec((None, D, tf), lambda i, j: (0, 0, j))] if e_spec else []
        emit_shapes += [jax.ShapeDtypeStruct((1, D, F), BF16)] if e_shape else []
    wd_spec, e_spec, e_shape = _weight_specs(w_down, (tf, D), lambda i, j: (j, 0))
    emit_specs += e_spec
    emit_shapes += e_shape
    assert not emit_specs or M == tm, "a weight tile must be visited once to be emitted"
    once = pl.Buffered(1)
    out, sg, sv, *emitted = pl.pallas_call(
        kern,
        grid=(M // tm, nf),
        in_specs=[
            pl.BlockSpec((tm, D), lambda i, j: (i, 0)),
            pl.BlockSpec((None, 1, D), lambda i, j: (layer, 0, 0)),
            *w_specs,
            *halves(lambda off: pl.BlockSpec((None, K, tf), lambda i, j: (layer, 0, off + j))),
            *halves(lambda off: pl.BlockSpec((None, 1, tf), lambda i, j: (layer, 0, off + j))),
            wd_spec,
            *halves(state_spec),
            *([pl.BlockSpec((1, D), lambda i, j: (0, 0))] if final else []),
        ],
        out_specs=[
            pl.BlockSpec((tm, D), lambda i, j: (i, 0), pipeline_mode=once),
            pl.BlockSpec((None, nf, cr, tf), lambda i, j: (i // tps, 0, 0, 0)),
            pl.BlockSpec((None, nf, cr, tf), lambda i, j: (i // tps, 0, 0, 0)),
            *emit_specs,
        ],
        out_shape=[
            jax.ShapeDtypeStruct((M, D), F32),
            jax.ShapeDtypeStruct((nseq, nf, cr, tf), F32),
            jax.ShapeDtypeStruct((nseq, nf, cr, tf), F32),
            *emit_shapes,
        ],
        scratch_shapes=[
            pltpu.VMEM((tm, D), BF16),
            pltpu.VMEM((cr + tm, tf), F32),
            pltpu.VMEM((cr + tm, tf), F32),
            pltpu.VMEM((nf, cr, tf), F32),
            pltpu.VMEM((nf, cr, tf), F32),
        ],
        compiler_params=_params(2),
        name="conv_ffn",
    )(x, gains, w_gate[0].arr, w_val[0].arr, conv_w, conv_w, conv_b, conv_b, w_down.arr, init, init,
      *([final_gain] if final else []))
    unfold = lambda s: jnp.swapaxes(s, 1, 2).reshape(nseq, cr, F)
    return out, jnp.concatenate([unfold(sg), unfold(sv)], axis=-1), (tuple(emitted) if emitted else None)


ATTN_SUPER = max(w for w, _ in ATTN_GROUPS)
ATTN_BATCH = 8
ATTN_TRANSPOSE_DIL = SUBLANES


def _attn_prompt_kernel(sl_ref, q0, k0, v0, q1, k1, v1, q2, k2, v2, o_ref,
                        kc0, vc0, kc1, vc1, kc2, vc2, o_scr, l_scr):
    n = pl.program_id(2)
    SB = o_ref.shape[0]
    blk = ATTN_BLK
    row = lax.broadcasted_iota(jnp.int32, (blk, 2 * blk), 0)
    col = lax.broadcasted_iota(jnp.int32, (blk, 2 * blk), 1)
    steps = row + blk - col
    band = jnp.logical_and(steps >= 0, steps <= blk)
    band_cur = jnp.logical_and(band, col >= blk)
    steps_f = steps.astype(F32)
    ones_v = jnp.ones((2 * blk, HEAD_DIM), BF16)
    inv_sqrt = 1.0 / math.sqrt(HEAD_DIM)
    nt = (((1,), (1,)), ((), ()))
    groups = zip(ATTN_GROUPS, (q0, q1, q2), (k0, k1, k2), (v0, v1, v2), (kc0, kc1, kc2), (vc0, vc1, vc2))
    for g, ((win, dil), q_ref, k_ref, v_ref, kcar, vcar) in enumerate(groups):
        @pl.when(n == 0)
        def _():
            kcar[...] = jnp.zeros(kcar.shape, kcar.dtype)
            vcar[...] = jnp.zeros(vcar.shape, vcar.dtype)

        alibi = (sl_ref[g:g + 1, :] * float(-dil)) * steps_f
        bias_full = jnp.where(band, alibi, NEG)
        bias_head = jnp.where(n > 0, bias_full, jnp.where(band_cur, alibi, NEG))

        def rows(start):
            return pl.ds(start, blk, stride=dil) if dil > 1 else pl.ds(start, blk)

        transposed = dil >= ATTN_TRANSPOSE_DIL
        if transposed:
            assert win == SB
            split = lambda ref: jnp.swapaxes(ref[...].astype(BF16).reshape(blk, dil, HEAD_DIM), 0, 1)
            q_ph, k_ph, v_ph = split(q_ref), split(k_ref), split(v_ref)
            k_prev, v_prev = kcar[...], vcar[...]
        o_parts, l_parts = [], []
        n_blocks = SB // blk
        for b0 in range(0, n_blocks, ATTN_BATCH):
            infos = []
            for bidx in range(b0, b0 + ATTN_BATCH):
                j, r = divmod(bidx, dil)
                infos.append((j, r, j * win + r))
            scores = []
            for j, r, p0 in infos:
                if transposed:
                    q, kp, kc = q_ph[r], k_prev[r], k_ph[r]
                else:
                    q = q_ref[rows(p0), :].astype(BF16)
                    kp = (kcar[rows(r), :] if j == 0 else k_ref[rows(p0 - win), :]).astype(BF16)
                    kc = k_ref[rows(p0), :].astype(BF16)
                scores.append(lax.dot_general(q, jnp.concatenate([kp, kc], axis=0), nt, preferred_element_type=F32))
            probs = []
            for (j, r, p0), s in zip(infos, scores):
                s = s * inv_sqrt + (bias_head if j == 0 else bias_full)
                m = jnp.max(jnp.maximum(s[:, :blk], s[:, blk:]), axis=-1, keepdims=True)
                probs.append((jnp.exp(s - m).astype(BF16), m))
            for (j, r, p0), (e, m) in zip(infos, probs):
                if transposed:
                    vp, vc = v_prev[r], v_ph[r]
                else:
                    vp = (vcar[rows(r), :] if j == 0 else v_ref[rows(p0 - win), :]).astype(BF16)
                    vc = v_ref[rows(p0), :].astype(BF16)
                vcat = jnp.concatenate([vp, vc], axis=0)
                res = jnp.dot(e, jnp.concatenate([vcat, ones_v], axis=1), preferred_element_type=F32)
                den = res[:, HEAD_DIM:]
                o_blk, l_blk = res[:, :HEAD_DIM] / den, m + jnp.log(den)
                if transposed:
                    o_parts.append(o_blk)
                    l_parts.append(l_blk)
                else:
                    o_scr[g, rows(p0), :] = o_blk
                    l_scr[g, rows(p0), :] = l_blk
        if transposed:
            merge = lambda parts: jnp.swapaxes(jnp.stack(parts, axis=0), 0, 1).reshape(SB, HEAD_DIM)
            o_scr[g] = merge(o_parts)
            l_scr[g] = merge(l_parts)
            kcar[...] = k_ph
            vcar[...] = v_ph
        else:
            kcar[...] = k_ref[SB - win:SB, :]
            vcar[...] = v_ref[SB - win:SB, :]

    chunk = 2 * blk
    for c0 in range(0, SB, chunk):
        ls = [l_scr[g, c0:c0 + chunk, :] for g in range(N_GROUPS)]
        mm = jnp.maximum(jnp.maximum(ls[0], ls[1]), ls[2])
        es = [jnp.exp(l - mm) for l in ls]
        acc = es[0] * o_scr[0, c0:c0 + chunk, :]
        for g in range(1, N_GROUPS):
            acc = acc + es[g] * o_scr[g, c0:c0 + chunk, :]
        o_ref[c0:c0 + chunk, :] = (acc / (es[0] + es[1] + es[2])).astype(BF16)


def attn_prompt(qkv_slabs, B, S):
    n_slabs, M, _ = qkv_slabs.shape
    SB = ATTN_SUPER
    assert n_slabs == 3 * N_GROUPS * GROUP_HEADS and S % SB == 0 and M == B * S
    assert all(w // d == ATTN_BLK for w, d in ATTN_GROUPS) and (SB // ATTN_BLK) % ATTN_BATCH == 0
    nsb = S // SB
    slopes = jnp.asarray(_alibi_slopes(), F32).T
    slopes = jnp.broadcast_to(slopes[:, :, None], (GROUP_HEADS, N_GROUPS, 2 * ATTN_BLK))

    def slab(g, comp):
        base = (g * 3 + comp) * GROUP_HEADS
        return pl.BlockSpec((None, SB, HEAD_DIM), lambda b, h, n: (base + h, b * nsb + n, 0))

    carry = [pltpu.VMEM((d, w // d, HEAD_DIM), BF16) if d >= ATTN_TRANSPOSE_DIL else pltpu.VMEM((w, HEAD_DIM), F32)
             for w, d in ATTN_GROUPS for _kv in range(2)]
    return pl.pallas_call(
        _attn_prompt_kernel,
        grid=(B, GROUP_HEADS, nsb),
        in_specs=[pl.BlockSpec((None, N_GROUPS, 2 * ATTN_BLK), lambda b, h, n: (h, 0, 0))]
        + [slab(g, comp) for g in range(N_GROUPS) for comp in range(3)],
        out_specs=pl.BlockSpec((None, SB, HEAD_DIM), lambda b, h, n: (h, b * nsb + n, 0)),
        out_shape=jax.ShapeDtypeStruct((GROUP_HEADS, M, HEAD_DIM), BF16),
        scratch_shapes=carry + [pltpu.VMEM((N_GROUPS, SB, HEAD_DIM), F32), pltpu.VMEM((N_GROUPS, SB, HEAD_DIM), F32)],
        compiler_params=_params(3),
        name="attn_prompt",
    )(slopes, *([qkv_slabs] * (3 * N_GROUPS)))


def _kv_tail_kernel(*refs):
    o_ref = refs[-1]
    for i, ref in enumerate(refs[:-1]):
        o_ref[i // 2, :, i % 2] = jnp.swapaxes(ref[...], 0, 1)


def kv_tails(qkv_layers, g, B, S, keep):
    chunk = min(keep, KV_TAIL_ROWS)
    assert keep % chunk == 0 and S % chunk == 0
    first = (S - keep) // chunk
    per_seq = S // chunk

    def slabs(comp):
        return pl.BlockSpec((GROUP_HEADS, chunk, HEAD_DIM), lambda b, t: (g * 3 + comp, b * per_seq + first + t, 0))

    n = len(qkv_layers)
    return pl.pallas_call(
        _kv_tail_kernel,
        grid=(B, keep // chunk),
        in_specs=[slabs(comp) for _ in range(n) for comp in (1, 2)],
        out_specs=pl.BlockSpec((n, None, chunk, 2, GROUP_HEADS, HEAD_DIM), lambda b, t: (0, b, t, 0, 0, 0)),
        out_shape=jax.ShapeDtypeStruct((n, B, keep, 2, GROUP_HEADS, HEAD_DIM), F32),
        compiler_params=_params(2),
        name="kv_tails",
    )(*[q for q in qkv_layers for _ in range(2)])


def _attn_sample_kernel(sl_ref, qkv_ref, c0_ref, c1_ref, c2_ref, o_ref):
    T = qkv_ref.shape[0]
    H = GROUP_HEADS
    caches = (c0_ref, c1_ref, c2_ref)
    rows = c0_ref.shape[0]
    l_idx = lax.broadcasted_iota(jnp.int32, (rows, H, 1), 0)
    l_f = l_idx.astype(F32)
    inv_sqrt = 1.0 / math.sqrt(HEAD_DIM)
    for t in range(T):
        outs, lses = [], []
        for g, (win, dil) in enumerate(ATTN_GROUPS):
            base = g * 3 * H
            slope = sl_ref[g][:, 0:1]
            q = qkv_ref[t, base:base + H, :]
            ph = 0 if dil == 1 else t
            kc = caches[g][:, ph * 2 * H:ph * 2 * H + H, :]
            vc = caches[g][:, ph * 2 * H + H:(ph + 1) * 2 * H, :]
            sc = jnp.sum(kc * q[None], axis=-1, keepdims=True) * inv_sqrt
            if dil == 1:
                sc = jnp.where(l_idx >= t, sc - slope[None] * (float(rows + t) - l_f), NEG)
                new_ts = list(range(t + 1))
            else:
                sc = sc - (slope[None] * float(dil)) * (float(rows) - l_f)
                new_ts = [t]
            m = jnp.max(sc, axis=0)
            s_new = []
            for t2 in new_ts:
                k2 = qkv_ref[t2, base + H:base + 2 * H, :]
                s2 = jnp.sum(q * k2, axis=-1, keepdims=True) * inv_sqrt - slope * float((t - t2) * dil)
                s_new.append(s2)
                m = jnp.maximum(m, s2)
            ec = jnp.exp(sc - m[None])
            den = jnp.sum(ec, axis=0)
            acc = jnp.sum(ec * vc, axis=0)
            for t2, s2 in zip(new_ts, s_new):
                v2 = qkv_ref[t2, base + 2 * H:base + 3 * H, :]
                e2 = jnp.exp(s2 - m)
                den = den + e2
                acc = acc + e2 * v2
            outs.append(acc / den)
            lses.append(m + jnp.log(den))
        mm = jnp.maximum(jnp.maximum(lses[0], lses[1]), lses[2])
        es = [jnp.exp(l - mm) for l in lses]
        o_ref[t] = (es[0] * outs[0] + es[1] * outs[1] + es[2] * outs[2]) / (es[0] + es[1] + es[2])


def attn_sample(qkv, caches, layer, T, Bs):
    H = GROUP_HEADS
    n_rows = qkv.shape[-1] // HEAD_DIM
    views, specs = [], []
    for (win, dil), c in zip(ATTN_GROUPS, caches):
        assert c.shape[1] == Bs and c.shape[2] == win and win // dil == ATTN_BLK and (dil == 1 or T <= dil)
        phases = min(dil, T)
        views.append(c.reshape(c.shape[0], Bs, win // dil, dil * 2 * H, HEAD_DIM))
        specs.append(pl.BlockSpec((None, None, win // dil, phases * 2 * H, HEAD_DIM), lambda b: (layer, b, 0, 0, 0)))
    slopes = jnp.broadcast_to(jnp.asarray(_alibi_slopes(), F32)[:, :, None], (N_GROUPS, H, HEAD_DIM))
    q_rows = jnp.swapaxes(qkv.reshape(T, Bs, n_rows, HEAD_DIM), 0, 1)
    out = pl.pallas_call(
        _attn_sample_kernel,
        grid=(Bs,),
        in_specs=[pl.BlockSpec((N_GROUPS, H, HEAD_DIM), lambda b: (0, 0, 0)),
                  pl.BlockSpec((None, T, n_rows, HEAD_DIM), lambda b: (b, 0, 0, 0)), *specs],
        out_specs=pl.BlockSpec((None, T, H, HEAD_DIM), lambda b: (b, 0, 0, 0)),
        out_shape=jax.ShapeDtypeStruct((Bs, T, H, HEAD_DIM), F32),
        compiler_params=_params(1),
        name="attn_sample",
    )(slopes, q_rows, *views)
    return jnp.swapaxes(out, 0, 1).reshape(T * Bs, ATTN_WIDTH)


class Tiles(NamedTuple):
    seq: int
    mm: int
    ffn: int
    tf: int


def _run_trunk(x, nseq, R, tiles, lru_h, lru_conv, kv_caches, ffn_conv, P, W, dims):
    depth = P["norm_mix"].shape[0]
    new_h, new_lconv, new_fconv, qkvs = [], [], [], []
    bf16_w = {name: [] for name in W}
    for layer in range(depth):
        j = layer // 2
        if layer % 2 == 0:
            proj, wb = norm_matmul(x, P["norm_mix"], layer, W["lru_w_in"][j], P["lru_b_in"], j, tm=tiles.mm)
            bf16_w["lru_w_in"].append(wb)
            hg, c_rows, h_rows = lru_core(proj, P["lru_conv_w"], P["lru_conv_b"], P["lru_w_a"], P["lru_b_a"],
                                          P["lru_w_i"], P["lru_b_i"], P["lru_lambda"], lru_conv[j], lru_h[j],
                                          layer=j, tm=tiles.seq, R=R)
            x, wb = matmul_res(hg, W["lru_w_out"][j], P["lru_b_out"], j, x, tm=tiles.mm)
            bf16_w["lru_w_out"].append(wb)
            new_h.append(h_rows)
            new_lconv.append(c_rows)
        else:
            prompt = kv_caches is None
            qkv, wb = norm_matmul(x, P["norm_mix"], layer, W["attn_w_qkv"][j], P["attn_b_zero"], j, tm=tiles.mm,
                                  slab_out=prompt)
            bf16_w["attn_w_qkv"].append(wb)
            qkvs.append(qkv)
            if prompt:
                o = attn_prompt(qkv, *dims)
            else:
                o = attn_sample(qkv, kv_caches, layer=j, T=dims[0], Bs=dims[1])
            x, wb = matmul_res(o, W["attn_w_o"][j], P["attn_bo_zero"], j, x, tm=tiles.mm)
            bf16_w["attn_w_o"].append(wb)
        x, f_rows, wbs = conv_ffn(x, P["norm_ffn"], W["ffn_w_gate"][layer], W["ffn_w_val"][layer], P["ffn_conv_w"],
                                  P["ffn_conv_b"], W["ffn_w_down"][layer], ffn_conv[layer], layer=layer,
                                  tm=tiles.ffn, tf=tiles.tf, R=R,
                                  final_gain=P["norm_final"] if layer == depth - 1 else None)
        for name, wb in zip(("ffn_w_gate", "ffn_w_val", "ffn_w_down"), wbs or (None,) * 3):
            bf16_w[name].append(wb)
        new_fconv.append(f_rows)
    return x, new_h, new_lconv, qkvs, new_fconv, bf16_w


def kernel(x_prompt, x_sample, cache_kv_w128, cache_kv_w512, cache_kv_w2048, state_lru_h, state_lru_conv, state_ffn_conv, norm_mix, norm_ffn, norm_final, lru_w_in, lru_b_in, lru_conv_w, lru_conv_b, lru_w_a, lru_b_a, lru_w_i, lru_b_i, lru_lambda, lru_w_out, lru_b_out, attn_w_qkv, attn_w_o, ffn_w_up, ffn_conv_w, ffn_conv_b, ffn_w_down):
    B, S, D = x_prompt.shape
    Bs, T, _ = x_sample.shape
    depth = norm_mix.shape[0]
    n_lru, W = lru_lambda.shape
    n_attn = attn_w_qkv.shape[0]
    F2 = ffn_w_up.shape[-1]
    assert Bs == SUBLANES, "the sample group is laid out time-major with one sublane per sequence"
    row3 = lambda a: a.reshape(a.shape[0], 1, a.shape[-1])
    P = {
        "norm_mix": row3(norm_mix), "norm_ffn": row3(norm_ffn), "norm_final": norm_final.reshape(1, D),
        "lru_b_in": row3(lru_b_in), "lru_conv_w": lru_conv_w, "lru_conv_b": row3(lru_conv_b),
        "lru_w_a": lru_w_a, "lru_b_a": row3(lru_b_a), "lru_w_i": lru_w_i, "lru_b_i": row3(lru_b_i),
        "lru_lambda": row3(lru_lambda), "lru_b_out": row3(lru_b_out),
        "attn_b_zero": jnp.zeros((n_attn, 1, attn_w_qkv.shape[-1]), F32),
        "attn_bo_zero": jnp.zeros((n_attn, 1, D), F32),
        "ffn_conv_w": ffn_conv_w, "ffn_conv_b": row3(ffn_conv_b),
    }
    stack = lambda w: [Weight(w, i) for i in range(w.shape[0])]
    W_f32 = {
        "lru_w_in": stack(lru_w_in), "lru_w_out": stack(lru_w_out),
        "attn_w_qkv": stack(attn_w_qkv), "attn_w_o": stack(attn_w_o),
        "ffn_w_gate": [(w, 0) for w in stack(ffn_w_up)], "ffn_w_val": [(w, F2 // 2) for w in stack(ffn_w_up)],
        "ffn_w_down": stack(ffn_w_down),
    }
    kc = lru_conv_w.shape[1] - 1
    kf = ffn_conv_w.shape[1] - 1

    tmaj = lambda a: jnp.swapaxes(a, 0, 1).reshape(1, a.shape[1] * a.shape[0], a.shape[-1])
    y_s, h_s, lc_s, qkv_s, fc_s, bf16_w = _run_trunk(
        jnp.swapaxes(x_sample, 0, 1).reshape(T * Bs, D), 1, Bs,
        Tiles(seq=T * Bs, mm=T * Bs, ffn=T * Bs, tf=_pick(F2 // 2, FFN_TILE_FEW_ROWS)),
        [state_lru_h[j].reshape(1, Bs, W) for j in range(n_lru)],
        [tmaj(state_lru_conv[j]) for j in range(n_lru)],
        (cache_kv_w128, cache_kv_w512, cache_kv_w2048),
        [tmaj(state_ffn_conv[l]) for l in range(depth)], P, W_f32, (T, Bs))

    W_bf16 = {name: [(Weight(w, 0), 0) if name in ("ffn_w_gate", "ffn_w_val") else Weight(w, 0) for w in ws]
              for name, ws in bf16_w.items()}

    row_tile = lambda cap: cap if S % cap == 0 else S
    tiles_p = Tiles(seq=row_tile(ROW_TILE_LRU), mm=row_tile(ROW_TILE_MATMUL), ffn=row_tile(ROW_TILE_MATMUL),
                    tf=_pick(F2 // 2, FFN_TILE))
    zeros = lambda n, w: [jnp.zeros((B, SUBLANES, w), F32)] * n
    y_p, h_p, lc_p, qkv_p, fc_p, _ = _run_trunk(
        x_prompt.reshape(B * S, D), B, 1, tiles_p, zeros(n_lru, W), zeros(n_lru, W), None, zeros(depth, F2), P, W_bf16, (B, S))

    bmaj = lambda a, k: jnp.swapaxes(a.reshape(-1, Bs, a.shape[-1])[-k:], 0, 1)
    kv_p, kv_s = [], []
    for g, (win, dil) in enumerate(ATTN_GROUPS):
        lo = (g * 3 + 1) * ATTN_WIDTH
        keep = min(win, S)
        kv_p.append(kv_tails(qkv_p, g, B, S, keep))
        kv_s.append(jnp.stack([jnp.swapaxes(q.reshape(T, Bs, -1), 0, 1)[:, :, lo:lo + 2 * ATTN_WIDTH]
                               .reshape(Bs, T, 2, GROUP_HEADS, HEAD_DIM) for q in qkv_s], axis=0))
    return (
        y_p.reshape(B, S, D),
        jnp.swapaxes(y_s.reshape(T, Bs, D), 0, 1),
        kv_p[0], kv_p[1], kv_p[2],
        jnp.stack([h[:, SUBLANES - 1] for h in h_p], axis=0),
        jnp.stack([c[:, SUBLANES - kc:] for c in lc_p], axis=0),
        jnp.stack([f[:, SUBLANES - kf:] for f in fc_p], axis=0),
        kv_s[0], kv_s[1], kv_s[2],
        jnp.stack([h[0] for h in h_s], axis=0),
        jnp.stack([bmaj(c[0], kc) for c in lc_s], axis=0),
        jnp.stack([bmaj(f[0], kf) for f in fc_s], axis=0),
    )
```

```python
import functools
import math
from typing import NamedTuple

import jax
import jax.numpy as jnp
from jax import lax
from jax.experimental import pallas as pl
from jax.experimental.pallas import tpu as pltpu

EPS = 1e-6
NEG = -1e30
LRU_C = 8.0
ATTN_GROUPS = ((128, 1), (512, 4), (2048, 16))
N_GROUPS = len(ATTN_GROUPS)
GROUP_HEADS = 8
HEAD_DIM = 128
ATTN_WIDTH = GROUP_HEADS * HEAD_DIM
ATTN_BLK = 128
SUBLANES = 8
LANES = 128
VMEM_LIMIT = 56 * 1024 * 1024

ROW_TILE_MATMUL = 1024
ROW_TILE_LRU = 1024
COL_TILE_MATMUL = 2048
COL_TILE_RESIDUAL = 1024
COL_TILE_FEW_ROWS = 2048
COL_TILE_F32_WEIGHT = 1024
FEW_ROWS = 256
LRU_CHANNEL_TILE = 1024
FFN_TILE = 512
FFN_TILE_FEW_ROWS = 1024
FFN_TILE_F32_WEIGHT = 512
KV_TAIL_ROWS = 256

F32 = jnp.float32
F32_TINY = 1.1754944e-38
BF16 = jnp.bfloat16


def _alibi_slopes():
    n = N_GROUPS * GROUP_HEADS
    return [[2.0 ** (-8.0 * (g * GROUP_HEADS + h + 1) / n) for h in range(GROUP_HEADS)] for g in range(N_GROUPS)]


def _gelu(x):
    c = math.sqrt(2.0 / math.pi)
    return x * (0.5 * (1.0 + jnp.tanh(c * (x + 0.044715 * (x * x * x)))))


def _rms(x, g):
    ms = jnp.mean(x * x, axis=-1, keepdims=True)
    return x * lax.rsqrt(ms + EPS) * g


def _pick(n, cap):
    best = None
    for t in range(LANES, min(n, cap) + 1, LANES):
        if n % t == 0:
            best = t
    assert best is not None, (n, cap)
    return best


def _params(n_axes):
    return pltpu.CompilerParams(dimension_semantics=("arbitrary",) * n_axes, vmem_limit_bytes=VMEM_LIMIT)


def _as_bf16_weight(w_ref, emit_refs):
    w = w_ref[...]
    if w.dtype != BF16:
        w = w.astype(BF16)
        emit_refs[0][...] = w
    return w


def _norm_matmul_kernel(x_ref, g_ref, w_ref, b_ref, o_ref, *rest):
    xn_ref = rest[-1]

    @pl.when(pl.program_id(1) == 0)
    def _():
        xn_ref[...] = _rms(x_ref[...], g_ref[...]).astype(BF16)

    acc = jnp.dot(xn_ref[...], _as_bf16_weight(w_ref, rest), preferred_element_type=F32) + b_ref[...]
    if len(o_ref.shape) == 2:
        o_ref[...] = acc
    else:
        for c in range(o_ref.shape[0]):
            o_ref[c] = acc[:, c * LANES:(c + 1) * LANES]


class Weight(NamedTuple):
    arr: jax.Array
    layer: int


def _weight_specs(w, block, index):
    spec = pl.BlockSpec((None, *block), lambda *ij: (w.layer, *index(*ij)))
    if w.arr.dtype == BF16:
        return spec, [], []
    return (spec, [pl.BlockSpec((None, *block), lambda *ij: (0, *index(*ij)))],
            [jax.ShapeDtypeStruct((1, *w.arr.shape[1:]), BF16)])


def _tile_cap(tm, w, many_rows_cap):
    if w.arr.dtype != BF16:
        return COL_TILE_F32_WEIGHT
    return many_rows_cap if tm > FEW_ROWS else COL_TILE_FEW_ROWS


def norm_matmul(x, gains, g_layer, w, bias, b_layer, tm, slab_out=False):
    M, D = x.shape
    N = w.arr.shape[-1]
    tn = _pick(N, _tile_cap(tm, w, COL_TILE_MATMUL))
    if slab_out:
        out_spec = pl.BlockSpec((tn // LANES, tm, LANES), lambda i, j: (j, i, 0))
        out_shape = jax.ShapeDtypeStruct((N // LANES, M, LANES), F32)
    else:
        out_spec = pl.BlockSpec((tm, tn), lambda i, j: (i, j))
        out_shape = jax.ShapeDtypeStruct((M, N), F32)
    w_spec, emit_specs, emit_shapes = _weight_specs(w, (D, tn), lambda i, j: (0, j))
    assert not emit_specs or M == tm, "a weight tile must be visited once to be emitted"
    out, *emitted = pl.pallas_call(
        _norm_matmul_kernel,
        grid=(M // tm, N // tn),
        in_specs=[
            pl.BlockSpec((tm, D), lambda i, j: (i, 0)),
            pl.BlockSpec((None, 1, D), lambda i, j: (g_layer, 0, 0)),
            w_spec,
            pl.BlockSpec((None, 1, tn), lambda i, j: (b_layer, 0, j)),
        ],
        out_specs=[out_spec, *emit_specs],
        out_shape=[out_shape, *emit_shapes],
        scratch_shapes=[pltpu.VMEM((tm, D), BF16)],
        compiler_params=_params(2),
        name="norm_matmul",
    )(x, gains, w.arr, bias)
    return out, (emitted[0] if emitted else None)


def _matmul_res_kernel(a_ref, w_ref, b_ref, r_ref, o_ref, *emit):
    if len(a_ref.shape) == 2:
        a = a_ref[...]
    else:
        a = jnp.concatenate([a_ref[c] for c in range(a_ref.shape[0])], axis=1)
    o_ref[...] = r_ref[...] + b_ref[...] + jnp.dot(a.astype(BF16), _as_bf16_weight(w_ref, emit),
                                                   preferred_element_type=F32)


def matmul_res(a, w, bias, b_layer, res, tm):
    M, N = res.shape
    K = w.arr.shape[1]
    tn = _pick(N, _tile_cap(tm, w, COL_TILE_RESIDUAL))
    if a.ndim == 2:
        a_spec = pl.BlockSpec((tm, K), lambda i, j: (i, 0))
    else:
        a_spec = pl.BlockSpec((K // LANES, tm, LANES), lambda i, j: (0, i, 0))
    w_spec, emit_specs, emit_shapes = _weight_specs(w, (K, tn), lambda i, j: (0, j))
    assert not emit_specs or M == tm, "a weight tile must be visited once to be emitted"
    out, *emitted = pl.pallas_call(
        _matmul_res_kernel,
        grid=(M // tm, N // tn),
        in_specs=[
            a_spec,
            w_spec,
            pl.BlockSpec((None, 1, tn), lambda i, j: (b_layer, 0, j)),
            pl.BlockSpec((tm, tn), lambda i, j: (i, j)),
        ],
        out_specs=[pl.BlockSpec((tm, tn), lambda i, j: (i, j)), *emit_specs],
        out_shape=[jax.ShapeDtypeStruct((M, N), F32), *emit_shapes],
        compiler_params=_params(2),
        name="matmul_res",
    )(a, w.arr, bias, res)
    return out, (emitted[0] if emitted else None)


def _lru_kernel(gate_ref, u_ref, cw_ref, cb_ref, wa_ref, ba_ref, wi_ref, bi_ref, lam_ref, ci_ref, hi_ref,
                o_ref, cs_ref, hs_ref,
                e_ref, a_scr, b_scr, h_scr, cc_ref, hc_ref, *, R, cru, tps):
    i = pl.program_id(0)
    c = pl.program_id(1)
    tm, tc = u_ref.shape

    @pl.when(i % tps == 0)
    def _():
        cc_ref[c] = ci_ref[...]
        hc_ref[c] = hi_ref[...]

    u = u_ref[...]
    e_ref[0:cru, :] = cc_ref[c]
    e_ref[cru:cru + tm, :] = u
    cw = cw_ref[...]
    uc = (cb_ref[...] + cw[3:4] * u + cw[2:3] * e_ref[cru - R:cru - R + tm, :]
          + cw[1:2] * e_ref[cru - 2 * R:cru - 2 * R + tm, :] + cw[0:1] * e_ref[cru - 3 * R:cru - 3 * R + tm, :])
    tail = e_ref[tm:tm + cru, :]
    cc_ref[c] = tail
    cs_ref[c] = tail

    hd = wa_ref.shape[-1]
    ucb = uc.astype(BF16)
    ra, ri = [], []
    for hh in range(tc // hd):
        ub = ucb[:, hh * hd:(hh + 1) * hd]
        ra.append(jnp.dot(ub, wa_ref[hh].astype(BF16), preferred_element_type=F32))
        ri.append(jnp.dot(ub, wi_ref[hh].astype(BF16), preferred_element_type=F32))
    r = jax.nn.sigmoid(jnp.concatenate(ra, axis=1) + ba_ref[...])
    ig = jax.nn.sigmoid(jnp.concatenate(ri, axis=1) + bi_ref[...])
    nlam = -lam_ref[...]
    softplus = jnp.maximum(nlam, 0.0) + jnp.log1p(jnp.exp(-jnp.abs(nlam)))
    log_a = (-LRU_C) * r * softplus
    a = jnp.exp(log_a)
    th = jnp.tanh(log_a)
    z = (-2.0 * th) / (1.0 - th)
    bx = (z * lax.rsqrt(jnp.maximum(z, F32_TINY))) * (ig * uc)

    if R == 1:
        A = a.reshape(tm // SUBLANES, SUBLANES, tc)
        B = bx.reshape(tm // SUBLANES, SUBLANES, tc)
        row = lax.broadcasted_iota(jnp.int32, A.shape, 1)
        s = 1
        while s < SUBLANES:
            m = row >= s
            B = jnp.where(m, A * pltpu.roll(B, s, axis=1) + B, B)
            A = jnp.where(m, A * pltpu.roll(A, s, axis=1), A)
            s *= 2
        a_scr[...] = A.reshape(tm, tc)
        b_scr[...] = B.reshape(tm, tc)
        h0 = hc_ref[c][SUBLANES - 1:SUBLANES, :]
    else:
        assert R == SUBLANES
        a_scr[...] = a
        b_scr[...] = bx
        h0 = hc_ref[c]

    def body(g, h):
        r0 = pl.multiple_of(g * SUBLANES, SUBLANES)
        hg = b_scr[pl.ds(r0, SUBLANES), :] + a_scr[pl.ds(r0, SUBLANES), :] * h
        h_scr[pl.ds(r0, SUBLANES), :] = hg
        return hg[SUBLANES - 1:SUBLANES, :] if R == 1 else hg

    n_groups = tm // SUBLANES
    lax.fori_loop(0, n_groups, body, h0, unroll=min(n_groups, 8))
    h_tail = h_scr[tm - SUBLANES:tm, :]
    hc_ref[c] = h_tail
    hs_ref[c] = h_tail
    o_ref[...] = (h_scr[...] * _gelu(gate_ref[...])).astype(BF16)


def lru_core(proj, conv_w, conv_b, w_a, b_a, w_i, b_i, lam, conv_init, h_init, layer, tm, R):
    M, W2 = proj.shape
    W = W2 // 2
    hd = w_a.shape[-1]
    tc = max(hd, _pick(W, LRU_CHANNEL_TILE))
    assert tc % hd == 0 and W % tc == 0
    nc = W // tc
    nseq, cru, _ = conv_init.shape
    tps = (M // tm) // nseq
    kern = functools.partial(_lru_kernel, R=R, cru=cru, tps=tps)
    row_vec = lambda: pl.BlockSpec((None, 1, tc), lambda i, c: (layer, 0, c))
    gate_w = lambda: pl.BlockSpec((None, tc // hd, hd, hd), lambda i, c: (layer, c, 0, 0))
    unfold = lambda s: jnp.swapaxes(s, 1, 2).reshape(nseq, s.shape[2], W)
    hg, c_rows, h_rows = pl.pallas_call(
        kern,
        grid=(M // tm, nc),
        in_specs=[
            pl.BlockSpec((tm, tc), lambda i, c: (i, c)),
            pl.BlockSpec((tm, tc), lambda i, c: (i, nc + c)),
            pl.BlockSpec((None, conv_w.shape[1], tc), lambda i, c: (layer, 0, c)),
            row_vec(), gate_w(), row_vec(), gate_w(), row_vec(), row_vec(),
            pl.BlockSpec((None, cru, tc), lambda i, c: (i // tps, 0, c)),
            pl.BlockSpec((None, SUBLANES, tc), lambda i, c: (i // tps, 0, c)),
        ],
        out_specs=[
            pl.BlockSpec((tm, tc), lambda i, c: (i, c)),
            pl.BlockSpec((None, nc, cru, tc), lambda i, c: (i // tps, 0, 0, 0)),
            pl.BlockSpec((None, nc, SUBLANES, tc), lambda i, c: (i // tps, 0, 0, 0)),
        ],
        out_shape=[
            jax.ShapeDtypeStruct((M, W), BF16),
            jax.ShapeDtypeStruct((nseq, nc, cru, tc), F32),
            jax.ShapeDtypeStruct((nseq, nc, SUBLANES, tc), F32),
        ],
        scratch_shapes=[
            pltpu.VMEM((cru + tm, tc), F32),
            pltpu.VMEM((tm, tc), F32),
            pltpu.VMEM((tm, tc), F32),
            pltpu.VMEM((tm, tc), F32),
            pltpu.VMEM((nc, cru, tc), F32),
            pltpu.VMEM((nc, SUBLANES, tc), F32),
        ],
        compiler_params=_params(2),
        name="lru_core",
    )(proj, proj, conv_w, conv_b, w_a, b_a, w_i, b_i, lam, conv_init, h_init)
    return hg, unfold(c_rows), unfold(h_rows)


def _ffn_kernel(x_ref, g_ref, wg_ref, wv_ref, cwg_ref, cwv_ref, cbg_ref, cbv_ref, wd_ref, ig_ref, iv_ref, *rest,
                R, cr, tps, final):
    fg_ref = rest[0] if final else None
    o_ref, sg_ref, sv_ref, *emit, xn_ref, eg_ref, ev_ref, cg_ref, cv_ref = rest[1:] if final else rest
    i = pl.program_id(0)
    j = pl.program_id(1)
    tm = x_ref.shape[0]

    @pl.when(j == 0)
    def _():
        x = x_ref[...]
        xn_ref[...] = _rms(x, g_ref[...]).astype(BF16)
        o_ref[...] = x

    @pl.when(i % tps == 0)
    def _():
        cg_ref[j] = ig_ref[...]
        cv_ref[j] = iv_ref[...]

    def side(w_ref, cw_ref, cb_ref, e_ref, c_ref, s_ref, emit_ref):
        up = jnp.dot(xn_ref[...], _as_bf16_weight(w_ref, emit_ref), preferred_element_type=F32)
        e_ref[0:cr, :] = c_ref[j]
        e_ref[cr:cr + tm, :] = up
        cw = cw_ref[...]
        conv = (cb_ref[...] + cw[2:3] * up + cw[1:2] * e_ref[cr - R:cr - R + tm, :]
                + cw[0:1] * e_ref[cr - 2 * R:cr - 2 * R + tm, :])
        tail = e_ref[tm:tm + cr, :]
        c_ref[j] = tail
        s_ref[j] = tail
        return conv

    cg = side(wg_ref, cwg_ref, cbg_ref, eg_ref, cg_ref, sg_ref, emit[0:1])
    cv = side(wv_ref, cwv_ref, cbv_ref, ev_ref, cv_ref, sv_ref, emit[1:2])
    act = (_gelu(cg) * cv).astype(BF16)
    o_ref[...] += jnp.dot(act, _as_bf16_weight(wd_ref, emit[2:3]), preferred_element_type=F32)
    if final:
        @pl.when(j == pl.num_programs(1) - 1)
        def _():
            o_ref[...] = _rms(o_ref[...], fg_ref[...])


def conv_ffn(x, gains, w_gate, w_val, conv_w, conv_b, w_down, init, layer, tm, tf, R, final_gain=None):
    M, D = x.shape
    F = w_down.arr.shape[1]
    emitting = w_down.arr.dtype != BF16
    if emitting:
        tf = min(tf, FFN_TILE_F32_WEIGHT)
    nf = F // tf
    nseq, cr, _ = init.shape
    tps = (M // tm) // nseq
    final = final_gain is not None
    kern = functools.partial(_ffn_kernel, R=R, cr=cr, tps=tps, final=final)
    K = conv_w.shape[1]
    halves = lambda mk: [mk(0), mk(nf)]
    state_spec = lambda off: pl.BlockSpec((None, cr, tf), lambda i, j: (i // tps, 0, off + j))
    w_specs, emit_specs, emit_shapes = [], [], []
    for w, col0 in (w_gate, w_val):
        assert col0 % tf == 0 and (w.arr.dtype != BF16) == emitting
        spec, e_spec, e_shape = _weight_specs(w, (D, tf), lambda i, j, off=col0 // tf: (0, off + j))
        w_specs.append(spec)
        emit_specs += [pl.BlockSpec((None, D, tf), lambda i, j: (0, 0, j))] if e_spec else []
        emit_shapes += [jax.ShapeDtypeStruct((1, D, F), BF16)] if e_shape else []
    wd_spec, e_spec, e_shape = _weight_specs(w_down, (tf, D), lambda i, j: (j, 0))
    emit_specs += e_spec
    emit_shapes += e_shape
    assert not emit_specs or M == tm, "a weight tile must be visited once to be emitted"
    once = pl.Buffered(1)
    out, sg, sv, *emitted = pl.pallas_call(
        kern,
        grid=(M // tm, nf),
        in_specs=[
            pl.BlockSpec((tm, D), lambda i, j: (i, 0)),
            pl.BlockSpec((None, 1, D), lambda i, j: (layer, 0, 0)),
            *w_specs,
            *halves(lambda off: pl.BlockSpec((None, K, tf), lambda i, j: (layer, 0, off + j))),
            *halves(lambda off: pl.BlockSpec((None, 1, tf), lambda i, j: (layer, 0, off + j))),
            wd_spec,
            *halves(state_spec),
            *([pl.BlockSpec((1, D), lambda i, j: (0, 0))] if final else []),
        ],
        out_specs=[
            pl.BlockSpec((tm, D), lambda i, j: (i, 0), pipeline_mode=once),
            pl.BlockSpec((None, nf, cr, tf), lambda i, j: (i // tps, 0, 0, 0)),
            pl.BlockSpec((None, nf, cr, tf), lambda i, j: (i // tps, 0, 0, 0)),
            *emit_specs,
        ],
        out_shape=[
            jax.ShapeDtypeStruct((M, D), F32),
            jax.ShapeDtypeStruct((nseq, nf, cr, tf), F32),
            jax.ShapeDtypeStruct((nseq, nf, cr, tf), F32),
            *emit_shapes,
        ],
        scratch_shapes=[
            pltpu.VMEM((tm, D), BF16),
            pltpu.VMEM((cr + tm, tf), F32),
            pltpu.VMEM((cr + tm, tf), F32),
            pltpu.VMEM((nf, cr, tf), F32),
            pltpu.VMEM((nf, cr, tf), F32),
        ],
        compiler_params=_params(2),
        name="conv_ffn",
    )(x, gains, w_gate[0].arr, w_val[0].arr, conv_w, conv_w, conv_b, conv_b, w_down.arr, init, init,
      *([final_gain] if final else []))
    unfold = lambda s: jnp.swapaxes(s, 1, 2).reshape(nseq, cr, F)
    return out, jnp.concatenate([unfold(sg), unfold(sv)], axis=-1), (tuple(emitted) if emitted else None)


ATTN_SUPER = max(w for w, _ in ATTN_GROUPS)
ATTN_BATCH = 8
ATTN_TRANSPOSE_DIL = SUBLANES


def _attn_prompt_kernel(sl_ref, q0, k0, v0, q1, k1, v1, q2, k2, v2, o_ref,
                        kc0, vc0, kc1, vc1, kc2, vc2, o_scr, l_scr):
    n = pl.program_id(2)
    SB = o_ref.shape[0]
    blk = ATTN_BLK
    row = lax.broadcasted_iota(jnp.int32, (blk, 2 * blk), 0)
    col = lax.broadcasted_iota(jnp.int32, (blk, 2 * blk), 1)
    steps = row + blk - col
    band = jnp.logical_and(steps >= 0, steps <= blk)
    band_cur = jnp.logical_and(band, col >= blk)
    steps_f = steps.astype(F32)
    ones_v = jnp.ones((2 * blk, HEAD_DIM), BF16)
    inv_sqrt = 1.0 / math.sqrt(HEAD_DIM)
    nt = (((1,), (1,)), ((), ()))
    groups = zip(ATTN_GROUPS, (q0, q1, q2), (k0, k1, k2), (v0, v1, v2), (kc0, kc1, kc2), (vc0, vc1, vc2))
    for g, ((win, dil), q_ref, k_ref, v_ref, kcar, vcar) in enumerate(groups):
        @pl.when(n == 0)
        def _():
            kcar[...] = jnp.zeros(kcar.shape, kcar.dtype)
            vcar[...] = jnp.zeros(vcar.shape, vcar.dtype)

        alibi = (sl_ref[g:g + 1, :] * float(-dil)) * steps_f
        bias_full = jnp.where(band, alibi, NEG)
        bias_head = jnp.where(n > 0, bias_full, jnp.where(band_cur, alibi, NEG))

        def rows(start):
            return pl.ds(start, blk, stride=dil) if dil > 1 else pl.ds(start, blk)

        transposed = dil >= ATTN_TRANSPOSE_DIL
        if transposed:
            assert win == SB
            split = lambda ref: jnp.swapaxes(ref[...].astype(BF16).reshape(blk, dil, HEAD_DIM), 0, 1)
            q_ph, k_ph, v_ph = split(q_ref), split(k_ref), split(v_ref)
            k_prev, v_prev = kcar[...], vcar[...]
        o_parts, l_parts = [], []
        n_blocks = SB // blk
        for b0 in range(0, n_blocks, ATTN_BATCH):
            infos = []
            for bidx in range(b0, b0 + ATTN_BATCH):
                j, r = divmod(bidx, dil)
                infos.append((j, r, j * win + r))
            scores = []
            for j, r, p0 in infos:
                if transposed:
                    q, kp, kc = q_ph[r], k_prev[r], k_ph[r]
                else:
                    q = q_ref[rows(p0), :].astype(BF16)
                    kp = (kcar[rows(r), :] if j == 0 else k_ref[rows(p0 - win), :]).astype(BF16)
                    kc = k_ref[rows(p0), :].astype(BF16)
                scores.append(lax.dot_general(q, jnp.concatenate([kp, kc], axis=0), nt, preferred_element_type=F32))
            probs = []
            for (j, r, p0), s in zip(infos, scores):
                s = s * inv_sqrt + (bias_head if j == 0 else bias_full)
                m = jnp.max(jnp.maximum(s[:, :blk], s[:, blk:]), axis=-1, keepdims=True)
                probs.append((jnp.exp(s - m).astype(BF16), m))
            for (j, r, p0), (e, m) in zip(infos, probs):
                if transposed:
                    vp, vc = v_prev[r], v_ph[r]
                else:
                    vp = (vcar[rows(r), :] if j == 0 else v_ref[rows(p0 - win), :]).astype(BF16)
                    vc = v_ref[rows(p0), :].astype(BF16)
                vcat = jnp.concatenate([vp, vc], axis=0)
                res = jnp.dot(e, jnp.concatenate([vcat, ones_v], axis=1), preferred_element_type=F32)
                den = res[:, HEAD_DIM:]
                o_blk, l_blk = res[:, :HEAD_DIM] / den, m + jnp.log(den)
                if transposed:
                    o_parts.append(o_blk)
                    l_parts.append(l_blk)
                else:
                    o_scr[g, rows(p0), :] = o_blk
                    l_scr[g, rows(p0), :] = l_blk
        if transposed:
            merge = lambda parts: jnp.swapaxes(jnp.stack(parts, axis=0), 0, 1).reshape(SB, HEAD_DIM)
            o_scr[g] = merge(o_parts)
            l_scr[g] = merge(l_parts)
            kcar[...] = k_ph
            vcar[...] = v_ph
        else:
            kcar[...] = k_ref[SB - win:SB, :]
            vcar[...] = v_ref[SB - win:SB, :]

    chunk = 2 * blk
    for c0 in range(0, SB, chunk):
        ls = [l_scr[g, c0:c0 + chunk, :] for g in range(N_GROUPS)]
        mm = jnp.maximum(jnp.maximum(ls[0], ls[1]), ls[2])
        es = [jnp.exp(l - mm) for l in ls]
        acc = es[0] * o_scr[0, c0:c0 + chunk, :]
        for g in range(1, N_GROUPS):
            acc = acc + es[g] * o_scr[g, c0:c0 + chunk, :]
        o_ref[c0:c0 + chunk, :] = (acc / (es[0] + es[1] + es[2])).astype(BF16)


def attn_prompt(qkv_slabs, B, S):
    n_slabs, M, _ = qkv_slabs.shape
    SB = ATTN_SUPER
    assert n_slabs == 3 * N_GROUPS * GROUP_HEADS and S % SB == 0 and M == B * S
    assert all(w // d == ATTN_BLK for w, d in ATTN_GROUPS) and (SB // ATTN_BLK) % ATTN_BATCH == 0
    nsb = S // SB
    slopes = jnp.asarray(_alibi_slopes(), F32).T
    slopes = jnp.broadcast_to(slopes[:, :, None], (GROUP_HEADS, N_GROUPS, 2 * ATTN_BLK))

    def slab(g, comp):
        base = (g * 3 + comp) * GROUP_HEADS
        return pl.BlockSpec((None, SB, HEAD_DIM), lambda b, h, n: (base + h, b * nsb + n, 0))

    carry = [pltpu.VMEM((d, w // d, HEAD_DIM), BF16) if d >= ATTN_TRANSPOSE_DIL else pltpu.VMEM((w, HEAD_DIM), F32)
             for w, d in ATTN_GROUPS for _kv in range(2)]
    return pl.pallas_call(
        _attn_prompt_kernel,
        grid=(B, GROUP_HEADS, nsb),
        in_specs=[pl.BlockSpec((None, N_GROUPS, 2 * ATTN_BLK), lambda b, h, n: (h, 0, 0))]
        + [slab(g, comp) for g in range(N_GROUPS) for comp in range(3)],
        out_specs=pl.BlockSpec((None, SB, HEAD_DIM), lambda b, h, n: (h, b * nsb + n, 0)),
        out_shape=jax.ShapeDtypeStruct((GROUP_HEADS, M, HEAD_DIM), BF16),
        scratch_shapes=carry + [pltpu.VMEM((N_GROUPS, SB, HEAD_DIM), F32), pltpu.VMEM((N_GROUPS, SB, HEAD_DIM), F32)],
        compiler_params=_params(3),
        name="attn_prompt",
    )(slopes, *([qkv_slabs] * (3 * N_GROUPS)))


def _kv_tail_kernel(*refs):
    o_ref = refs[-1]
    for i, ref in enumerate(refs[:-1]):
        o_ref[i // 2, :, i % 2] = jnp.swapaxes(ref[...], 0, 1)


def kv_tails(qkv_layers, g, B, S, keep):
    chunk = min(keep, KV_TAIL_ROWS)
    assert keep % chunk == 0 and S % chunk == 0
    first = (S - keep) // chunk
    per_seq = S // chunk

    def slabs(comp):
        return pl.BlockSpec((GROUP_HEADS, chunk, HEAD_DIM), lambda b, t: (g * 3 + comp, b * per_seq + first + t, 0))

    n = len(qkv_layers)
    return pl.pallas_call(
        _kv_tail_kernel,
        grid=(B, keep // chunk),
        in_specs=[slabs(comp) for _ in range(n) for comp in (1, 2)],
        out_specs=pl.BlockSpec((n, None, chunk, 2, GROUP_HEADS, HEAD_DIM), lambda b, t: (0, b, t, 0, 0, 0)),
        out_shape=jax.ShapeDtypeStruct((n, B, keep, 2, GROUP_HEADS, HEAD_DIM), F32),
        compiler_params=_params(2),
        name="kv_tails",
    )(*[q for q in qkv_layers for _ in range(2)])


def _attn_sample_kernel(sl_ref, qkv_ref, c0_ref, c1_ref, c2_ref, o_ref):
    T = qkv_ref.shape[0]
    H = GROUP_HEADS
    caches = (c0_ref, c1_ref, c2_ref)
    rows = c0_ref.shape[0]
    l_idx = lax.broadcasted_iota(jnp.int32, (rows, H, 1), 0)
    l_f = l_idx.astype(F32)
    inv_sqrt = 1.0 / math.sqrt(HEAD_DIM)
    for t in range(T):
        outs, lses = [], []
        for g, (win, dil) in enumerate(ATTN_GROUPS):
            base = g * 3 * H
            slope = sl_ref[g][:, 0:1]
            q = qkv_ref[t, base:base + H, :]
            ph = 0 if dil == 1 else t
            kc = caches[g][:, ph * 2 * H:ph * 2 * H + H, :]
            vc = caches[g][:, ph * 2 * H + H:(ph + 1) * 2 * H, :]
            sc = jnp.sum(kc * q[None], axis=-1, keepdims=True) * inv_sqrt
            if dil == 1:
                sc = jnp.where(l_idx >= t, sc - slope[None] * (float(rows + t) - l_f), NEG)
                new_ts = list(range(t + 1))
            else:
                sc = sc - (slope[None] * float(dil)) * (float(rows) - l_f)
                new_ts = [t]
            m = jnp.max(sc, axis=0)
            s_new = []
            for t2 in new_ts:
                k2 = qkv_ref[t2, base + H:base + 2 * H, :]
                s2 = jnp.sum(q * k2, axis=-1, keepdims=True) * inv_sqrt - slope * float((t - t2) * dil)
                s_new.append(s2)
                m = jnp.maximum(m, s2)
            ec = jnp.exp(sc - m[None])
            den = jnp.sum(ec, axis=0)
            acc = jnp.sum(ec * vc, axis=0)
            for t2, s2 in zip(new_ts, s_new):
                v2 = qkv_ref[t2, base + 2 * H:base + 3 * H, :]
                e2 = jnp.exp(s2 - m)
                den = den + e2
                acc = acc + e2 * v2
            outs.append(acc / den)
            lses.append(m + jnp.log(den))
        mm = jnp.maximum(jnp.maximum(lses[0], lses[1]), lses[2])
        es = [jnp.exp(l - mm) for l in lses]
        o_ref[t] = (es[0] * outs[0] + es[1] * outs[1] + es[2] * outs[2]) / (es[0] + es[1] + es[2])


def attn_sample(qkv, caches, layer, T, Bs):
    H = GROUP_HEADS
    n_rows = qkv.shape[-1] // HEAD_DIM
    views, specs = [], []
    for (win, dil), c in zip(ATTN_GROUPS, caches):
        assert c.shape[1] == Bs and c.shape[2] == win and win // dil == ATTN_BLK and (dil == 1 or T <= dil)
        phases = min(dil, T)
        views.append(c.reshape(c.shape[0], Bs, win // dil, dil * 2 * H, HEAD_DIM))
        specs.append(pl.BlockSpec((None, None, win // dil, phases * 2 * H, HEAD_DIM), lambda b: (layer, b, 0, 0, 0)))
    slopes = jnp.broadcast_to(jnp.asarray(_alibi_slopes(), F32)[:, :, None], (N_GROUPS, H, HEAD_DIM))
    q_rows = jnp.swapaxes(qkv.reshape(T, Bs, n_rows, HEAD_DIM), 0, 1)
    out = pl.pallas_call(
        _attn_sample_kernel,
        grid=(Bs,),
        in_specs=[pl.BlockSpec((N_GROUPS, H, HEAD_DIM), lambda b: (0, 0, 0)),
                  pl.BlockSpec((None, T, n_rows, HEAD_DIM), lambda b: (b, 0, 0, 0)), *specs],
        out_specs=pl.BlockSpec((None, T, H, HEAD_DIM), lambda b: (b, 0, 0, 0)),
        out_shape=jax.ShapeDtypeStruct((Bs, T, H, HEAD_DIM), F32),
        compiler_params=_params(1),
        name="attn_sample",
    )(slopes, q_rows, *views)
    return jnp.swapaxes(out, 0, 1).reshape(T * Bs, ATTN_WIDTH)


class Tiles(NamedTuple):
    seq: int
    mm: int
    ffn: int
    tf: int


def _run_trunk(x, nseq, R, tiles, lru_h, lru_conv, kv_caches, ffn_conv, P, W, dims):
    depth = P["norm_mix"].shape[0]
    new_h, new_lconv, new_fconv, qkvs = [], [], [], []
    bf16_w = {name: [] for name in W}
    for layer in range(depth):
        j = layer // 2
        if layer % 2 == 0:
            proj, wb = norm_matmul(x, P["norm_mix"], layer, W["lru_w_in"][j], P["lru_b_in"], j, tm=tiles.mm)
            bf16_w["lru_w_in"].append(wb)
            hg, c_rows, h_rows = lru_core(proj, P["lru_conv_w"], P["lru_conv_b"], P["lru_w_a"], P["lru_b_a"],
                                          P["lru_w_i"], P["lru_b_i"], P["lru_lambda"], lru_conv[j], lru_h[j],
                                          layer=j, tm=tiles.seq, R=R)
            x, wb = matmul_res(hg, W["lru_w_out"][j], P["lru_b_out"], j, x, tm=tiles.mm)
            bf16_w["lru_w_out"].append(wb)
            new_h.append(h_rows)
            new_lconv.append(c_rows)
        else:
            prompt = kv_caches is None
            qkv, wb = norm_matmul(x, P["norm_mix"], layer, W["attn_w_qkv"][j], P["attn_b_zero"], j, tm=tiles.mm,
                                  slab_out=prompt)
            bf16_w["attn_w_qkv"].append(wb)
            qkvs.append(qkv)
            if prompt:
                o = attn_prompt(qkv, *dims)
            else:
                o = attn_sample(qkv, kv_caches, layer=j, T=dims[0], Bs=dims[1])
            x, wb = matmul_res(o, W["attn_w_o"][j], P["attn_bo_zero"], j, x, tm=tiles.mm)
            bf16_w["attn_w_o"].append(wb)
        x, f_rows, wbs = conv_ffn(x, P["norm_ffn"], W["ffn_w_gate"][layer], W["ffn_w_val"][layer], P["ffn_conv_w"],
                                  P["ffn_conv_b"], W["ffn_w_down"][layer], ffn_conv[layer], layer=layer,
                                  tm=tiles.ffn, tf=tiles.tf, R=R,
                                  final_gain=P["norm_final"] if layer == depth - 1 else None)
        for name, wb in zip(("ffn_w_gate", "ffn_w_val", "ffn_w_down"), wbs or (None,) * 3):
            bf16_w[name].append(wb)
        new_fconv.append(f_rows)
    return x, new_h, new_lconv, qkvs, new_fconv, bf16_w


def kernel(x_prompt, x_sample, cache_kv_w128, cache_kv_w512, cache_kv_w2048, state_lru_h, state_lru_conv, state_ffn_conv, norm_mix, norm_ffn, norm_final, lru_w_in, lru_b_in, lru_conv_w, lru_conv_b, lru_w_a, lru_b_a, lru_w_i, lru_b_i, lru_lambda, lru_w_out, lru_b_out, attn_w_qkv, attn_w_o, ffn_w_up, ffn_conv_w, ffn_conv_b, ffn_w_down):
    B, S, D = x_prompt.shape
    Bs, T, _ = x_sample.shape
    depth = norm_mix.shape[0]
    n_lru, W = lru_lambda.shape
    n_attn = attn_w_qkv.shape[0]
    F2 = ffn_w_up.shape[-1]
    assert Bs == SUBLANES, "the sample group is laid out time-major with one sublane per sequence"
    row3 = lambda a: a.reshape(a.shape[0], 1, a.shape[-1])
    P = {
        "norm_mix": row3(norm_mix), "norm_ffn": row3(norm_ffn), "norm_final": norm_final.reshape(1, D),
        "lru_b_in": row3(lru_b_in), "lru_conv_w": lru_conv_w, "lru_conv_b": row3(lru_conv_b),
        "lru_w_a": lru_w_a, "lru_b_a": row3(lru_b_a), "lru_w_i": lru_w_i, "lru_b_i": row3(lru_b_i),
        "lru_lambda": row3(lru_lambda), "lru_b_out": row3(lru_b_out),
        "attn_b_zero": jnp.zeros((n_attn, 1, attn_w_qkv.shape[-1]), F32),
        "attn_bo_zero": jnp.zeros((n_attn, 1, D), F32),
        "ffn_conv_w": ffn_conv_w, "ffn_conv_b": row3(ffn_conv_b),
    }
    stack = lambda w: [Weight(w, i) for i in range(w.shape[0])]
    W_f32 = {
        "lru_w_in": stack(lru_w_in), "lru_w_out": stack(lru_w_out),
        "attn_w_qkv": stack(attn_w_qkv), "attn_w_o": stack(attn_w_o),
        "ffn_w_gate": [(w, 0) for w in stack(ffn_w_up)], "ffn_w_val": [(w, F2 // 2) for w in stack(ffn_w_up)],
        "ffn_w_down": stack(ffn_w_down),
    }
    kc = lru_conv_w.shape[1] - 1
    kf = ffn_conv_w.shape[1] - 1

    tmaj = lambda a: jnp.swapaxes(a, 0, 1).reshape(1, a.shape[1] * a.shape[0], a.shape[-1])
    y_s, h_s, lc_s, qkv_s, fc_s, bf16_w = _run_trunk(
        jnp.swapaxes(x_sample, 0, 1).reshape(T * Bs, D), 1, Bs,
        Tiles(seq=T * Bs, mm=T * Bs, ffn=T * Bs, tf=_pick(F2 // 2, FFN_TILE_FEW_ROWS)),
        [state_lru_h[j].reshape(1, Bs, W) for j in range(n_lru)],
        [tmaj(state_lru_conv[j]) for j in range(n_lru)],
        (cache_kv_w128, cache_kv_w512, cache_kv_w2048),
        [tmaj(state_ffn_conv[l]) for l in range(depth)], P, W_f32, (T, Bs))

    W_bf16 = {name: [(Weight(w, 0), 0) if name in ("ffn_w_gate", "ffn_w_val") else Weight(w, 0) for w in ws]
              for name, ws in bf16_w.items()}

    row_tile = lambda cap: cap if S % cap == 0 else S
    tiles_p = Tiles(seq=row_tile(ROW_TILE_LRU), mm=row_tile(ROW_TILE_MATMUL), ffn=row_tile(ROW_TILE_MATMUL),
                    tf=_pick(F2 // 2, FFN_TILE))
    zeros = lambda n, w: [jnp.zeros((B, SUBLANES, w), F32)] * n
    y_p, h_p, lc_p, qkv_p, fc_p, _ = _run_trunk(
        x_prompt.reshape(B * S, D), B, 1, tiles_p, zeros(n_lru, W), zeros(n_lru, W), None, zeros(depth, F2), P, W_bf16, (B, S))

    bmaj = lambda a, k: jnp.swapaxes(a.reshape(-1, Bs, a.shape[-1])[-k:], 0, 1)
    kv_p, kv_s = [], []
    for g, (win, dil) in enumerate(ATTN_GROUPS):
        lo = (g * 3 + 1) * ATTN_WIDTH
        keep = min(win, S)
        kv_p.append(kv_tails(qkv_p, g, B, S, keep))
        kv_s.append(jnp.stack([jnp.swapaxes(q.reshape(T, Bs, -1), 0, 1)[:, :, lo:lo + 2 * ATTN_WIDTH]
                               .reshape(Bs, T, 2, GROUP_HEADS, HEAD_DIM) for q in qkv_s], axis=0))
    return (
        y_p.reshape(B, S, D),
        jnp.swapaxes(y_s.reshape(T, Bs, D), 0, 1),
        kv_p[0], kv_p[1], kv_p[2],
        jnp.stack([h[:, SUBLANES - 1] for h in h_p], axis=0),
        jnp.stack([c[:, SUBLANES - kc:] for c in lc_p], axis=0),
        jnp.stack([f[:, SUBLANES - kf:] for f in fc_p], axis=0),
        kv_s[0], kv_s[1], kv_s[2],
        jnp.stack([h[0] for h in h_s], axis=0),
        jnp.stack([bmaj(c[0], kc) for c in lc_s], axis=0),
        jnp.stack([bmaj(f[0], kf) for f in fc_s], axis=0),
    )
```

```python
import functools
import math
from typing import NamedTuple

import jax
import jax.numpy as jnp
from jax import lax
from jax.experimental import pallas as pl
from jax.experimental.pallas import tpu as pltpu

EPS = 1e-6
NEG = -1e30
LRU_C = 8.0
ATTN_GROUPS = ((128, 1), (512, 4), (2048, 16))
N_GROUPS = len(ATTN_GROUPS)
GROUP_HEADS = 8
HEAD_DIM = 128
ATTN_WIDTH = GROUP_HEADS * HEAD_DIM
ATTN_BLK = 128
SUBLANES = 8
LANES = 128
VMEM_LIMIT = 56 * 1024 * 1024

ROW_TILE_MATMUL = 1024
ROW_TILE_LRU = 1024
COL_TILE_MATMUL = 2048
COL_TILE_RESIDUAL = 1024
COL_TILE_FEW_ROWS = 2048
COL_TILE_F32_WEIGHT = 1024
FEW_ROWS = 256
LRU_CHANNEL_TILE = 1024
FFN_TILE = 768
FFN_TILE_FEW_ROWS = 1024
FFN_TILE_F32_WEIGHT = 512
KV_TAIL_ROWS = 256

F32 = jnp.float32
F32_TINY = 1.1754944e-38
BF16 = jnp.bfloat16


def _alibi_slopes():
    n = N_GROUPS * GROUP_HEADS
    return [[2.0 ** (-8.0 * (g * GROUP_HEADS + h + 1) / n) for h in range(GROUP_HEADS)] for g in range(N_GROUPS)]


def _gelu(x):
    c = math.sqrt(2.0 / math.pi)
    return x * (0.5 * (1.0 + jnp.tanh(c * (x + 0.044715 * (x * x * x)))))


def _rms(x, g):
    ms = jnp.mean(x * x, axis=-1, keepdims=True)
    return x * lax.rsqrt(ms + EPS) * g


def _shift_rows(x, carry, s):
    tm, width = x.shape
    cr = carry.shape[0]
    if s % SUBLANES == 0:
        return jnp.concatenate([carry[cr - s:cr], x[:tm - s]], axis=0)
    assert s < SUBLANES
    groups = pltpu.roll(x.reshape(tm // SUBLANES, SUBLANES, width), s, axis=1)
    before = jnp.concatenate([pltpu.roll(carry[cr - SUBLANES:cr], s, axis=0)[None], groups[:-1]], axis=0)
    row = lax.broadcasted_iota(jnp.int32, groups.shape, 1)
    return jnp.where(row < s, before, groups).reshape(tm, width)


def _pick(n, cap):
    best = None
    for t in range(LANES, min(n, cap) + 1, LANES):
        if n % t == 0:
            best = t
    assert best is not None, (n, cap)
    return best


def _params(n_axes):
    return pltpu.CompilerParams(dimension_semantics=("arbitrary",) * n_axes, vmem_limit_bytes=VMEM_LIMIT)


def _as_bf16_weight(w_ref, emit_refs):
    w = w_ref[...]
    if w.dtype != BF16:
        w = w.astype(BF16)
        emit_refs[0][...] = w
    return w


def _norm_matmul_kernel(x_ref, g_ref, w_ref, b_ref, o_ref, *rest):
    xn_ref = rest[-1]

    @pl.when(pl.program_id(1) == 0)
    def _():
        xn_ref[...] = _rms(x_ref[...], g_ref[...]).astype(BF16)

    acc = jnp.dot(xn_ref[...], _as_bf16_weight(w_ref, rest), preferred_element_type=F32) + b_ref[...]
    if len(o_ref.shape) == 2:
        o_ref[...] = acc
    else:
        for c in range(o_ref.shape[0]):
            o_ref[c] = acc[:, c * LANES:(c + 1) * LANES]


class Weight(NamedTuple):
    arr: jax.Array
    layer: int


def _weight_specs(w, block, index):
    spec = pl.BlockSpec((None, *block), lambda *ij: (w.layer, *index(*ij)))
    if w.arr.dtype == BF16:
        return spec, [], []
    return (spec, [pl.BlockSpec((None, *block), lambda *ij: (0, *index(*ij)))],
            [jax.ShapeDtypeStruct((1, *w.arr.shape[1:]), BF16)])


def _tile_cap(tm, w, many_rows_cap):
    if w.arr.dtype != BF16:
        return COL_TILE_F32_WEIGHT
    return many_rows_cap if tm > FEW_ROWS else COL_TILE_FEW_ROWS


def norm_matmul(x, gains, g_layer, w, bias, b_layer, tm, slab_out=False):
    M, D = x.shape
    N = w.arr.shape[-1]
    tn = _pick(N, _tile_cap(tm, w, COL_TILE_MATMUL))
    if slab_out:
        out_spec = pl.BlockSpec((tn // LANES, tm, LANES), lambda i, j: (j, i, 0))
        out_shape = jax.ShapeDtypeStruct((N // LANES, M, LANES), F32)
    else:
        out_spec = pl.BlockSpec((tm, tn), lambda i, j: (i, j))
        out_shape = jax.ShapeDtypeStruct((M, N), F32)
    w_spec, emit_specs, emit_shapes = _weight_specs(w, (D, tn), lambda i, j: (0, j))
    assert not emit_specs or M == tm, "a weight tile must be visited once to be emitted"
    out, *emitted = pl.pallas_call(
        _norm_matmul_kernel,
        grid=(M // tm, N // tn),
        in_specs=[
            pl.BlockSpec((tm, D), lambda i, j: (i, 0)),
            pl.BlockSpec((None, 1, D), lambda i, j: (g_layer, 0, 0)),
            w_spec,
            pl.BlockSpec((None, 1, tn), lambda i, j: (b_layer, 0, j)),
        ],
        out_specs=[out_spec, *emit_specs],
        out_shape=[out_shape, *emit_shapes],
        scratch_shapes=[pltpu.VMEM((tm, D), BF16)],
        compiler_params=_params(2),
        name="norm_matmul",
    )(x, gains, w.arr, bias)
    return out, (emitted[0] if emitted else None)


def _matmul_res_kernel(a_ref, w_ref, b_ref, r_ref, o_ref, *emit):
    if len(a_ref.shape) == 2:
        a = a_ref[...]
    else:
        a = jnp.concatenate([a_ref[c] for c in range(a_ref.shape[0])], axis=1)
    o_ref[...] = r_ref[...] + b_ref[...] + jnp.dot(a.astype(BF16), _as_bf16_weight(w_ref, emit),
                                                   preferred_element_type=F32)


def matmul_res(a, w, bias, b_layer, res, tm):
    M, N = res.shape
    K = w.arr.shape[1]
    tn = _pick(N, _tile_cap(tm, w, COL_TILE_RESIDUAL))
    if a.ndim == 2:
        a_spec = pl.BlockSpec((tm, K), lambda i, j: (i, 0))
    else:
        a_spec = pl.BlockSpec((K // LANES, tm, LANES), lambda i, j: (0, i, 0))
    w_spec, emit_specs, emit_shapes = _weight_specs(w, (K, tn), lambda i, j: (0, j))
    assert not emit_specs or M == tm, "a weight tile must be visited once to be emitted"
    out, *emitted = pl.pallas_call(
        _matmul_res_kernel,
        grid=(M // tm, N // tn),
        in_specs=[
            a_spec,
            w_spec,
            pl.BlockSpec((None, 1, tn), lambda i, j: (b_layer, 0, j)),
            pl.BlockSpec((tm, tn), lambda i, j: (i, j)),
        ],
        out_specs=[pl.BlockSpec((tm, tn), lambda i, j: (i, j)), *emit_specs],
        out_shape=[jax.ShapeDtypeStruct((M, N), F32), *emit_shapes],
        compiler_params=_params(2),
        name="matmul_res",
    )(a, w.arr, bias, res)
    return out, (emitted[0] if emitted else None)


def _lru_kernel(gate_ref, u_ref, cw_ref, cb_ref, wa_ref, ba_ref, wi_ref, bi_ref, lam_ref, ci_ref, hi_ref,
                o_ref, cs_ref, hs_ref,
                a_scr, b_scr, h_scr, cc_ref, hc_ref, *, R, cru, tps):
    i = pl.program_id(0)
    c = pl.program_id(1)
    tm, tc = u_ref.shape

    @pl.when(i % tps == 0)
    def _():
        cc_ref[c] = ci_ref[...]
        hc_ref[c] = hi_ref[...]

    u = u_ref[...]
    carry = cc_ref[c]
    cw = cw_ref[...]
    uc = (cb_ref[...] + cw[3:4] * u + cw[2:3] * _shift_rows(u, carry, R)
          + cw[1:2] * _shift_rows(u, carry, 2 * R) + cw[0:1] * _shift_rows(u, carry, 3 * R))
    tail = u[tm - cru:tm]
    cc_ref[c] = tail
    cs_ref[c] = tail

    hd = wa_ref.shape[-1]
    ucb = uc.astype(BF16)
    ra, ri = [], []
    for hh in range(tc // hd):
        ub = ucb[:, hh * hd:(hh + 1) * hd]
        ra.append(jnp.dot(ub, wa_ref[hh].astype(BF16), preferred_element_type=F32))
        ri.append(jnp.dot(ub, wi_ref[hh].astype(BF16), preferred_element_type=F32))
    r = jax.nn.sigmoid(jnp.concatenate(ra, axis=1) + ba_ref[...])
    ig = jax.nn.sigmoid(jnp.concatenate(ri, axis=1) + bi_ref[...])
    nlam = -lam_ref[...]
    softplus = jnp.maximum(nlam, 0.0) + jnp.log1p(jnp.exp(-jnp.abs(nlam)))
    log_a = (-LRU_C) * r * softplus
    a = jnp.exp(log_a)
    th = jnp.tanh(log_a)
    z = (-2.0 * th) / (1.0 - th)
    bx = (z * lax.rsqrt(jnp.maximum(z, F32_TINY))) * (ig * uc)

    if R == 1:
        A = a.reshape(tm // SUBLANES, SUBLANES, tc)
        B = bx.reshape(tm // SUBLANES, SUBLANES, tc)
        row = lax.broadcasted_iota(jnp.int32, A.shape, 1)
        s = 1
        while s < SUBLANES:
            m = row >= s
            B = jnp.where(m, A * pltpu.roll(B, s, axis=1) + B, B)
            A = jnp.where(m, A * pltpu.roll(A, s, axis=1), A)
            s *= 2
        a_scr[...] = A.reshape(tm, tc)
        b_scr[...] = B.reshape(tm, tc)
        h0 = hc_ref[c][SUBLANES - 1:SUBLANES, :]
    else:
        assert R == SUBLANES
        a_scr[...] = a
        b_scr[...] = bx
        h0 = hc_ref[c]

    def body(g, h):
        r0 = pl.multiple_of(g * SUBLANES, SUBLANES)
        hg = b_scr[pl.ds(r0, SUBLANES), :] + a_scr[pl.ds(r0, SUBLANES), :] * h
        h_scr[pl.ds(r0, SUBLANES), :] = hg
        return hg[SUBLANES - 1:SUBLANES, :] if R == 1 else hg

    n_groups = tm // SUBLANES
    lax.fori_loop(0, n_groups, body, h0, unroll=min(n_groups, 8))
    h_tail = h_scr[tm - SUBLANES:tm, :]
    hc_ref[c] = h_tail
    hs_ref[c] = h_tail
    o_ref[...] = (h_scr[...] * _gelu(gate_ref[...])).astype(BF16)


def lru_core(proj, conv_w, conv_b, w_a, b_a, w_i, b_i, lam, conv_init, h_init, layer, tm, R):
    M, W2 = proj.shape
    W = W2 // 2
    hd = w_a.shape[-1]
    tc = max(hd, _pick(W, LRU_CHANNEL_TILE))
    assert tc % hd == 0 and W % tc == 0
    nc = W // tc
    nseq, cru, _ = conv_init.shape
    tps = (M // tm) // nseq
    kern = functools.partial(_lru_kernel, R=R, cru=cru, tps=tps)
    row_vec = lambda: pl.BlockSpec((None, 1, tc), lambda i, c: (layer, 0, c))
    gate_w = lambda: pl.BlockSpec((None, tc // hd, hd, hd), lambda i, c: (layer, c, 0, 0))
    unfold = lambda s: jnp.swapaxes(s, 1, 2).reshape(nseq, s.shape[2], W)
    hg, c_rows, h_rows = pl.pallas_call(
        kern,
        grid=(M // tm, nc),
        in_specs=[
            pl.BlockSpec((tm, tc), lambda i, c: (i, c)),
            pl.BlockSpec((tm, tc), lambda i, c: (i, nc + c)),
            pl.BlockSpec((None, conv_w.shape[1], tc), lambda i, c: (layer, 0, c)),
            row_vec(), gate_w(), row_vec(), gate_w(), row_vec(), row_vec(),
            pl.BlockSpec((None, cru, tc), lambda i, c: (i // tps, 0, c)),
            pl.BlockSpec((None, SUBLANES, tc), lambda i, c: (i // tps, 0, c)),
        ],
        out_specs=[
            pl.BlockSpec((tm, tc), lambda i, c: (i, c)),
            pl.BlockSpec((None, nc, cru, tc), lambda i, c: (i // tps, 0, 0, 0)),
            pl.BlockSpec((None, nc, SUBLANES, tc), lambda i, c: (i // tps, 0, 0, 0)),
        ],
        out_shape=[
            jax.ShapeDtypeStruct((M, W), BF16),
            jax.ShapeDtypeStruct((nseq, nc, cru, tc), F32),
            jax.ShapeDtypeStruct((nseq, nc, SUBLANES, tc), F32),
        ],
        scratch_shapes=[
            pltpu.VMEM((tm, tc), F32),
            pltpu.VMEM((tm, tc), F32),
            pltpu.VMEM((tm, tc), F32),
            pltpu.VMEM((nc, cru, tc), F32),
            pltpu.VMEM((nc, SUBLANES, tc), F32),
        ],
        compiler_params=_params(2),
        name="lru_core",
    )(proj, proj, conv_w, conv_b, w_a, b_a, w_i, b_i, lam, conv_init, h_init)
    return hg, unfold(c_rows), unfold(h_rows)


def _ffn_kernel(x_ref, g_ref, wg_ref, wv_ref, cwg_ref, cwv_ref, cbg_ref, cbv_ref, wd_ref, ig_ref, iv_ref, *rest,
                R, cr, tps, final):
    fg_ref = rest[0] if final else None
    o_ref, sg_ref, sv_ref, *emit, xn_ref, cg_ref, cv_ref = rest[1:] if final else rest
    i = pl.program_id(0)
    j = pl.program_id(1)
    tm = x_ref.shape[0]

    @pl.when(j == 0)
    def _():
        x = x_ref[...]
        xn_ref[...] = _rms(x, g_ref[...]).astype(BF16)
        o_ref[...] = x

    @pl.when(i % tps == 0)
    def _():
        cg_ref[j] = ig_ref[...]
        cv_ref[j] = iv_ref[...]

    def side(w_ref, cw_ref, cb_ref, c_ref, s_ref, emit_ref):
        up = jnp.dot(xn_ref[...], _as_bf16_weight(w_ref, emit_ref), preferred_element_type=F32)
        carry = c_ref[j]
        cw = cw_ref[...]
        conv = (cb_ref[...] + cw[2:3] * up + cw[1:2] * _shift_rows(up, carry, R) + cw[0:1] * _shift_rows(up, carry, 2 * R))
        tail = up[tm - cr:tm]
        c_ref[j] = tail
        s_ref[j] = tail
        return conv

    cg = side(wg_ref, cwg_ref, cbg_ref, cg_ref, sg_ref, emit[0:1])
    cv = side(wv_ref, cwv_ref, cbv_ref, cv_ref, sv_ref, emit[1:2])
    act = (_gelu(cg) * cv).astype(BF16)
    o_ref[...] += jnp.dot(act, _as_bf16_weight(wd_ref, emit[2:3]), preferred_element_type=F32)
    if final:
        @pl.when(j == pl.num_programs(1) - 1)
        def _():
            o_ref[...] = _rms(o_ref[...], fg_ref[...])


def conv_ffn(x, gains, w_gate, w_val, conv_w, conv_b, w_down, init, layer, tm, tf, R, final_gain=None):
    M, D = x.shape
    F = w_down.arr.shape[1]
    emitting = w_down.arr.dtype != BF16
    if emitting:
        tf = min(tf, FFN_TILE_F32_WEIGHT)
    nf = F // tf
    nseq, cr, _ = init.shape
    tps = (M // tm) // nseq
    final = final_gain is not None
    kern = functools.partial(_ffn_kernel, R=R, cr=cr, tps=tps, final=final)
    K = conv_w.shape[1]
    halves = lambda mk: [mk(0), mk(nf)]
    state_spec = lambda off: pl.BlockSpec((None, cr, tf), lambda i, j: (i // tps, 0, off + j))
    w_specs, emit_specs, emit_shapes = [], [], []
    for w, col0 in (w_gate, w_val):
        assert col0 % tf == 0 and (w.arr.dtype != BF16) == emitting
        spec, e_spec, e_shape = _weight_specs(w, (D, tf), lambda i, j, off=col0 // tf: (0, off + j))
        w_specs.append(spec)
        emit_specs += [pl.BlockSpec((None, D, tf), lambda i, j: (0, 0, j))] if e_spec else []
        emit_shapes += [jax.ShapeDtypeStruct((1, D, F), BF16)] if e_shape else []
    wd_spec, e_spec, e_shape = _weight_specs(w_down, (tf, D), lambda i, j: (j, 0))
    emit_specs += e_spec
    emit_shapes += e_shape
    assert not emit_specs or M == tm, "a weight tile must be visited once to be emitted"
    once = pl.Buffered(1)
    out, sg, sv, *emitted = pl.pallas_call(
        kern,
        grid=(M // tm, nf),
        in_specs=[
            pl.BlockSpec((tm, D), lambda i, j: (i, 0)),
            pl.BlockSpec((None, 1, D), lambda i, j: (layer, 0, 0)),
            *w_specs,
            *halves(lambda off: pl.BlockSpec((None, K, tf), lambda i, j: (layer, 0, off + j))),
            *halves(lambda off: pl.BlockSpec((None, 1, tf), lambda i, j: (layer, 0, off + j))),
            wd_spec,
            *halves(state_spec),
            *([pl.BlockSpec((1, D), lambda i, j: (0, 0))] if final else []),
        ],
        out_specs=[
            pl.BlockSpec((tm, D), lambda i, j: (i, 0), pipeline_mode=once),
            pl.BlockSpec((None, nf, cr, tf), lambda i, j: (i // tps, 0, 0, 0)),
            pl.BlockSpec((None, nf, cr, tf), lambda i, j: (i // tps, 0, 0, 0)),
            *emit_specs,
        ],
        out_shape=[
            jax.ShapeDtypeStruct((M, D), F32),
            jax.ShapeDtypeStruct((nseq, nf, cr, tf), F32),
            jax.ShapeDtypeStruct((nseq, nf, cr, tf), F32),
            *emit_shapes,
        ],
        scratch_shapes=[
            pltpu.VMEM((tm, D), BF16),
            pltpu.VMEM((nf, cr, tf), F32),
            pltpu.VMEM((nf, cr, tf), F32),
        ],
        compiler_params=_params(2),
        name="conv_ffn",
    )(x, gains, w_gate[0].arr, w_val[0].arr, conv_w, conv_w, conv_b, conv_b, w_down.arr, init, init,
      *([final_gain] if final else []))
    unfold = lambda s: jnp.swapaxes(s, 1, 2).reshape(nseq, cr, F)
    return out, jnp.concatenate([unfold(sg), unfold(sv)], axis=-1), (tuple(emitted) if emitted else None)


ATTN_SUPER = max(w for w, _ in ATTN_GROUPS)
ATTN_BATCH = 8
ATTN_TRANSPOSE_DIL = SUBLANES


def _attn_prompt_kernel(sl_ref, q0, k0, v0, q1, k1, v1, q2, k2, v2, o_ref,
                        kc0, vc0, kc1, vc1, kc2, vc2, o_scr, l_scr):
    n = pl.program_id(2)
    SB = o_ref.shape[0]
    blk = ATTN_BLK
    row = lax.broadcasted_iota(jnp.int32, (blk, 2 * blk), 0)
    col = lax.broadcasted_iota(jnp.int32, (blk, 2 * blk), 1)
    steps = row + blk - col
    band = jnp.logical_and(steps >= 0, steps <= blk)
    band_cur = jnp.logical_and(band, col >= blk)
    steps_f = steps.astype(F32)
    ones_v = jnp.ones((2 * blk, HEAD_DIM), BF16)
    inv_sqrt = 1.0 / math.sqrt(HEAD_DIM)
    nt = (((1,), (1,)), ((), ()))
    groups = zip(ATTN_GROUPS, (q0, q1, q2), (k0, k1, k2), (v0, v1, v2), (kc0, kc1, kc2), (vc0, vc1, vc2))
    for g, ((win, dil), q_ref, k_ref, v_ref, kcar, vcar) in enumerate(groups):
        @pl.when(n == 0)
        def _():
            kcar[...] = jnp.zeros(kcar.shape, kcar.dtype)
            vcar[...] = jnp.zeros(vcar.shape, vcar.dtype)

        alibi = (sl_ref[g:g + 1, :] * float(-dil)) * steps_f
        bias_full = jnp.where(band, alibi, NEG)
        bias_head = jnp.where(n > 0, bias_full, jnp.where(band_cur, alibi, NEG))

        def rows(start):
            return pl.ds(start, blk, stride=dil) if dil > 1 else pl.ds(start, blk)

        transposed = dil >= ATTN_TRANSPOSE_DIL
        if transposed:
            assert win == SB
            split = lambda ref: jnp.swapaxes(ref[...].astype(BF16).reshape(blk, dil, HEAD_DIM), 0, 1)
            q_ph, k_ph, v_ph = split(q_ref), split(k_ref), split(v_ref)
            k_prev, v_prev = kcar[...], vcar[...]
        o_parts, l_parts = [], []
        n_blocks = SB // blk
        for b0 in range(0, n_blocks, ATTN_BATCH):
            infos = []
            for bidx in range(b0, b0 + ATTN_BATCH):
                j, r = divmod(bidx, dil)
                infos.append((j, r, j * win + r))
            scores = []
            for j, r, p0 in infos:
                if transposed:
                    q, kp, kc = q_ph[r], k_prev[r], k_ph[r]
                else:
                    q = q_ref[rows(p0), :].astype(BF16)
                    kp = (kcar[rows(r), :] if j == 0 else k_ref[rows(p0 - win), :]).astype(BF16)
                    kc = k_ref[rows(p0), :].astype(BF16)
                scores.append(lax.dot_general(q, jnp.concatenate([kp, kc], axis=0), nt, preferred_element_type=F32))
            probs = []
            for (j, r, p0), s in zip(infos, scores):
                s = s * inv_sqrt + (bias_head if j == 0 else bias_full)
                m = jnp.max(jnp.maximum(s[:, :blk], s[:, blk:]), axis=-1, keepdims=True)
                probs.append((jnp.exp(s - m).astype(BF16), m))
            for (j, r, p0), (e, m) in zip(infos, probs):
                if transposed:
                    vp, vc = v_prev[r], v_ph[r]
                else:
                    vp = (vcar[rows(r), :] if j == 0 else v_ref[rows(p0 - win), :]).astype(BF16)
                    vc = v_ref[rows(p0), :].astype(BF16)
                vcat = jnp.concatenate([vp, vc], axis=0)
                res = jnp.dot(e, jnp.concatenate([vcat, ones_v], axis=1), preferred_element_type=F32)
                den = res[:, HEAD_DIM:]
                o_blk, l_blk = res[:, :HEAD_DIM] / den, m + jnp.log(den)
                if transposed:
                    o_parts.append(o_blk)
                    l_parts.append(l_blk)
                else:
                    o_scr[g, rows(p0), :] = o_blk
                    l_scr[g, rows(p0), :] = l_blk
        if transposed:
            merge = lambda parts: jnp.swapaxes(jnp.stack(parts, axis=0), 0, 1).reshape(SB, HEAD_DIM)
            o_scr[g] = merge(o_parts)
            l_scr[g] = merge(l_parts)
            kcar[...] = k_ph
            vcar[...] = v_ph
        else:
            kcar[...] = k_ref[SB - win:SB, :]
            vcar[...] = v_ref[SB - win:SB, :]

    chunk = 2 * blk
    for c0 in range(0, SB, chunk):
        ls = [l_scr[g, c0:c0 + chunk, :] for g in range(N_GROUPS)]
        mm = jnp.maximum(jnp.maximum(ls[0], ls[1]), ls[2])
        es = [jnp.exp(l - mm) for l in ls]
        acc = es[0] * o_scr[0, c0:c0 + chunk, :]
        for g in range(1, N_GROUPS):
            acc = acc + es[g] * o_scr[g, c0:c0 + chunk, :]
        o_ref[c0:c0 + chunk, :] = (acc / (es[0] + es[1] + es[2])).astype(BF16)


def attn_prompt(qkv_slabs, B, S):
    n_slabs, M, _ = qkv_slabs.shape
    SB = ATTN_SUPER
    assert n_slabs == 3 * N_GROUPS * GROUP_HEADS and S % SB == 0 and M == B * S
    assert all(w // d == ATTN_BLK for w, d in ATTN_GROUPS) and (SB // ATTN_BLK) % ATTN_BATCH == 0
    nsb = S // SB
    slopes = jnp.asarray(_alibi_slopes(), F32).T
    slopes = jnp.broadcast_to(slopes[:, :, None], (GROUP_HEADS, N_GROUPS, 2 * ATTN_BLK))

    def slab(g, comp):
        base = (g * 3 + comp) * GROUP_HEADS
        return pl.BlockSpec((None, SB, HEAD_DIM), lambda b, h, n: (base + h, b * nsb + n, 0))

    carry = [pltpu.VMEM((d, w // d, HEAD_DIM), BF16) if d >= ATTN_TRANSPOSE_DIL else pltpu.VMEM((w, HEAD_DIM), F32)
             for w, d in ATTN_GROUPS for _kv in range(2)]
    return pl.pallas_call(
        _attn_prompt_kernel,
        grid=(B, GROUP_HEADS, nsb),
        in_specs=[pl.BlockSpec((None, N_GROUPS, 2 * ATTN_BLK), lambda b, h, n: (h, 0, 0))]
        + [slab(g, comp) for g in range(N_GROUPS) for comp in range(3)],
        out_specs=pl.BlockSpec((None, SB, HEAD_DIM), lambda b, h, n: (h, b * nsb + n, 0)),
        out_shape=jax.ShapeDtypeStruct((GROUP_HEADS, M, HEAD_DIM), BF16),
        scratch_shapes=carry + [pltpu.VMEM((N_GROUPS, SB, HEAD_DIM), F32), pltpu.VMEM((N_GROUPS, SB, HEAD_DIM), F32)],
        compiler_params=_params(3),
        name="attn_prompt",
    )(slopes, *([qkv_slabs] * (3 * N_GROUPS)))


def _kv_tail_kernel(*refs):
    o_ref = refs[-1]
    for i, ref in enumerate(refs[:-1]):
        o_ref[i // 2, :, i % 2] = jnp.swapaxes(ref[...], 0, 1)


def kv_tails(qkv_layers, g, B, S, keep):
    chunk = min(keep, KV_TAIL_ROWS)
    assert keep % chunk == 0 and S % chunk == 0
    first = (S - keep) // chunk
    per_seq = S // chunk

    def slabs(comp):
        return pl.BlockSpec((GROUP_HEADS, chunk, HEAD_DIM), lambda b, t: (g * 3 + comp, b * per_seq + first + t, 0))

    n = len(qkv_layers)
    return pl.pallas_call(
        _kv_tail_kernel,
        grid=(B, keep // chunk),
        in_specs=[slabs(comp) for _ in range(n) for comp in (1, 2)],
        out_specs=pl.BlockSpec((n, None, chunk, 2, GROUP_HEADS, HEAD_DIM), lambda b, t: (0, b, t, 0, 0, 0)),
        out_shape=jax.ShapeDtypeStruct((n, B, keep, 2, GROUP_HEADS, HEAD_DIM), F32),
        compiler_params=_params(2),
        name="kv_tails",
    )(*[q for q in qkv_layers for _ in range(2)])


def _attn_sample_kernel(sl_ref, qkv_ref, c0_ref, c1_ref, c2_ref, o_ref):
    T = qkv_ref.shape[0]
    H = GROUP_HEADS
    caches = (c0_ref, c1_ref, c2_ref)
    rows = c0_ref.shape[0]
    l_idx = lax.broadcasted_iota(jnp.int32, (rows, H, 1), 0)
    l_f = l_idx.astype(F32)
    inv_sqrt = 1.0 / math.sqrt(HEAD_DIM)
    for t in range(T):
        outs, lses = [], []
        for g, (win, dil) in enumerate(ATTN_GROUPS):
            base = g * 3 * H
            slope = sl_ref[g][:, 0:1]
            q = qkv_ref[t, base:base + H, :]
            ph = 0 if dil == 1 else t
            kc = caches[g][:, ph * 2 * H:ph * 2 * H + H, :]
            vc = caches[g][:, ph * 2 * H + H:(ph + 1) * 2 * H, :]
            sc = jnp.sum(kc * q[None], axis=-1, keepdims=True) * inv_sqrt
            if dil == 1:
                sc = jnp.where(l_idx >= t, sc - slope[None] * (float(rows + t) - l_f), NEG)
                new_ts = list(range(t + 1))
            else:
                sc = sc - (slope[None] * float(dil)) * (float(rows) - l_f)
                new_ts = [t]
            m = jnp.max(sc, axis=0)
            s_new = []
            for t2 in new_ts:
                k2 = qkv_ref[t2, base + H:base + 2 * H, :]
                s2 = jnp.sum(q * k2, axis=-1, keepdims=True) * inv_sqrt - slope * float((t - t2) * dil)
                s_new.append(s2)
                m = jnp.maximum(m, s2)
            ec = jnp.exp(sc - m[None])
            den = jnp.sum(ec, axis=0)
            acc = jnp.sum(ec * vc, axis=0)
            for t2, s2 in zip(new_ts, s_new):
                v2 = qkv_ref[t2, base + 2 * H:base + 3 * H, :]
                e2 = jnp.exp(s2 - m)
                den = den + e2
                acc = acc + e2 * v2
            outs.append(acc / den)
            lses.append(m + jnp.log(den))
        mm = jnp.maximum(jnp.maximum(lses[0], lses[1]), lses[2])
        es = [jnp.exp(l - mm) for l in lses]
        o_ref[t] = (es[0] * outs[0] + es[1] * outs[1] + es[2] * outs[2]) / (es[0] + es[1] + es[2])


def attn_sample(qkv, caches, layer, T, Bs):
    H = GROUP_HEADS
    n_rows = qkv.shape[-1] // HEAD_DIM
    views, specs = [], []
    for (win, dil), c in zip(ATTN_GROUPS, caches):
        assert c.shape[1] == Bs and c.shape[2] == win and win // dil == ATTN_BLK and (dil == 1 or T <= dil)
        phases = min(dil, T)
        views.append(c.reshape(c.shape[0], Bs, win // dil, dil * 2 * H, HEAD_DIM))
        specs.append(pl.BlockSpec((None, None, win // dil, phases * 2 * H, HEAD_DIM), lambda b: (layer, b, 0, 0, 0)))
    slopes = jnp.broadcast_to(jnp.asarray(_alibi_slopes(), F32)[:, :, None], (N_GROUPS, H, HEAD_DIM))
    q_rows = jnp.swapaxes(qkv.reshape(T, Bs, n_rows, HEAD_DIM), 0, 1)
    out = pl.pallas_call(
        _attn_sample_kernel,
        grid=(Bs,),
        in_specs=[pl.BlockSpec((N_GROUPS, H, HEAD_DIM), lambda b: (0, 0, 0)),
                  pl.BlockSpec((None, T, n_rows, HEAD_DIM), lambda b: (b, 0, 0, 0)), *specs],
        out_specs=pl.BlockSpec((None, T, H, HEAD_DIM), lambda b: (b, 0, 0, 0)),
        out_shape=jax.ShapeDtypeStruct((Bs, T, H, HEAD_DIM), F32),
        compiler_params=_params(1),
        name="attn_sample",
    )(slopes, q_rows, *views)
    return jnp.swapaxes(out, 0, 1).reshape(T * Bs, ATTN_WIDTH)


class Tiles(NamedTuple):
    seq: int
    mm: int
    ffn: int
    tf: int


def _run_trunk(x, nseq, R, tiles, lru_h, lru_conv, kv_caches, ffn_conv, P, W, dims):
    depth = P["norm_mix"].shape[0]
    new_h, new_lconv, new_fconv, qkvs = [], [], [], []
    bf16_w = {name: [] for name in W}
    for layer in range(depth):
        j = layer // 2
        if layer % 2 == 0:
            proj, wb = norm_matmul(x, P["norm_mix"], layer, W["lru_w_in"][j], P["lru_b_in"], j, tm=tiles.mm)
            bf16_w["lru_w_in"].append(wb)
            hg, c_rows, h_rows = lru_core(proj, P["lru_conv_w"], P["lru_conv_b"], P["lru_w_a"], P["lru_b_a"],
                                          P["lru_w_i"], P["lru_b_i"], P["lru_lambda"], lru_conv[j], lru_h[j],
                                          layer=j, tm=tiles.seq, R=R)
            x, wb = matmul_res(hg, W["lru_w_out"][j], P["lru_b_out"], j, x, tm=tiles.mm)
            bf16_w["lru_w_out"].append(wb)
            new_h.append(h_rows)
            new_lconv.append(c_rows)
        else:
            prompt = kv_caches is None
            qkv, wb = norm_matmul(x, P["norm_mix"], layer, W["attn_w_qkv"][j], P["attn_b_zero"], j, tm=tiles.mm,
                                  slab_out=prompt)
            bf16_w["attn_w_qkv"].append(wb)
            qkvs.append(qkv)
            if prompt:
                o = attn_prompt(qkv, *dims)
            else:
                o = attn_sample(qkv, kv_caches, layer=j, T=dims[0], Bs=dims[1])
            x, wb = matmul_res(o, W["attn_w_o"][j], P["attn_bo_zero"], j, x, tm=tiles.mm)
            bf16_w["attn_w_o"].append(wb)
        x, f_rows, wbs = conv_ffn(x, P["norm_ffn"], W["ffn_w_gate"][layer], W["ffn_w_val"][layer], P["ffn_conv_w"],
                                  P["ffn_conv_b"], W["ffn_w_down"][layer], ffn_conv[layer], layer=layer,
                                  tm=tiles.ffn, tf=tiles.tf, R=R,
                                  final_gain=P["norm_final"] if layer == depth - 1 else None)
        for name, wb in zip(("ffn_w_gate", "ffn_w_val", "ffn_w_down"), wbs or (None,) * 3):
            bf16_w[name].append(wb)
        new_fconv.append(f_rows)
    return x, new_h, new_lconv, qkvs, new_fconv, bf16_w


def kernel(x_prompt, x_sample, cache_kv_w128, cache_kv_w512, cache_kv_w2048, state_lru_h, state_lru_conv, state_ffn_conv, norm_mix, norm_ffn, norm_final, lru_w_in, lru_b_in, lru_conv_w, lru_conv_b, lru_w_a, lru_b_a, lru_w_i, lru_b_i, lru_lambda, lru_w_out, lru_b_out, attn_w_qkv, attn_w_o, ffn_w_up, ffn_conv_w, ffn_conv_b, ffn_w_down):
    B, S, D = x_prompt.shape
    Bs, T, _ = x_sample.shape
    depth = norm_mix.shape[0]
    n_lru, W = lru_lambda.shape
    n_attn = attn_w_qkv.shape[0]
    F2 = ffn_w_up.shape[-1]
    assert Bs == SUBLANES, "the sample group is laid out time-major with one sublane per sequence"
    row3 = lambda a: a.reshape(a.shape[0], 1, a.shape[-1])
    P = {
        "norm_mix": row3(norm_mix), "norm_ffn": row3(norm_ffn), "norm_final": norm_final.reshape(1, D),
        "lru_b_in": row3(lru_b_in), "lru_conv_w": lru_conv_w, "lru_conv_b": row3(lru_conv_b),
        "lru_w_a": lru_w_a, "lru_b_a": row3(lru_b_a), "lru_w_i": lru_w_i, "lru_b_i": row3(lru_b_i),
        "lru_lambda": row3(lru_lambda), "lru_b_out": row3(lru_b_out),
        "attn_b_zero": jnp.zeros((n_attn, 1, attn_w_qkv.shape[-1]), F32),
        "attn_bo_zero": jnp.zeros((n_attn, 1, D), F32),
        "ffn_conv_w": ffn_conv_w, "ffn_conv_b": row3(ffn_conv_b),
    }
    stack = lambda w: [Weight(w, i) for i in range(w.shape[0])]
    W_f32 = {
        "lru_w_in": stack(lru_w_in), "lru_w_out": stack(lru_w_out),
        "attn_w_qkv": stack(attn_w_qkv), "attn_w_o": stack(attn_w_o),
        "ffn_w_gate": [(w, 0) for w in stack(ffn_w_up)], "ffn_w_val": [(w, F2 // 2) for w in stack(ffn_w_up)],
        "ffn_w_down": stack(ffn_w_down),
    }
    kc = lru_conv_w.shape[1] - 1
    kf = ffn_conv_w.shape[1] - 1

    tmaj = lambda a: jnp.swapaxes(a, 0, 1).reshape(1, a.shape[1] * a.shape[0], a.shape[-1])
    y_s, h_s, lc_s, qkv_s, fc_s, bf16_w = _run_trunk(
        jnp.swapaxes(x_sample, 0, 1).reshape(T * Bs, D), 1, Bs,
        Tiles(seq=T * Bs, mm=T * Bs, ffn=T * Bs, tf=_pick(F2 // 2, FFN_TILE_FEW_ROWS)),
        [state_lru_h[j].reshape(1, Bs, W) for j in range(n_lru)],
        [tmaj(state_lru_conv[j]) for j in range(n_lru)],
        (cache_kv_w128, cache_kv_w512, cache_kv_w2048),
        [tmaj(state_ffn_conv[l]) for l in range(depth)], P, W_f32, (T, Bs))

    W_bf16 = {name: [(Weight(w, 0), 0) if name in ("ffn_w_gate", "ffn_w_val") else Weight(w, 0) for w in ws]
              for name, ws in bf16_w.items()}

    row_tile = lambda cap: cap if S % cap == 0 else S
    tiles_p = Tiles(seq=row_tile(ROW_TILE_LRU), mm=row_tile(ROW_TILE_MATMUL), ffn=row_tile(ROW_TILE_MATMUL),
                    tf=_pick(F2 // 2, FFN_TILE))
    zeros = lambda n, w: [jnp.zeros((B, SUBLANES, w), F32)] * n
    y_p, h_p, lc_p, qkv_p, fc_p, _ = _run_trunk(
        x_prompt.reshape(B * S, D), B, 1, tiles_p, zeros(n_lru, W), zeros(n_lru, W), None, zeros(depth, F2), P, W_bf16, (B, S))

    bmaj = lambda a, k: jnp.swapaxes(a.reshape(-1, Bs, a.shape[-1])[-k:], 0, 1)
    kv_p, kv_s = [], []
    for g, (win, dil) in enumerate(ATTN_GROUPS):
        lo = (g * 3 + 1) * ATTN_WIDTH
        keep = min(win, S)
        kv_p.append(kv_tails(qkv_p, g, B, S, keep))
        kv_s.append(jnp.stack([jnp.swapaxes(q.reshape(T, Bs, -1), 0, 1)[:, :, lo:lo + 2 * ATTN_WIDTH]
                               .reshape(Bs, T, 2, GROUP_HEADS, HEAD_DIM) for q in qkv_s], axis=0))
    return (
        y_p.reshape(B, S, D),
        jnp.swapaxes(y_s.reshape(T, Bs, D), 0, 1),
        kv_p[0], kv_p[1], kv_p[2],
        jnp.stack([h[:, SUBLANES - 1] for h in h_p], axis=0),
        jnp.stack([c[:, SUBLANES - kc:] for c in lc_p], axis=0),
        jnp.stack([f[:, SUBLANES - kf:] for f in fc_p], axis=0),
        kv_s[0], kv_s[1], kv_s[2],
        jnp.stack([h[0] for h in h_s], axis=0),
        jnp.stack([bmaj(c[0], kc) for c in lc_s], axis=0),
        jnp.stack([bmaj(f[0], kf) for f in fc_s], axis=0),
    )
```

```python
import functools
import math
from typing import NamedTuple

import jax
import jax.numpy as jnp
from jax import lax
from jax.experimental import pallas as pl
from jax.experimental.pallas import tpu as pltpu

EPS = 1e-6
NEG = -1e30
LRU_C = 8.0
ATTN_GROUPS = ((128, 1), (512, 4), (2048, 16))
N_GROUPS = len(ATTN_GROUPS)
GROUP_HEADS = 8
HEAD_DIM = 128
ATTN_WIDTH = GROUP_HEADS * HEAD_DIM
ATTN_BLK = 128
SUBLANES = 8
LANES = 128
VMEM_LIMIT = 56 * 1024 * 1024

ROW_TILE_MATMUL = 1024
ROW_TILE_LRU = 1024
COL_TILE_MATMUL = 2048
COL_TILE_RESIDUAL = 1024
COL_TILE_FEW_ROWS = 2048
COL_TILE_F32_WEIGHT = 1024
FEW_ROWS = 256
LRU_CHANNEL_TILE = 1024
FFN_TILE = 768
FFN_TILE_FEW_ROWS = 1024
FFN_TILE_F32_WEIGHT = 512
KV_TAIL_ROWS = 256

F32 = jnp.float32
F32_TINY = 1.1754944e-38
BF16 = jnp.bfloat16


def _alibi_slopes():
    n = N_GROUPS * GROUP_HEADS
    return [[2.0 ** (-8.0 * (g * GROUP_HEADS + h + 1) / n) for h in range(GROUP_HEADS)] for g in range(N_GROUPS)]


def _gelu(x):
    c = math.sqrt(2.0 / math.pi)
    return x * (0.5 * (1.0 + jnp.tanh(c * (x + 0.044715 * (x * x * x)))))


def _rms(x, g):
    ms = jnp.mean(x * x, axis=-1, keepdims=True)
    return x * lax.rsqrt(ms + EPS) * g


def _shift_rows(x, carry, s):
    tm, width = x.shape
    cr = carry.shape[0]
    if s % SUBLANES == 0:
        return jnp.concatenate([carry[cr - s:cr], x[:tm - s]], axis=0)
    assert s < SUBLANES
    groups = pltpu.roll(x.reshape(tm // SUBLANES, SUBLANES, width), s, axis=1)
    before = jnp.concatenate([pltpu.roll(carry[cr - SUBLANES:cr], s, axis=0)[None], groups[:-1]], axis=0)
    row = lax.broadcasted_iota(jnp.int32, groups.shape, 1)
    return jnp.where(row < s, before, groups).reshape(tm, width)


def _pick(n, cap):
    best = None
    for t in range(LANES, min(n, cap) + 1, LANES):
        if n % t == 0:
            best = t
    assert best is not None, (n, cap)
    return best


def _params(n_axes):
    return pltpu.CompilerParams(dimension_semantics=("arbitrary",) * n_axes, vmem_limit_bytes=VMEM_LIMIT)


def _as_bf16_weight(w_ref, emit_refs):
    w = w_ref[...]
    if w.dtype != BF16:
        w = w.astype(BF16)
        emit_refs[0][...] = w
    return w


def _norm_matmul_kernel(x_ref, g_ref, w_ref, b_ref, o_ref, *rest):
    xn_ref = rest[-1]

    @pl.when(pl.program_id(1) == 0)
    def _():
        xn_ref[...] = _rms(x_ref[...], g_ref[...]).astype(BF16)

    acc = jnp.dot(xn_ref[...], _as_bf16_weight(w_ref, rest), preferred_element_type=F32) + b_ref[...]
    if len(o_ref.shape) == 2:
        o_ref[...] = acc
    else:
        for c in range(o_ref.shape[0]):
            o_ref[c] = acc[:, c * LANES:(c + 1) * LANES]


class Weight(NamedTuple):
    arr: jax.Array
    layer: int


def _weight_specs(w, block, index):
    spec = pl.BlockSpec((None, *block), lambda *ij: (w.layer, *index(*ij)))
    if w.arr.dtype == BF16:
        return spec, [], []
    return (spec, [pl.BlockSpec((None, *block), lambda *ij: (0, *index(*ij)))],
            [jax.ShapeDtypeStruct((1, *w.arr.shape[1:]), BF16)])


def _tile_cap(tm, w, many_rows_cap):
    if w.arr.dtype != BF16:
        return COL_TILE_F32_WEIGHT
    return many_rows_cap if tm > FEW_ROWS else COL_TILE_FEW_ROWS


def norm_matmul(x, gains, g_layer, w, bias, b_layer, tm, slab_out=False):
    M, D = x.shape
    N = w.arr.shape[-1]
    tn = _pick(N, _tile_cap(tm, w, COL_TILE_MATMUL))
    if slab_out:
        out_spec = pl.BlockSpec((tn // LANES, tm, LANES), lambda i, j: (j, i, 0))
        out_shape = jax.ShapeDtypeStruct((N // LANES, M, LANES), F32)
    else:
        out_spec = pl.BlockSpec((tm, tn), lambda i, j: (i, j))
        out_shape = jax.ShapeDtypeStruct((M, N), F32)
    w_spec, emit_specs, emit_shapes = _weight_specs(w, (D, tn), lambda i, j: (0, j))
    assert not emit_specs or M == tm, "a weight tile must be visited once to be emitted"
    out, *emitted = pl.pallas_call(
        _norm_matmul_kernel,
        grid=(M // tm, N // tn),
        in_specs=[
            pl.BlockSpec((tm, D), lambda i, j: (i, 0)),
            pl.BlockSpec((None, 1, D), lambda i, j: (g_layer, 0, 0)),
            w_spec,
            pl.BlockSpec((None, 1, tn), lambda i, j: (b_layer, 0, j)),
        ],
        out_specs=[out_spec, *emit_specs],
        out_shape=[out_shape, *emit_shapes],
        scratch_shapes=[pltpu.VMEM((tm, D), BF16)],
        compiler_params=_params(2),
        name="norm_matmul",
    )(x, gains, w.arr, bias)
    return out, (emitted[0] if emitted else None)


def _matmul_res_kernel(a_ref, w_ref, b_ref, r_ref, o_ref, *emit):
    if len(a_ref.shape) == 2:
        a = a_ref[...]
    else:
        a = jnp.concatenate([a_ref[c] for c in range(a_ref.shape[0])], axis=1)
    o_ref[...] = r_ref[...] + b_ref[...] + jnp.dot(a.astype(BF16), _as_bf16_weight(w_ref, emit),
                                                   preferred_element_type=F32)


def matmul_res(a, w, bias, b_layer, res, tm):
    M, N = res.shape
    K = w.arr.shape[1]
    tn = _pick(N, _tile_cap(tm, w, COL_TILE_RESIDUAL))
    if a.ndim == 2:
        a_spec = pl.BlockSpec((tm, K), lambda i, j: (i, 0))
    else:
        a_spec = pl.BlockSpec((K // LANES, tm, LANES), lambda i, j: (0, i, 0))
    w_spec, emit_specs, emit_shapes = _weight_specs(w, (K, tn), lambda i, j: (0, j))
    assert not emit_specs or M == tm, "a weight tile must be visited once to be emitted"
    out, *emitted = pl.pallas_call(
        _matmul_res_kernel,
        grid=(M // tm, N // tn),
        in_specs=[
            a_spec,
            w_spec,
            pl.BlockSpec((None, 1, tn), lambda i, j: (b_layer, 0, j)),
            pl.BlockSpec((tm, tn), lambda i, j: (i, j)),
        ],
        out_specs=[pl.BlockSpec((tm, tn), lambda i, j: (i, j)), *emit_specs],
        out_shape=[jax.ShapeDtypeStruct((M, N), F32), *emit_shapes],
        compiler_params=_params(2),
        name="matmul_res",
    )(a, w.arr, bias, res)
    return out, (emitted[0] if emitted else None)


def _lru_kernel(gate_ref, u_ref, cw_ref, cb_ref, wa_ref, ba_ref, wi_ref, bi_ref, lam_ref, ci_ref, hi_ref,
                o_ref, cs_ref, hs_ref,
                a_scr, b_scr, h_scr, cc_ref, hc_ref, *, R, cru, tps):
    i = pl.program_id(0)
    c = pl.program_id(1)
    tm, tc = u_ref.shape

    @pl.when(i % tps == 0)
    def _():
        cc_ref[c] = ci_ref[...]
        hc_ref[c] = hi_ref[...]

    u = u_ref[...]
    carry = cc_ref[c]
    cw = cw_ref[...]
    uc = (cb_ref[...] + cw[3:4] * u + cw[2:3] * _shift_rows(u, carry, R)
          + cw[1:2] * _shift_rows(u, carry, 2 * R) + cw[0:1] * _shift_rows(u, carry, 3 * R))
    tail = u[tm - cru:tm]
    cc_ref[c] = tail
    cs_ref[c] = tail

    hd = wa_ref.shape[-1]
    ucb = uc.astype(BF16)
    ra, ri = [], []
    for hh in range(tc // hd):
        ub = ucb[:, hh * hd:(hh + 1) * hd]
        ra.append(jnp.dot(ub, wa_ref[hh].astype(BF16), preferred_element_type=F32))
        ri.append(jnp.dot(ub, wi_ref[hh].astype(BF16), preferred_element_type=F32))
    r = jax.nn.sigmoid(jnp.concatenate(ra, axis=1) + ba_ref[...])
    ig = jax.nn.sigmoid(jnp.concatenate(ri, axis=1) + bi_ref[...])
    nlam = -lam_ref[...]
    softplus = jnp.maximum(nlam, 0.0) + jnp.log1p(jnp.exp(-jnp.abs(nlam)))
    log_a = (-LRU_C) * r * softplus
    a = jnp.exp(log_a)
    th = jnp.tanh(log_a)
    z = (-2.0 * th) / (1.0 - th)
    bx = (z * lax.rsqrt(jnp.maximum(z, F32_TINY))) * (ig * uc)

    if R == 1:
        A = a.reshape(tm // SUBLANES, SUBLANES, tc)
        B = bx.reshape(tm // SUBLANES, SUBLANES, tc)
        row = lax.broadcasted_iota(jnp.int32, A.shape, 1)
        s = 1
        while s < SUBLANES:
            m = row >= s
            B = jnp.where(m, A * pltpu.roll(B, s, axis=1) + B, B)
            A = jnp.where(m, A * pltpu.roll(A, s, axis=1), A)
            s *= 2
        a_scr[...] = A.reshape(tm, tc)
        b_scr[...] = B.reshape(tm, tc)
        h0 = hc_ref[c][SUBLANES - 1:SUBLANES, :]
    else:
        assert R == SUBLANES
        a_scr[...] = a
        b_scr[...] = bx
        h0 = hc_ref[c]

    def body(g, h):
        r0 = pl.multiple_of(g * SUBLANES, SUBLANES)
        hg = b_scr[pl.ds(r0, SUBLANES), :] + a_scr[pl.ds(r0, SUBLANES), :] * h
        h_scr[pl.ds(r0, SUBLANES), :] = hg
        return hg[SUBLANES - 1:SUBLANES, :] if R == 1 else hg

    n_groups = tm // SUBLANES
    lax.fori_loop(0, n_groups, body, h0, unroll=min(n_groups, 8))
    h_tail = h_scr[tm - SUBLANES:tm, :]
    hc_ref[c] = h_tail
    hs_ref[c] = h_tail
    o_ref[...] = (h_scr[...] * _gelu(gate_ref[...])).astype(BF16)


def lru_core(proj, conv_w, conv_b, w_a, b_a, w_i, b_i, lam, conv_init, h_init, layer, tm, R):
    M, W2 = proj.shape
    W = W2 // 2
    hd = w_a.shape[-1]
    tc = max(hd, _pick(W, LRU_CHANNEL_TILE))
    assert tc % hd == 0 and W % tc == 0
    nc = W // tc
    nseq, cru, _ = conv_init.shape
    tps = (M // tm) // nseq
    kern = functools.partial(_lru_kernel, R=R, cru=cru, tps=tps)
    row_vec = lambda: pl.BlockSpec((None, 1, tc), lambda i, c: (layer, 0, c))
    gate_w = lambda: pl.BlockSpec((None, tc // hd, hd, hd), lambda i, c: (layer, c, 0, 0))
    unfold = lambda s: jnp.swapaxes(s, 1, 2).reshape(nseq, s.shape[2], W)
    hg, c_rows, h_rows = pl.pallas_call(
        kern,
        grid=(M // tm, nc),
        in_specs=[
            pl.BlockSpec((tm, tc), lambda i, c: (i, c)),
            pl.BlockSpec((tm, tc), lambda i, c: (i, nc + c)),
            pl.BlockSpec((None, conv_w.shape[1], tc), lambda i, c: (layer, 0, c)),
            row_vec(), gate_w(), row_vec(), gate_w(), row_vec(), row_vec(),
            pl.BlockSpec((None, cru, tc), lambda i, c: (i // tps, 0, c)),
            pl.BlockSpec((None, SUBLANES, tc), lambda i, c: (i // tps, 0, c)),
        ],
        out_specs=[
            pl.BlockSpec((tm, tc), lambda i, c: (i, c)),
            pl.BlockSpec((None, nc, cru, tc), lambda i, c: (i // tps, 0, 0, 0)),
            pl.BlockSpec((None, nc, SUBLANES, tc), lambda i, c: (i // tps, 0, 0, 0)),
        ],
        out_shape=[
            jax.ShapeDtypeStruct((M, W), BF16),
            jax.ShapeDtypeStruct((nseq, nc, cru, tc), F32),
            jax.ShapeDtypeStruct((nseq, nc, SUBLANES, tc), F32),
        ],
        scratch_shapes=[
            pltpu.VMEM((tm, tc), F32),
            pltpu.VMEM((tm, tc), F32),
            pltpu.VMEM((tm, tc), F32),
            pltpu.VMEM((nc, cru, tc), F32),
            pltpu.VMEM((nc, SUBLANES, tc), F32),
        ],
        compiler_params=_params(2),
        name="lru_core",
    )(proj, proj, conv_w, conv_b, w_a, b_a, w_i, b_i, lam, conv_init, h_init)
    return hg, unfold(c_rows), unfold(h_rows)


def _ffn_kernel(x_ref, g_ref, wg_ref, wv_ref, cwg_ref, cwv_ref, cbg_ref, cbv_ref, wd_ref, ig_ref, iv_ref, *rest,
                R, cr, tps, final):
    fg_ref = rest[0] if final else None
    o_hbm, sg_ref, sv_ref, *emit, xn_ref, cg_ref, cv_ref, acc_ref, out_sem = rest[1:] if final else rest
    i = pl.program_id(0)
    j = pl.program_id(1)
    tm = x_ref.shape[0]

    def write_back(tile):
        rows = pl.ds(pl.multiple_of(tile * tm, tm), tm)
        return pltpu.make_async_copy(acc_ref, o_hbm.at[rows, :], out_sem)

    @pl.when(j == 0)
    def _():
        xn_ref[...] = _rms(x_ref[...], g_ref[...]).astype(BF16)

    @pl.when(jnp.logical_and(j == 0, i > 0))
    def _():
        write_back(i - 1).wait()

    @pl.when(j == 0)
    def _():
        acc_ref[...] = x_ref[...]

    @pl.when(i % tps == 0)
    def _():
        cg_ref[j] = ig_ref[...]
        cv_ref[j] = iv_ref[...]

    def side(w_ref, cw_ref, cb_ref, c_ref, s_ref, emit_ref):
        up = jnp.dot(xn_ref[...], _as_bf16_weight(w_ref, emit_ref), preferred_element_type=F32)
        carry = c_ref[j]
        cw = cw_ref[...]
        conv = (cb_ref[...] + cw[2:3] * up + cw[1:2] * _shift_rows(up, carry, R) + cw[0:1] * _shift_rows(up, carry, 2 * R))
        tail = up[tm - cr:tm]
        c_ref[j] = tail
        s_ref[j] = tail
        return conv

    cg = side(wg_ref, cwg_ref, cbg_ref, cg_ref, sg_ref, emit[0:1])
    cv = side(wv_ref, cwv_ref, cbv_ref, cv_ref, sv_ref, emit[1:2])
    act = (_gelu(cg) * cv).astype(BF16)
    acc_ref[...] += jnp.dot(act, _as_bf16_weight(wd_ref, emit[2:3]), preferred_element_type=F32)

    @pl.when(j == pl.num_programs(1) - 1)
    def _():
        if final:
            acc_ref[...] = _rms(acc_ref[...], fg_ref[...])
        write_back(i).start()

    @pl.when(jnp.logical_and(j == pl.num_programs(1) - 1, i == pl.num_programs(0) - 1))
    def _():
        write_back(i).wait()


def conv_ffn(x, gains, w_gate, w_val, conv_w, conv_b, w_down, init, layer, tm, tf, R, final_gain=None):
    M, D = x.shape
    F = w_down.arr.shape[1]
    emitting = w_down.arr.dtype != BF16
    if emitting:
        tf = min(tf, FFN_TILE_F32_WEIGHT)
    nf = F // tf
    nseq, cr, _ = init.shape
    tps = (M // tm) // nseq
    final = final_gain is not None
    kern = functools.partial(_ffn_kernel, R=R, cr=cr, tps=tps, final=final)
    K = conv_w.shape[1]
    halves = lambda mk: [mk(0), mk(nf)]
    state_spec = lambda off: pl.BlockSpec((None, cr, tf), lambda i, j: (i // tps, 0, off + j))
    w_specs, emit_specs, emit_shapes = [], [], []
    for w, col0 in (w_gate, w_val):
        assert col0 % tf == 0 and (w.arr.dtype != BF16) == emitting
        spec, e_spec, e_shape = _weight_specs(w, (D, tf), lambda i, j, off=col0 // tf: (0, off + j))
        w_specs.append(spec)
        emit_specs += [pl.BlockSpec((None, D, tf), lambda i, j: (0, 0, j))] if e_spec else []
        emit_shapes += [jax.ShapeDtypeStruct((1, D, F), BF16)] if e_shape else []
    wd_spec, e_spec, e_shape = _weight_specs(w_down, (tf, D), lambda i, j: (j, 0))
    emit_specs += e_spec
    emit_shapes += e_shape
    assert not emit_specs or M == tm, "a weight tile must be visited once to be emitted"
    out, sg, sv, *emitted = pl.pallas_call(
        kern,
        grid=(M // tm, nf),
        in_specs=[
            pl.BlockSpec((tm, D), lambda i, j: (i, 0)),
            pl.BlockSpec((None, 1, D), lambda i, j: (layer, 0, 0)),
            *w_specs,
            *halves(lambda off: pl.BlockSpec((None, K, tf), lambda i, j: (layer, 0, off + j))),
            *halves(lambda off: pl.BlockSpec((None, 1, tf), lambda i, j: (layer, 0, off + j))),
            wd_spec,
            *halves(state_spec),
            *([pl.BlockSpec((1, D), lambda i, j: (0, 0))] if final else []),
        ],
        out_specs=[
            pl.BlockSpec(memory_space=pl.ANY),
            pl.BlockSpec((None, nf, cr, tf), lambda i, j: (i // tps, 0, 0, 0)),
            pl.BlockSpec((None, nf, cr, tf), lambda i, j: (i // tps, 0, 0, 0)),
            *emit_specs,
        ],
        out_shape=[
            jax.ShapeDtypeStruct((M, D), F32),
            jax.ShapeDtypeStruct((nseq, nf, cr, tf), F32),
            jax.ShapeDtypeStruct((nseq, nf, cr, tf), F32),
            *emit_shapes,
        ],
        scratch_shapes=[
            pltpu.VMEM((tm, D), BF16),
            pltpu.VMEM((nf, cr, tf), F32),
            pltpu.VMEM((nf, cr, tf), F32),
            pltpu.VMEM((tm, D), F32),
            pltpu.SemaphoreType.DMA(()),
        ],
        compiler_params=_params(2),
        name="conv_ffn",
    )(x, gains, w_gate[0].arr, w_val[0].arr, conv_w, conv_w, conv_b, conv_b, w_down.arr, init, init,
      *([final_gain] if final else []))
    unfold = lambda s: jnp.swapaxes(s, 1, 2).reshape(nseq, cr, F)
    return out, jnp.concatenate([unfold(sg), unfold(sv)], axis=-1), (tuple(emitted) if emitted else None)


ATTN_SUPER = max(w for w, _ in ATTN_GROUPS)
ATTN_BATCH = 8
ATTN_TRANSPOSE_DIL = SUBLANES


def _attn_prompt_kernel(sl_ref, q0, k0, v0, q1, k1, v1, q2, k2, v2, o_ref,
                        kc0, vc0, kc1, vc1, kc2, vc2, o_scr, l_scr):
    n = pl.program_id(2)
    SB = o_ref.shape[0]
    blk = ATTN_BLK
    row = lax.broadcasted_iota(jnp.int32, (blk, 2 * blk), 0)
    col = lax.broadcasted_iota(jnp.int32, (blk, 2 * blk), 1)
    steps = row + blk - col
    band = jnp.logical_and(steps >= 0, steps <= blk)
    band_cur = jnp.logical_and(band, col >= blk)
    steps_f = steps.astype(F32)
    ones_v = jnp.ones((2 * blk, HEAD_DIM), BF16)
    inv_sqrt = 1.0 / math.sqrt(HEAD_DIM)
    nt = (((1,), (1,)), ((), ()))
    groups = zip(ATTN_GROUPS, (q0, q1, q2), (k0, k1, k2), (v0, v1, v2), (kc0, kc1, kc2), (vc0, vc1, vc2))
    for g, ((win, dil), q_ref, k_ref, v_ref, kcar, vcar) in enumerate(groups):
        @pl.when(n == 0)
        def _():
            kcar[...] = jnp.zeros(kcar.shape, kcar.dtype)
            vcar[...] = jnp.zeros(vcar.shape, vcar.dtype)

        alibi = (sl_ref[g:g + 1, :] * float(-dil)) * steps_f
        bias_full = jnp.where(band, alibi, NEG)
        bias_head = jnp.where(n > 0, bias_full, jnp.where(band_cur, alibi, NEG))

        def rows(start):
            return pl.ds(start, blk, stride=dil) if dil > 1 else pl.ds(start, blk)

        transposed = dil >= ATTN_TRANSPOSE_DIL
        if transposed:
            assert win == SB
            split = lambda ref: jnp.swapaxes(ref[...].astype(BF16).reshape(blk, dil, HEAD_DIM), 0, 1)
            q_ph, k_ph, v_ph = split(q_ref), split(k_ref), split(v_ref)
            k_prev, v_prev = kcar[...], vcar[...]
        o_parts, l_parts = [], []
        n_blocks = SB // blk
        for b0 in range(0, n_blocks, ATTN_BATCH):
            infos = []
            for bidx in range(b0, b0 + ATTN_BATCH):
                j, r = divmod(bidx, dil)
                infos.append((j, r, j * win + r))
            scores = []
            for j, r, p0 in infos:
                if transposed:
                    q, kp, kc = q_ph[r], k_prev[r], k_ph[r]
                else:
                    q = q_ref[rows(p0), :].astype(BF16)
                    kp = (kcar[rows(r), :] if j == 0 else k_ref[rows(p0 - win), :]).astype(BF16)
                    kc = k_ref[rows(p0), :].astype(BF16)
                scores.append(lax.dot_general(q, jnp.concatenate([kp, kc], axis=0), nt, preferred_element_type=F32))
            probs = []
            for (j, r, p0), s in zip(infos, scores):
                s = s * inv_sqrt + (bias_head if j == 0 else bias_full)
                m = jnp.max(jnp.maximum(s[:, :blk], s[:, blk:]), axis=-1, keepdims=True)
                probs.append((jnp.exp(s - m).astype(BF16), m))
            for (j, r, p0), (e, m) in zip(infos, probs):
                if transposed:
                    vp, vc = v_prev[r], v_ph[r]
                else:
                    vp = (vcar[rows(r), :] if j == 0 else v_ref[rows(p0 - win), :]).astype(BF16)
                    vc = v_ref[rows(p0), :].astype(BF16)
                vcat = jnp.concatenate([vp, vc], axis=0)
                res = jnp.dot(e, jnp.concatenate([vcat, ones_v], axis=1), preferred_element_type=F32)
                den = res[:, HEAD_DIM:]
                o_blk, l_blk = res[:, :HEAD_DIM] / den, m + jnp.log(den)
                if transposed:
                    o_parts.append(o_blk)
                    l_parts.append(l_blk)
                else:
                    o_scr[g, rows(p0), :] = o_blk
                    l_scr[g, rows(p0), :] = l_blk
        if transposed:
            merge = lambda parts: jnp.swapaxes(jnp.stack(parts, axis=0), 0, 1).reshape(SB, HEAD_DIM)
            o_scr[g] = merge(o_parts)
            l_scr[g] = merge(l_parts)
            kcar[...] = k_ph
            vcar[...] = v_ph
        else:
            kcar[...] = k_ref[SB - win:SB, :]
            vcar[...] = v_ref[SB - win:SB, :]

    chunk = 2 * blk
    for c0 in range(0, SB, chunk):
        ls = [l_scr[g, c0:c0 + chunk, :] for g in range(N_GROUPS)]
        mm = jnp.maximum(jnp.maximum(ls[0], ls[1]), ls[2])
        es = [jnp.exp(l - mm) for l in ls]
        acc = es[0] * o_scr[0, c0:c0 + chunk, :]
        for g in range(1, N_GROUPS):
            acc = acc + es[g] * o_scr[g, c0:c0 + chunk, :]
        o_ref[c0:c0 + chunk, :] = (acc / (es[0] + es[1] + es[2])).astype(BF16)


def attn_prompt(qkv_slabs, B, S):
    n_slabs, M, _ = qkv_slabs.shape
    SB = ATTN_SUPER
    assert n_slabs == 3 * N_GROUPS * GROUP_HEADS and S % SB == 0 and M == B * S
    assert all(w // d == ATTN_BLK for w, d in ATTN_GROUPS) and (SB // ATTN_BLK) % ATTN_BATCH == 0
    nsb = S // SB
    slopes = jnp.asarray(_alibi_slopes(), F32).T
    slopes = jnp.broadcast_to(slopes[:, :, None], (GROUP_HEADS, N_GROUPS, 2 * ATTN_BLK))

    def slab(g, comp):
        base = (g * 3 + comp) * GROUP_HEADS
        return pl.BlockSpec((None, SB, HEAD_DIM), lambda b, h, n: (base + h, b * nsb + n, 0))

    carry = [pltpu.VMEM((d, w // d, HEAD_DIM), BF16) if d >= ATTN_TRANSPOSE_DIL else pltpu.VMEM((w, HEAD_DIM), F32)
             for w, d in ATTN_GROUPS for _kv in range(2)]
    return pl.pallas_call(
        _attn_prompt_kernel,
        grid=(B, GROUP_HEADS, nsb),
        in_specs=[pl.BlockSpec((None, N_GROUPS, 2 * ATTN_BLK), lambda b, h, n: (h, 0, 0))]
        + [slab(g, comp) for g in range(N_GROUPS) for comp in range(3)],
        out_specs=pl.BlockSpec((None, SB, HEAD_DIM), lambda b, h, n: (h, b * nsb + n, 0)),
        out_shape=jax.ShapeDtypeStruct((GROUP_HEADS, M, HEAD_DIM), BF16),
        scratch_shapes=carry + [pltpu.VMEM((N_GROUPS, SB, HEAD_DIM), F32), pltpu.VMEM((N_GROUPS, SB, HEAD_DIM), F32)],
        compiler_params=_params(3),
        name="attn_prompt",
    )(slopes, *([qkv_slabs] * (3 * N_GROUPS)))


def _kv_tail_kernel(*refs):
    o_ref = refs[-1]
    for i, ref in enumerate(refs[:-1]):
        o_ref[i // 2, :, i % 2] = jnp.swapaxes(ref[...], 0, 1)


def kv_tails(qkv_layers, g, B, S, keep):
    chunk = min(keep, KV_TAIL_ROWS)
    assert keep % chunk == 0 and S % chunk == 0
    first = (S - keep) // chunk
    per_seq = S // chunk

    def slabs(comp):
        return pl.BlockSpec((GROUP_HEADS, chunk, HEAD_DIM), lambda b, t: (g * 3 + comp, b * per_seq + first + t, 0))

    n = len(qkv_layers)
    return pl.pallas_call(
        _kv_tail_kernel,
        grid=(B, keep // chunk),
        in_specs=[slabs(comp) for _ in range(n) for comp in (1, 2)],
        out_specs=pl.BlockSpec((n, None, chunk, 2, GROUP_HEADS, HEAD_DIM), lambda b, t: (0, b, t, 0, 0, 0)),
        out_shape=jax.ShapeDtypeStruct((n, B, keep, 2, GROUP_HEADS, HEAD_DIM), F32),
        compiler_params=_params(2),
        name="kv_tails",
    )(*[q for q in qkv_layers for _ in range(2)])


def _attn_sample_kernel(sl_ref, qkv_ref, c0_ref, c1_ref, c2_ref, o_ref):
    T = qkv_ref.shape[0]
    H = GROUP_HEADS
    caches = (c0_ref, c1_ref, c2_ref)
    rows = c0_ref.shape[0]
    l_idx = lax.broadcasted_iota(jnp.int32, (rows, H, 1), 0)
    l_f = l_idx.astype(F32)
    inv_sqrt = 1.0 / math.sqrt(HEAD_DIM)
    for t in range(T):
        outs, lses = [], []
        for g, (win, dil) in enumerate(ATTN_GROUPS):
            base = g * 3 * H
            slope = sl_ref[g][:, 0:1]
            q = qkv_ref[t, base:base + H, :]
            ph = 0 if dil == 1 else t
            kc = caches[g][:, ph * 2 * H:ph * 2 * H + H, :]
            vc = caches[g][:, ph * 2 * H + H:(ph + 1) * 2 * H, :]
            sc = jnp.sum(kc * q[None], axis=-1, keepdims=True) * inv_sqrt
            if dil == 1:
                sc = jnp.where(l_idx >= t, sc - slope[None] * (float(rows + t) - l_f), NEG)
                new_ts = list(range(t + 1))
            else:
                sc = sc - (slope[None] * float(dil)) * (float(rows) - l_f)
                new_ts = [t]
            m = jnp.max(sc, axis=0)
            s_new = []
            for t2 in new_ts:
                k2 = qkv_ref[t2, base + H:base + 2 * H, :]
                s2 = jnp.sum(q * k2, axis=-1, keepdims=True) * inv_sqrt - slope * float((t - t2) * dil)
                s_new.append(s2)
                m = jnp.maximum(m, s2)
            ec = jnp.exp(sc - m[None])
            den = jnp.sum(ec, axis=0)
            acc = jnp.sum(ec * vc, axis=0)
            for t2, s2 in zip(new_ts, s_new):
                v2 = qkv_ref[t2, base + 2 * H:base + 3 * H, :]
                e2 = jnp.exp(s2 - m)
                den = den + e2
                acc = acc + e2 * v2
            outs.append(acc / den)
            lses.append(m + jnp.log(den))
        mm = jnp.maximum(jnp.maximum(lses[0], lses[1]), lses[2])
        es = [jnp.exp(l - mm) for l in lses]
        o_ref[t] = (es[0] * outs[0] + es[1] * outs[1] + es[2] * outs[2]) / (es[0] + es[1] + es[2])


def attn_sample(qkv, caches, layer, T, Bs):
    H = GROUP_HEADS
    n_rows = qkv.shape[-1] // HEAD_DIM
    views, specs = [], []
    for (win, dil), c in zip(ATTN_GROUPS, caches):
        assert c.shape[1] == Bs and c.shape[2] == win and win // dil == ATTN_BLK and (dil == 1 or T <= dil)
        phases = min(dil, T)
        views.append(c.reshape(c.shape[0], Bs, win // dil, dil * 2 * H, HEAD_DIM))
        specs.append(pl.BlockSpec((None, None, win // dil, phases * 2 * H, HEAD_DIM), lambda b: (layer, b, 0, 0, 0)))
    slopes = jnp.broadcast_to(jnp.asarray(_alibi_slopes(), F32)[:, :, None], (N_GROUPS, H, HEAD_DIM))
    q_rows = jnp.swapaxes(qkv.reshape(T, Bs, n_rows, HEAD_DIM), 0, 1)
    out = pl.pallas_call(
        _attn_sample_kernel,
        grid=(Bs,),
        in_specs=[pl.BlockSpec((N_GROUPS, H, HEAD_DIM), lambda b: (0, 0, 0)),
                  pl.BlockSpec((None, T, n_rows, HEAD_DIM), lambda b: (b, 0, 0, 0)), *specs],
        out_specs=pl.BlockSpec((None, T, H, HEAD_DIM), lambda b: (b, 0, 0, 0)),
        out_shape=jax.ShapeDtypeStruct((Bs, T, H, HEAD_DIM), F32),
        compiler_params=_params(1),
        name="attn_sample",
    )(slopes, q_rows, *views)
    return jnp.swapaxes(out, 0, 1).reshape(T * Bs, ATTN_WIDTH)


class Tiles(NamedTuple):
    seq: int
    mm: int
    ffn: int
    tf: int


def _run_trunk(x, nseq, R, tiles, lru_h, lru_conv, kv_caches, ffn_conv, P, W, dims):
    depth = P["norm_mix"].shape[0]
    new_h, new_lconv, new_fconv, qkvs = [], [], [], []
    bf16_w = {name: [] for name in W}
    for layer in range(depth):
        j = layer // 2
        if layer % 2 == 0:
            proj, wb = norm_matmul(x, P["norm_mix"], layer, W["lru_w_in"][j], P["lru_b_in"], j, tm=tiles.mm)
            bf16_w["lru_w_in"].append(wb)
            hg, c_rows, h_rows = lru_core(proj, P["lru_conv_w"], P["lru_conv_b"], P["lru_w_a"], P["lru_b_a"],
                                          P["lru_w_i"], P["lru_b_i"], P["lru_lambda"], lru_conv[j], lru_h[j],
                                          layer=j, tm=tiles.seq, R=R)
            x, wb = matmul_res(hg, W["lru_w_out"][j], P["lru_b_out"], j, x, tm=tiles.mm)
            bf16_w["lru_w_out"].append(wb)
            new_h.append(h_rows)
            new_lconv.append(c_rows)
        else:
            prompt = kv_caches is None
            qkv, wb = norm_matmul(x, P["norm_mix"], layer, W["attn_w_qkv"][j], P["attn_b_zero"], j, tm=tiles.mm,
                                  slab_out=prompt)
            bf16_w["attn_w_qkv"].append(wb)
            qkvs.append(qkv)
            if prompt:
                o = attn_prompt(qkv, *dims)
            else:
                o = attn_sample(qkv, kv_caches, layer=j, T=dims[0], Bs=dims[1])
            x, wb = matmul_res(o, W["attn_w_o"][j], P["attn_bo_zero"], j, x, tm=tiles.mm)
            bf16_w["attn_w_o"].append(wb)
        x, f_rows, wbs = conv_ffn(x, P["norm_ffn"], W["ffn_w_gate"][layer], W["ffn_w_val"][layer], P["ffn_conv_w"],
                                  P["ffn_conv_b"], W["ffn_w_down"][layer], ffn_conv[layer], layer=layer,
                                  tm=tiles.ffn, tf=tiles.tf, R=R,
                                  final_gain=P["norm_final"] if layer == depth - 1 else None)
        for name, wb in zip(("ffn_w_gate", "ffn_w_val", "ffn_w_down"), wbs or (None,) * 3):
            bf16_w[name].append(wb)
        new_fconv.append(f_rows)
    return x, new_h, new_lconv, qkvs, new_fconv, bf16_w


def kernel(x_prompt, x_sample, cache_kv_w128, cache_kv_w512, cache_kv_w2048, state_lru_h, state_lru_conv, state_ffn_conv, norm_mix, norm_ffn, norm_final, lru_w_in, lru_b_in, lru_conv_w, lru_conv_b, lru_w_a, lru_b_a, lru_w_i, lru_b_i, lru_lambda, lru_w_out, lru_b_out, attn_w_qkv, attn_w_o, ffn_w_up, ffn_conv_w, ffn_conv_b, ffn_w_down):
    B, S, D = x_prompt.shape
    Bs, T, _ = x_sample.shape
    depth = norm_mix.shape[0]
    n_lru, W = lru_lambda.shape
    n_attn = attn_w_qkv.shape[0]
    F2 = ffn_w_up.shape[-1]
    assert Bs == SUBLANES, "the sample group is laid out time-major with one sublane per sequence"
    row3 = lambda a: a.reshape(a.shape[0], 1, a.shape[-1])
    P = {
        "norm_mix": row3(norm_mix), "norm_ffn": row3(norm_ffn), "norm_final": norm_final.reshape(1, D),
        "lru_b_in": row3(lru_b_in), "lru_conv_w": lru_conv_w, "lru_conv_b": row3(lru_conv_b),
        "lru_w_a": lru_w_a, "lru_b_a": row3(lru_b_a), "lru_w_i": lru_w_i, "lru_b_i": row3(lru_b_i),
        "lru_lambda": row3(lru_lambda), "lru_b_out": row3(lru_b_out),
        "attn_b_zero": jnp.zeros((n_attn, 1, attn_w_qkv.shape[-1]), F32),
        "attn_bo_zero": jnp.zeros((n_attn, 1, D), F32),
        "ffn_conv_w": ffn_conv_w, "ffn_conv_b": row3(ffn_conv_b),
    }
    stack = lambda w: [Weight(w, i) for i in range(w.shape[0])]
    W_f32 = {
        "lru_w_in": stack(lru_w_in), "lru_w_out": stack(lru_w_out),
        "attn_w_qkv": stack(attn_w_qkv), "attn_w_o": stack(attn_w_o),
        "ffn_w_gate": [(w, 0) for w in stack(ffn_w_up)], "ffn_w_val": [(w, F2 // 2) for w in stack(ffn_w_up)],
        "ffn_w_down": stack(ffn_w_down),
    }
    kc = lru_conv_w.shape[1] - 1
    kf = ffn_conv_w.shape[1] - 1

    tmaj = lambda a: jnp.swapaxes(a, 0, 1).reshape(1, a.shape[1] * a.shape[0], a.shape[-1])
    y_s, h_s, lc_s, qkv_s, fc_s, bf16_w = _run_trunk(
        jnp.swapaxes(x_sample, 0, 1).reshape(T * Bs, D), 1, Bs,
        Tiles(seq=T * Bs, mm=T * Bs, ffn=T * Bs, tf=_pick(F2 // 2, FFN_TILE_FEW_ROWS)),
        [state_lru_h[j].reshape(1, Bs, W) for j in range(n_lru)],
        [tmaj(state_lru_conv[j]) for j in range(n_lru)],
        (cache_kv_w128, cache_kv_w512, cache_kv_w2048),
        [tmaj(state_ffn_conv[l]) for l in range(depth)], P, W_f32, (T, Bs))

    W_bf16 = {name: [(Weight(w, 0), 0) if name in ("ffn_w_gate", "ffn_w_val") else Weight(w, 0) for w in ws]
              for name, ws in bf16_w.items()}

    row_tile = lambda cap: cap if S % cap == 0 else S
    tiles_p = Tiles(seq=row_tile(ROW_TILE_LRU), mm=row_tile(ROW_TILE_MATMUL), ffn=row_tile(ROW_TILE_MATMUL),
                    tf=_pick(F2 // 2, FFN_TILE))
    zeros = lambda n, w: [jnp.zeros((B, SUBLANES, w), F32)] * n
    y_p, h_p, lc_p, qkv_p, fc_p, _ = _run_trunk(
        x_prompt.reshape(B * S, D), B, 1, tiles_p, zeros(n_lru, W), zeros(n_lru, W), None, zeros(depth, F2), P, W_bf16, (B, S))

    bmaj = lambda a, k: jnp.swapaxes(a.reshape(-1, Bs, a.shape[-1])[-k:], 0, 1)
    kv_p, kv_s = [], []
    for g, (win, dil) in enumerate(ATTN_GROUPS):
        lo = (g * 3 + 1) * ATTN_WIDTH
        keep = min(win, S)
        kv_p.append(kv_tails(qkv_p, g, B, S, keep))
        kv_s.append(jnp.stack([jnp.swapaxes(q.reshape(T, Bs, -1), 0, 1)[:, :, lo:lo + 2 * ATTN_WIDTH]
                               .reshape(Bs, T, 2, GROUP_HEADS, HEAD_DIM) for q in qkv_s], axis=0))
    return (
        y_p.reshape(B, S, D),
        jnp.swapaxes(y_s.reshape(T, Bs, D), 0, 1),
        kv_p[0], kv_p[1], kv_p[2],
        jnp.stack([h[:, SUBLANES - 1] for h in h_p], axis=0),
        jnp.stack([c[:, SUBLANES - kc:] for c in lc_p], axis=0),
        jnp.stack([f[:, SUBLANES - kf:] for f in fc_p], axis=0),
        kv_s[0], kv_s[1], kv_s[2],
        jnp.stack([h[0] for h in h_s], axis=0),
        jnp.stack([bmaj(c[0], kc) for c in lc_s], axis=0),
        jnp.stack([bmaj(f[0], kf) for f in fc_s], axis=0),
    )
```
